```python
import jax, jax.numpy as jnp
from jax import lax
import numpy as np

D_MODEL = 1024
BATCH = 8
SEQ = 4096
DEPTH = 1

CHUNK = 64
ATTN_HEADS = 8
ATTN_KV_HEADS = 2
HEAD_DIM = 64
ATTN_WIDTH = ATTN_HEADS * HEAD_DIM
KV_WIDTH = ATTN_KV_HEADS * HEAD_DIM
ATTN_SCALE = HEAD_DIM ** -0.5
IDX_HEADS = 4
IDX_DIM = 64
IDX_ROPE_DIM = 32
IDX_SCALE = (IDX_HEADS ** -0.5) * (IDX_DIM ** -0.5)
IDX_TOPK_MAX = 256
QUERY_BLOCK = 128
POOL_GROUPS = 4
POOL_WINDOWS = (2, 4, 8, 16)
POOL_WIDTH = 512
POOL_GROUP_DIM = POOL_WIDTH // POOL_GROUPS
N_EXPERTS = 32
TOP_K = 4
EXPERT_DIM = D_MODEL
SWIGLU_ALPHA = 1.702
SWIGLU_LIMIT = 7.0
EXPERT_BLOCK = 256
ROPE_THETA = 10000.0
NORM_EPS = 1e-6
IN_SPLITS = (ATTN_WIDTH, KV_WIDTH, KV_WIDTH, IDX_HEADS * IDX_DIM, IDX_DIM, IDX_HEADS, POOL_WIDTH, D_MODEL, D_MODEL)
IN_WIDTH = ATTN_WIDTH + 2 * KV_WIDTH + IDX_HEADS * IDX_DIM + IDX_DIM + IDX_HEADS + POOL_WIDTH + 2 * D_MODEL

kernel_name = "hybrid_dsa_pool_moe_block"


def rms_norm(x, g):
    xf = x.astype(jnp.float32)
    y = xf * lax.rsqrt(jnp.mean(xf * xf, axis=-1, keepdims=True) + NORM_EPS)
    return (y * g.astype(jnp.float32)).astype(x.dtype)


def layer_norm(x, g, b):
    xf = x.astype(jnp.float32)
    mu = jnp.mean(xf, axis=-1, keepdims=True)
    var = jnp.mean(jnp.square(xf - mu), axis=-1, keepdims=True)
    y = (xf - mu) * lax.rsqrt(var + NORM_EPS)
    return (y * g.astype(jnp.float32) + b.astype(jnp.float32)).astype(x.dtype)


def rope(x, positions):
    d = x.shape[-1]
    inv_freq = ROPE_THETA ** (-jnp.arange(0, d, 2, dtype=jnp.float32) / d)
    ang = positions.astype(jnp.float32)[:, :, None, None] * inv_freq
    cos, sin = jnp.cos(ang), jnp.sin(ang)
    xf = x.astype(jnp.float32)
    x1, x2 = xf[..., : d // 2], xf[..., d // 2:]
    return jnp.concatenate([x1 * cos - x2 * sin, x2 * cos + x1 * sin], axis=-1).astype(x.dtype)


def dsa_attention(q, k, v, q_idx, k_idx, w_idx):
    b, s = q.shape[0], q.shape[1]
    n_sel = min(IDX_TOPK_MAX, s // 4)
    n_blocks = s // QUERY_BLOCK
    groups = ATTN_HEADS // ATTN_KV_HEADS
    key_pos = jnp.arange(s, dtype=jnp.int32)
    k_idx_f = k_idx.astype(jnp.float32)

    def to_blocks(a):
        return jnp.moveaxis(a.reshape((b, n_blocks, QUERY_BLOCK) + a.shape[2:]), 1, 0)

    q_b = to_blocks(q.reshape(b, s, ATTN_KV_HEADS, groups, HEAD_DIM))
    qi_b = to_blocks(q_idx)
    w_b = to_blocks(w_idx)
    starts = jnp.arange(n_blocks, dtype=jnp.int32) * QUERY_BLOCK

    def block_fn(args):
        qb, qib, wb, t0 = args
        t = t0 + jnp.arange(QUERY_BLOCK, dtype=jnp.int32)
        chunk_end = (t // CHUNK + 1) * CHUNK
        logits = jnp.einsum('bthd,bsd->bths', qib.astype(jnp.float32), k_idx_f)
        score = jnp.einsum('bths,bth->bts', jax.nn.relu(logits), wb.astype(jnp.float32)) * IDX_SCALE
        admissible = key_pos[None, :] < chunk_end[:, None]
        score = jnp.where(admissible[None], score, -jnp.inf)
        _, sel = lax.top_k(score, n_sel)
        valid = sel < chunk_end[None, :, None]
        k_sel = jax.vmap(lambda kk, ii: kk[ii])(k, sel)
        v_sel = jax.vmap(lambda vv, ii: vv[ii])(v, sel)
        att = jnp.einsum('btrgd,btnrd->btrgn', qb.astype(jnp.float32), k_sel.astype(jnp.float32)) * ATTN_SCALE
        att = jnp.where(valid[:, :, None, None, :], att, -jnp.inf)
        prob = jax.nn.softmax(att, axis=-1)
        return jnp.einsum('btrgn,btnrd->btrgd', prob.astype(v.dtype), v_sel)

    out = lax.map(block_fn, (q_b, qi_b, w_b, starts))
    return jnp.moveaxis(out, 0, 1).reshape(b, s, ATTN_WIDTH)


def multiscale_pool(p, pool_mix_w, pool_scale):
    b, s, _ = p.shape
    pf = p.astype(jnp.float32).reshape(b, s, POOL_GROUPS, POOL_GROUP_DIM)
    csum = jnp.concatenate([jnp.zeros_like(pf[:, :1]), jnp.cumsum(pf, axis=1)], axis=1)
    t = jnp.arange(s, dtype=jnp.int32)
    means = []
    for g, w in enumerate(POOL_WINDOWS):
        lo = jnp.maximum(t + 1 - w, 0)
        cnt = jnp.minimum(t + 1, w).astype(jnp.float32)
        means.append((csum[:, 1:, g] - csum[:, lo, g]) / cnt[None, :, None])
    mean = jnp.stack(means, axis=2)
    mixed = (mean - pf).astype(p.dtype)
    mixed = jnp.einsum('bsgc,gcd->bsgd', mixed, pool_mix_w)
    return mixed.reshape(b, s, POOL_WIDTH) * pool_scale


def moe_ffn(h, router_w, router_b, w_gate_up, b_gate_up, w_down, b_down):
    b, s, d = h.shape
    n_tok = b * s
    hf = h.reshape(n_tok, d)
    logits = (hf @ router_w + router_b).astype(jnp.float32)
    top_vals, top_idx = lax.top_k(logits, TOP_K)
    gates = jax.nn.softmax(top_vals, axis=-1)
    n_assign = n_tok * TOP_K
    expert_flat = top_idx.reshape(-1)
    token_flat = jnp.arange(n_assign, dtype=jnp.int32) // TOP_K
    gate_flat = gates.reshape(-1)
    order = jnp.argsort(expert_flat)
    e_sorted = expert_flat[order]
    counts = jnp.bincount(expert_flat, length=N_EXPERTS)
    group_start = jnp.cumsum(counts) - counts
    padded = (counts + EXPERT_BLOCK - 1) // EXPERT_BLOCK * EXPERT_BLOCK
    pad_end = jnp.cumsum(padded)
    pad_start = pad_end - padded
    dest = pad_start[e_sorted] + (jnp.arange(n_assign, dtype=jnp.int32) - group_start[e_sorted])
    n_blocks = -(-(n_assign + N_EXPERTS * (EXPERT_BLOCK - 1)) // EXPERT_BLOCK)
    n_rows = n_blocks * EXPERT_BLOCK
    slot_token = jnp.full((n_rows,), n_tok, jnp.int32).at[dest].set(token_flat[order])
    slot_gate = jnp.zeros((n_rows,), jnp.float32).at[dest].set(gate_flat[order])
    block_expert = jnp.minimum(
        jnp.searchsorted(pad_end, jnp.arange(n_blocks, dtype=jnp.int32) * EXPERT_BLOCK, side='right'),
        N_EXPERTS - 1)
    h_pad = jnp.concatenate([hf, jnp.zeros((1, d), hf.dtype)], axis=0)

    def run_block(args):
        tok, e = args
        xb = h_pad[tok]
        gu = xb @ w_gate_up[e] + b_gate_up[e]
        gate = jnp.minimum(gu[:, :EXPERT_DIM], SWIGLU_LIMIT)
        up = jnp.clip(gu[:, EXPERT_DIM:], -SWIGLU_LIMIT, SWIGLU_LIMIT)
        act = gate * jax.nn.sigmoid(SWIGLU_ALPHA * gate) * (up + 1.0)
        return act @ w_down[e] + b_down[e]

    out = lax.map(run_block, (slot_token.reshape(n_blocks, EXPERT_BLOCK), block_expert))
    out = out.reshape(n_rows, d) * slot_gate[:, None].astype(out.dtype)
    y = jax.ops.segment_sum(out, slot_token, num_segments=n_tok + 1)[:n_tok]
    return y.reshape(b, s, d)


def setup_inputs(seed: int = 0) -> dict:
    key = jax.random.key(seed)
    ks = jax.random.split(key, 24)
    L = DEPTH

    def normal(k, shape, scale):
        return jax.random.normal(k, shape, jnp.float32) * scale

    x = normal(ks[0], (BATCH, SEQ, D_MODEL), 1.0)
    offsets = jax.random.randint(ks[1], (BATCH, 1), 0, 16, dtype=jnp.int32) * CHUNK
    positions = jnp.arange(SEQ, dtype=jnp.int32)[None, :] + offsets
    return {
        "x": x,
        "positions": positions,
        "norm1_g": 1.0 + normal(ks[2], (L, D_MODEL), 0.05),
        "w_in": normal(ks[3], (L, D_MODEL, IN_WIDTH), D_MODEL ** -0.5),
        "q_norm_g": 1.0 + normal(ks[4], (L, HEAD_DIM), 0.05),
        "k_norm_g": 1.0 + normal(ks[5], (L, HEAD_DIM), 0.05),
        "idx_k_norm_g": 1.0 + normal(ks[6], (L, IDX_DIM), 0.05),
        "idx_k_norm_b": normal(ks[7], (L, IDX_DIM), 0.05),
        "w_branch_attn": normal(ks[8], (L, ATTN_WIDTH, D_MODEL), ATTN_WIDTH ** -0.5),
        "pool_mix_w": normal(ks[9], (L, POOL_GROUPS, POOL_GROUP_DIM, POOL_GROUP_DIM), POOL_GROUP_DIM ** -0.5),
        "pool_scale": 1.0 + normal(ks[10], (L, POOL_WIDTH), 0.05),
        "w_branch_pool": normal(ks[11], (L, POOL_WIDTH, D_MODEL), POOL_WIDTH ** -0.5),
        "w_out": normal(ks[12], (L, D_MODEL, D_MODEL), D_MODEL ** -0.5),
        "norm2_g": 1.0 + normal(ks[13], (L, D_MODEL), 0.05),
        "router_w": normal(ks[14], (L, D_MODEL, N_EXPERTS), D_MODEL ** -0.5),
        "router_b": normal(ks[15], (L, N_EXPERTS), 0.01),
        "w_gate_up": normal(ks[16], (L, N_EXPERTS, D_MODEL, 2 * EXPERT_DIM), D_MODEL ** -0.5),
        "b_gate_up": normal(ks[17], (L, N_EXPERTS, 2 * EXPERT_DIM), 0.02),
        "w_down": normal(ks[18], (L, N_EXPERTS, EXPERT_DIM, D_MODEL), EXPERT_DIM ** -0.5),
        "b_down": normal(ks[19], (L, N_EXPERTS, D_MODEL), 0.02),
    }


def reference(x, positions, norm1_g, w_in, q_norm_g, k_norm_g, idx_k_norm_g, idx_k_norm_b,
              w_branch_attn, pool_mix_w, pool_scale, w_branch_pool, w_out, norm2_g,
              router_w, router_b, w_gate_up, b_gate_up, w_down, b_down):
    b, s, _ = x.shape
    split_at = np.cumsum(IN_SPLITS)[:-1].tolist()
    for l in range(DEPTH):
        h = rms_norm(x, norm1_g[l])
        proj = h @ w_in[l]
        q, k, v, qi, ki, wi, pool_in, gate_a, gate_p = jnp.split(proj, split_at, axis=-1)
        q = rope(rms_norm(q.reshape(b, s, ATTN_HEADS, HEAD_DIM), q_norm_g[l]), positions)
        k = rope(rms_norm(k.reshape(b, s, ATTN_KV_HEADS, HEAD_DIM), k_norm_g[l]), positions)
        v = v.reshape(b, s, ATTN_KV_HEADS, HEAD_DIM)
        qi = qi.reshape(b, s, IDX_HEADS, IDX_DIM)
        qi = jnp.concatenate([rope(qi[..., :IDX_ROPE_DIM], positions), qi[..., IDX_ROPE_DIM:]], axis=-1)
        ki = layer_norm(ki, idx_k_norm_g[l], idx_k_norm_b[l])[:, :, None, :]
        ki = jnp.concatenate([rope(ki[..., :IDX_ROPE_DIM], positions), ki[..., IDX_ROPE_DIM:]], axis=-1)[:, :, 0]
        attn = dsa_attention(q, k, v, qi, ki, wi)
        pooled = multiscale_pool(pool_in, pool_mix_w[l], pool_scale[l])
        merged = (jax.nn.sigmoid(gate_a) * (attn @ w_branch_attn[l])
                  + jax.nn.sigmoid(gate_p) * (pooled @ w_branch_pool[l]))
        x = x + merged @ w_out[l]
        h2 = rms_norm(x, norm2_g[l])
        x = x + moe_ffn(h2, router_w[l], router_b[l], w_gate_up[l], b_gate_up[l], w_down[l], b_down[l])
    return x
```

```python
import functools

import numpy as np
import jax
import jax.numpy as jnp
from jax import lax
from jax.experimental import pallas as pl
from jax.experimental.pallas import tpu as pltpu

F32 = jnp.float32
BF16 = jnp.bfloat16
I32 = jnp.int32

D_MODEL = 1024
CHUNK = 64
ATTN_HEADS = 8
ATTN_KV_HEADS = 2
HEAD_DIM = 64
GROUPS = ATTN_HEADS // ATTN_KV_HEADS
ATTN_WIDTH = ATTN_HEADS * HEAD_DIM
KV_WIDTH = ATTN_KV_HEADS * HEAD_DIM
ATTN_SCALE = HEAD_DIM ** -0.5
IDX_HEADS = 4
IDX_DIM = 64
IDX_ROPE_DIM = 32
IDX_SCALE = (IDX_HEADS ** -0.5) * (IDX_DIM ** -0.5)
IDX_TOPK_MAX = 256
QUERY_BLOCK = 128
POOL_WINDOWS = (2, 4, 8, 16)
POOL_WIDTH = 512
POOL_GROUP_DIM = 128
POOL_HALO = 16
N_EXPERTS = 32
TOP_K = 4
EXPERT_DIM = 1024
SWIGLU_ALPHA = 1.702
SWIGLU_LIMIT = 7.0
ROPE_THETA = 10000.0
NORM_EPS = 1e-6

LANES = 128
VMEM_LIMIT = 56 * 1024 * 1024
FLT_MAX = float(np.finfo(np.float32).max)
MASKED = -1e30

C_Q = 0
C_K = 512
C_V = 640
C_QI = 768
C_KIW = 1024
C_POOL = 1152
C_GATE = 1664
W_PACKED = 3712
C_SMALL_END = C_POOL

ROW_BLOCK = 512
VALUE_BISECT_ITERS = 16
BISECT_CAP = 64


def _cparams(*sem):
    return pltpu.CompilerParams(dimension_semantics=sem, vmem_limit_bytes=VMEM_LIMIT)


def _swap_halves(xc, first, half):
    return jnp.where(first, pltpu.roll(xc, LANES - half, 1), pltpu.roll(xc, half, 1))


def _proj_kernel(x_ref, pos_ref, g1_ref, w_ref, rows_ref, gsum_ref,
                 q_ref, kt_ref, v_ref, qi_ref, kit_ref, wi_ref, pool_ref, gate_ref):
    x = x_ref[0]
    tm = x.shape[0]
    ms = jnp.mean(x * x, axis=-1, keepdims=True)
    h = (x * lax.rsqrt(ms + NORM_EPS) * g1_ref[...]).astype(BF16)
    d1 = jnp.dot(h, w_ref[:, 0:C_SMALL_END], preferred_element_type=F32)
    pool_ref[0] = jnp.dot(h, w_ref[:, C_POOL:C_GATE], preferred_element_type=F32)
    gate_ref[0] = jnp.dot(h, w_ref[:, C_GATE:W_PACKED], preferred_element_type=F32)

    pos = pos_ref[0].astype(F32)
    rows = rows_ref[...]
    lane = lax.broadcasted_iota(I32, (tm, LANES), 1)
    first_m = (lane & (HEAD_DIM - 1)) < HEAD_DIM // 2
    first_i = (lane & (IDX_ROPE_DIM - 1)) < IDX_ROPE_DIM // 2
    ang_m = pos * rows[0:1]
    cos_m = jnp.cos(ang_m)
    sin_m = jnp.sin(ang_m) * rows[1:2]
    ang_i = pos * rows[2:3]
    cos_i = jnp.cos(ang_i)
    sin_i = jnp.sin(ang_i) * rows[3:4]
    gsum = gsum_ref[...]

    def head_rms(xc, grow):
        sq = xc * xc
        hi = sq.astype(BF16)
        lo = (sq - hi.astype(F32)).astype(BF16)
        ssum = (jnp.dot(hi, gsum, preferred_element_type=F32)
                + jnp.dot(lo, gsum, preferred_element_type=F32))
        return xc * lax.rsqrt(ssum * (1.0 / HEAD_DIM) + NORM_EPS) * grow

    def rope_m(xc):
        return xc * cos_m + _swap_halves(xc, first_m, HEAD_DIM // 2) * sin_m

    def rope_i(xc):
        return xc * cos_i + _swap_halves(xc, first_i, IDX_ROPE_DIM // 2) * sin_i

    for c in range(ATTN_WIDTH // LANES):
        qc = d1[:, C_Q + c * LANES:C_Q + (c + 1) * LANES]
        qr = rope_m(head_rms(qc, rows[4:5])) * ATTN_SCALE
        q_ref[0, :, c * LANES:(c + 1) * LANES] = qr.astype(BF16)

    kr = rope_m(head_rms(d1[:, C_K:C_K + KV_WIDTH], rows[5:6]))
    krt = kr.T
    for r in range(ATTN_KV_HEADS):
        kt_ref[0, r] = krt[r * HEAD_DIM:(r + 1) * HEAD_DIM].astype(BF16)
        v_ref[0, r] = d1[:, C_V + r * HEAD_DIM:C_V + (r + 1) * HEAD_DIM].astype(BF16)

    for c in range(IDX_HEADS * IDX_DIM // LANES):
        qc = d1[:, C_QI + c * LANES:C_QI + (c + 1) * LANES]
        qi_ref[0, :, c * LANES:(c + 1) * LANES] = rope_i(qc).astype(BF16)

    kiw = d1[:, C_KIW:C_KIW + LANES]
    in_ki = lane < IDX_DIM
    mu = jnp.sum(jnp.where(in_ki, kiw, 0.0), axis=-1, keepdims=True) * (1.0 / IDX_DIM)
    dv = jnp.where(in_ki, kiw - mu, 0.0)
    var = jnp.sum(dv * dv, axis=-1, keepdims=True) * (1.0 / IDX_DIM)
    kin = dv * lax.rsqrt(var + NORM_EPS) * rows[6:7] + rows[7:8]
    kit_ref[0] = rope_i(kin).T[0:IDX_DIM].astype(BF16)
    wi_ref[0] = kiw * IDX_SCALE


def _proj_call(x, pos3, g1, w_packed, rows, gsum):
    b, s, d = x.shape
    tm = min(512, s)
    grid = (b, s // tm)
    full2 = lambda i, j: (0, 0)
    out_shape = (
        jax.ShapeDtypeStruct((b, s, ATTN_WIDTH), BF16),
        jax.ShapeDtypeStruct((b, ATTN_KV_HEADS, HEAD_DIM, s), BF16),
        jax.ShapeDtypeStruct((b, ATTN_KV_HEADS, s, HEAD_DIM), BF16),
        jax.ShapeDtypeStruct((b, s, IDX_HEADS * IDX_DIM), BF16),
        jax.ShapeDtypeStruct((b, IDX_DIM, s), BF16),
        jax.ShapeDtypeStruct((b, s, LANES), F32),
        jax.ShapeDtypeStruct((b, s, POOL_WIDTH), F32),
        jax.ShapeDtypeStruct((b, s, 2 * D_MODEL), F32),
    )
    in_specs = [
        pl.BlockSpec((1, tm, d), lambda i, j: (i, j, 0)),
        pl.BlockSpec((1, tm, 1), lambda i, j: (i, j, 0)),
        pl.BlockSpec((1, d), full2),
        pl.BlockSpec((d, W_PACKED), full2),
        pl.BlockSpec((8, LANES), full2),
        pl.BlockSpec((LANES, LANES), full2),
    ]
    out_specs = (
        pl.BlockSpec((1, tm, ATTN_WIDTH), lambda i, j: (i, j, 0)),
        pl.BlockSpec((1, ATTN_KV_HEADS, HEAD_DIM, tm), lambda i, j: (i, 0, 0, j)),
        pl.BlockSpec((1, ATTN_KV_HEADS, tm, HEAD_DIM), lambda i, j: (i, 0, j, 0)),
        pl.BlockSpec((1, tm, IDX_HEADS * IDX_DIM), lambda i, j: (i, j, 0)),
        pl.BlockSpec((1, IDX_DIM, tm), lambda i, j: (i, 0, j)),
        pl.BlockSpec((1, tm, LANES), lambda i, j: (i, j, 0)),
        pl.BlockSpec((1, tm, POOL_WIDTH), lambda i, j: (i, j, 0)),
        pl.BlockSpec((1, tm, 2 * D_MODEL), lambda i, j: (i, j, 0)),
    )
    return pl.pallas_call(
        _proj_kernel, grid=grid, in_specs=in_specs, out_specs=out_specs, out_shape=out_shape,
        compiler_params=_cparams("parallel", "parallel"), name="proj",
    )(x, pos3, g1, w_packed, rows, gsum)


def _sort_key(x):
    b = lax.bitcast_convert_type(x, I32)
    return b ^ ((b >> 31) & 0x7FFFFFFF)


def _unsort_key(k):
    return lax.bitcast_convert_type(k ^ ((k >> 31) & 0x7FFFFFFF), F32)


def _attn_kernel(q_ref, qi_ref, wi_ref, kit_ref, kt_ref, v_ref, o_ref, sc_ref, *, n_sel, ck):
    qb = QUERY_BLOCK
    ngrp = ck // LANES
    t0 = pl.program_id(1) * qb
    nck = (t0 + qb + ck - 1) // ck
    row = lax.broadcasted_iota(I32, (qb, 1), 0)
    cend = (((t0 + row) >> (CHUNK.bit_length() - 1)) + 1) * CHUNK
    lane_ck = lax.broadcasted_iota(I32, (qb, ck), 1)
    lane_g = lax.broadcasted_iota(I32, (qb, LANES), 1)
    k_sel = float(n_sel)

    qi = qi_ref[0]
    qst = jnp.concatenate([qi[:, h * IDX_DIM:(h + 1) * IDX_DIM] for h in range(IDX_HEADS)], axis=0)
    wi = wi_ref[0]
    wcols = [wi[:, IDX_DIM + h:IDX_DIM + h + 1] for h in range(IDX_HEADS)]

    def score_body(c, carry):
        mxp, mnp, c0p, c1p = carry
        off = pl.multiple_of(c * ck, ck)
        lg = jnp.dot(qst, kit_ref[0, :, pl.ds(off, ck)], preferred_element_type=F32)
        sc = jnp.maximum(lg[0:qb], 0.0) * wcols[0]
        for h in range(1, IDX_HEADS):
            sc = sc + jnp.maximum(lg[h * qb:(h + 1) * qb], 0.0) * wcols[h]
        adm = (off + lane_ck) < cend
        sc = jnp.where(adm, sc, -jnp.inf)
        sc_ref[:, pl.ds(off, ck)] = sc
        for g in range(ngrp):
            sg = sc[:, g * LANES:(g + 1) * LANES]
            ag = adm[:, g * LANES:(g + 1) * LANES]
            mxp = jnp.maximum(mxp, sg)
            mnp = jnp.minimum(mnp, jnp.where(ag, sg, jnp.inf))
            c0p = c0p + jnp.where(sg >= 0.0, 1.0, 0.0)
            c1p = c1p + jnp.where(sg > 0.0, 1.0, 0.0)
        return mxp, mnp, c0p, c1p

    init = (jnp.full((qb, LANES), -jnp.inf, F32), jnp.full((qb, LANES), jnp.inf, F32),
            jnp.zeros((qb, LANES), F32), jnp.zeros((qb, LANES), F32))
    mxp, mnp, c0p, c1p = lax.fori_loop(0, nck, score_body, init)
    mx = jnp.max(mxp, axis=1, keepdims=True)
    mn = jnp.min(mnp, axis=1, keepdims=True)
    c0 = jnp.sum(c0p, axis=1, keepdims=True)
    c1 = jnp.sum(c1p, axis=1, keepdims=True)

    def count(pred):
        def body(c, acc):
            off = pl.multiple_of(c * ck, ck)
            blk = sc_ref[:, pl.ds(off, ck)]
            for g in range(ngrp):
                hit = pred(blk[:, g * LANES:(g + 1) * LANES], off + g * LANES + lane_g)
                acc = acc + jnp.where(hit, 1.0, 0.0)
            return acc
        acc = lax.fori_loop(0, nck, body, jnp.zeros((qb, LANES), F32))
        return jnp.sum(acc, axis=1, keepdims=True)

    small = cend.astype(F32) <= k_sel
    at_zero = jnp.logical_and(jnp.logical_not(small), jnp.logical_and(c1 < k_sel, c0 >= k_sel))
    positive = jnp.logical_and(jnp.logical_not(small), c1 >= k_sel)
    lo0 = jnp.where(positive, 0.0, mn)
    hi0 = jnp.where(positive, jnp.minimum(2.0 * mx, FLT_MAX), 0.0)
    thr0 = jnp.where(small, -FLT_MAX, 0.0)
    done0 = jnp.where(jnp.logical_or(small, at_zero), 1.0, 0.0)
    tie0 = jnp.where(jnp.logical_and(at_zero, c0 > k_sel), 1.0, 0.0)
    left0 = jnp.sum(1.0 - done0)

    def bisect_cond(st):
        it, left = st[0], st[1]
        return jnp.logical_and(it < BISECT_CAP, left > 0.0)

    def bisect_body(st):
        it, _, lo, hi, thr, done, tie = st
        mid_v = 0.5 * lo + 0.5 * hi
        klo = _sort_key(lo)
        khi = _sort_key(hi)
        mid_k = _unsort_key((klo & khi) + ((klo ^ khi) >> 1))
        mid = jnp.where(it < VALUE_BISECT_ITERS, mid_v, mid_k)
        stuck = jnp.logical_or(mid <= lo, mid >= hi)
        midb = jnp.broadcast_to(mid, (qb, LANES))
        cnt = count(lambda s, _: s >= midb)
        active = done == 0.0
        moving = jnp.logical_and(active, jnp.logical_not(stuck))
        hit = jnp.logical_and(moving, cnt == k_sel)
        new_tie = jnp.logical_and(active, stuck)
        thr = jnp.where(hit, mid, jnp.where(new_tie, lo, thr))
        tie = jnp.where(new_tie, 1.0, tie)
        done = jnp.where(jnp.logical_or(hit, new_tie), 1.0, done)
        upd = jnp.logical_and(moving, jnp.logical_not(hit))
        lo = jnp.where(jnp.logical_and(upd, cnt >= k_sel), mid, lo)
        hi = jnp.where(jnp.logical_and(upd, cnt < k_sel), mid, hi)
        return it + 1, jnp.sum(1.0 - done), lo, hi, thr, done, tie

    st = lax.while_loop(bisect_cond, bisect_body, (jnp.int32(0), left0, lo0, hi0, thr0, done0, tie0))
    thr, tie = st[4], st[6]

    @pl.when(jnp.sum(tie) > 0.0)
    def _():
        thrb = jnp.broadcast_to(thr, (qb, LANES))
        need = k_sel - count(lambda s, _: s > thrb)

        def bit_body(i, jsel):
            cand = jsel + lax.shift_left(jnp.int32(1), (sc_ref.shape[1].bit_length() - 1) - i)
            candb = jnp.broadcast_to(cand, (qb, LANES))
            f = count(lambda s, kp: jnp.logical_and(s == thrb, kp < candb))
            return jnp.where(f <= need, cand, jsel)

        jsel = lax.fori_loop(0, sc_ref.shape[1].bit_length(), bit_body, jnp.zeros((qb, 1), I32))
        jsel = jnp.where(tie > 0.0, jsel, jnp.int32(2 ** 30))

        def drop_body(c, carry):
            off = pl.multiple_of(c * ck, ck)
            blk = sc_ref[:, pl.ds(off, ck)]
            drop = jnp.logical_and(blk == thr, (off + lane_ck) >= jsel)
            sc_ref[:, pl.ds(off, ck)] = jnp.where(drop, -jnp.inf, blk)
            return carry

        lax.fori_loop(0, nck, drop_body, 0)

    q = q_ref[0]
    outs = []
    for r in range(ATTN_KV_HEADS):
        qs = jnp.concatenate(
            [q[:, (r * GROUPS + g) * HEAD_DIM:(r * GROUPS + g + 1) * HEAD_DIM] for g in range(GROUPS)],
            axis=0)

        def att_body(c, carry, r=r, qs=qs):
            m, l, acc = carry
            off = pl.multiple_of(c * ck, ck)
            s = jnp.dot(qs, kt_ref[0, r, :, pl.ds(off, ck)], preferred_element_type=F32)
            bias = jnp.where(sc_ref[:, pl.ds(off, ck)] >= thr, 0.0, MASKED)
            s = (s.reshape(GROUPS, qb, ck) + bias[None]).reshape(GROUPS * qb, ck)
            m_new = jnp.maximum(m, jnp.max(s, axis=1, keepdims=True))
            alpha = jnp.exp(m - m_new)
            p = jnp.exp(s - m_new)
            l = alpha * l + jnp.sum(p, axis=1, keepdims=True)
            pv = jnp.dot(p.astype(BF16), v_ref[0, r, pl.ds(off, ck), :], preferred_element_type=F32)
            return m_new, l, alpha * acc + pv

        init_a = (jnp.full((GROUPS * qb, 1), MASKED, F32), jnp.zeros((GROUPS * qb, 1), F32),
                  jnp.zeros((GROUPS * qb, HEAD_DIM), F32))
        _, l, acc = lax.fori_loop(0, nck, att_body, init_a)
        o = acc / l
        outs.extend(o[g * qb:(g + 1) * qb] for g in range(GROUPS))
    o_ref[0] = jnp.concatenate(outs, axis=1).astype(BF16)


def _attn_call(q, kt, v, qi, kit, wi):
    b, s, _ = q.shape
    n_sel = min(IDX_TOPK_MAX, s // 4)
    ck = min(512, s)
    grid = (b, s // QUERY_BLOCK)
    in_specs = [
        pl.BlockSpec((1, QUERY_BLOCK, ATTN_WIDTH), lambda i, j: (i, j, 0)),
        pl.BlockSpec((1, QUERY_BLOCK, IDX_HEADS * IDX_DIM), lambda i, j: (i, j, 0)),
        pl.BlockSpec((1, QUERY_BLOCK, LANES), lambda i, j: (i, j, 0)),
        pl.BlockSpec((1, IDX_DIM, s), lambda i, j: (i, 0, 0)),
        pl.BlockSpec((1, ATTN_KV_HEADS, HEAD_DIM, s), lambda i, j: (i, 0, 0, 0)),
        pl.BlockSpec((1, ATTN_KV_HEADS, s, HEAD_DIM), lambda i, j: (i, 0, 0, 0)),
    ]
    return pl.pallas_call(
        functools.partial(_attn_kernel, n_sel=n_sel, ck=ck),
        grid=grid, in_specs=in_specs,
        out_specs=pl.BlockSpec((1, QUERY_BLOCK, ATTN_WIDTH), lambda i, j: (i, j, 0)),
        out_shape=jax.ShapeDtypeStruct((b, s, ATTN_WIDTH), BF16),
        scratch_shapes=[pltpu.VMEM((QUERY_BLOCK, s), F32)],
        compiler_params=_cparams("parallel", "parallel"), name="attn",
    )(q, qi, wi, kit, kt, v)


def _merge_kernel(x_ref, attn_ref, pool_ref, gate_ref, wba_ref, mix_ref, pscale_ref, wbp_ref, wout_ref,
                  o_ref, ext_ref):
    tm = x_ref.shape[1]
    j = pl.program_id(1)

    @pl.when(j == 0)
    def _():
        ext_ref[0:POOL_HALO] = jnp.zeros((POOL_HALO, POOL_WIDTH), F32)

    @pl.when(j > 0)
    def _():
        ext_ref[0:POOL_HALO] = ext_ref[tm:tm + POOL_HALO]

    ext_ref[POOL_HALO:POOL_HALO + tm] = pool_ref[0]
    t = j * tm + lax.broadcasted_iota(I32, (tm, 1), 0)
    mixed = []
    for g, w in enumerate(POOL_WINDOWS):
        cols = slice(g * POOL_GROUP_DIM, (g + 1) * POOL_GROUP_DIM)
        cur = ext_ref[POOL_HALO:POOL_HALO + tm, cols]
        wsum = cur
        for i in range(1, w):
            wsum = wsum + ext_ref[POOL_HALO - i:POOL_HALO - i + tm, cols]
        cnt = jnp.minimum(t + 1, w).astype(F32)
        dev = (wsum / cnt - cur).astype(BF16)
        mixed.append(jnp.dot(dev, mix_ref[g], preferred_element_type=F32))
    pooled = (jnp.concatenate(mixed, axis=1) * pscale_ref[...]).astype(BF16)
    branch_a = jnp.dot(attn_ref[0], wba_ref[...], preferred_element_type=F32)
    branch_p = jnp.dot(pooled, wbp_ref[...], preferred_element_type=F32)
    gates = gate_ref[0]
    merged = (jax.nn.sigmoid(gates[:, 0:D_MODEL]) * branch_a
              + jax.nn.sigmoid(gates[:, D_MODEL:2 * D_MODEL]) * branch_p)
    o_ref[0] = x_ref[0] + jnp.dot(merged.astype(BF16), wout_ref[...], preferred_element_type=F32)


def _merge_call(x, attn, pool, gates, wba, mix, pscale, wbp, wout):
    b, s, d = x.shape
    tm = min(256, s)
    grid = (b, s // tm)
    tile = lambda w: pl.BlockSpec((1, tm, w), lambda i, j: (i, j, 0))
    full2 = lambda i, j: (0, 0)
    in_specs = [
        tile(d), tile(ATTN_WIDTH), tile(POOL_WIDTH), tile(2 * D_MODEL),
        pl.BlockSpec((ATTN_WIDTH, d), full2),
        pl.BlockSpec((len(POOL_WINDOWS), POOL_GROUP_DIM, POOL_GROUP_DIM), lambda i, j: (0, 0, 0)),
        pl.BlockSpec((1, POOL_WIDTH), full2),
        pl.BlockSpec((POOL_WIDTH, d), full2),
        pl.BlockSpec((d, d), full2),
    ]
    return pl.pallas_call(
        _merge_kernel, grid=grid, in_specs=in_specs, out_specs=tile(d),
        out_shape=jax.ShapeDtypeStruct((b, s, d), F32),
        scratch_shapes=[pltpu.VMEM((POOL_HALO + tm, POOL_WIDTH), F32)],
        compiler_params=_cparams("parallel", "arbitrary"), name="merge",
    )(x, attn, pool, gates, wba, mix, pscale, wbp, wout)


def _router_kernel(x_ref, g2_ref, rw_ref, rb_ref, tri_ref, h_ref, meta_ref, gate_ref, cnt_ref, carry_ref):
    i = pl.program_id(0)
    x = x_ref[...]
    tm = x.shape[0]

    @pl.when(i == 0)
    def _():
        carry_ref[...] = jnp.zeros((1, LANES), F32)

    ms = jnp.mean(x * x, axis=-1, keepdims=True)
    h = x * lax.rsqrt(ms + NORM_EPS) * g2_ref[...]
    h_ref[...] = h
    logits = jnp.dot(h.astype(BF16), rw_ref[...], preferred_element_type=F32) + rb_ref[...]
    lane = lax.broadcasted_iota(I32, (tm, LANES), 1)
    work = jnp.where(lane < N_EXPERTS, logits, -jnp.inf)
    vals, idxs = [], []
    for _ in range(TOP_K):
        m = jnp.max(work, axis=-1, keepdims=True)
        idx = jnp.min(jnp.where(work == m, lane, LANES), axis=-1, keepdims=True)
        vals.append(m)
        idxs.append(idx)
        work = jnp.where(lane == idx, -jnp.inf, work)
    exps = [jnp.exp(v - vals[0]) for v in vals]
    denom = exps[0] + exps[1] + exps[2] + exps[3]
    member = jnp.zeros((tm, LANES), F32)
    for idx in idxs:
        member = member + jnp.where(lane == idx, 1.0, 0.0)
    before = jnp.dot(tri_ref[...], member.astype(BF16), preferred_element_type=F32) + carry_ref[...]
    meta = jnp.zeros((tm, LANES), I32)
    gate = jnp.zeros((tm, LANES), F32)
    for k in range(TOP_K):
        rank = jnp.sum(jnp.where(lane == idxs[k], before, 0.0), axis=-1, keepdims=True).astype(I32)
        meta = jnp.where(lane == k, idxs[k], meta)
        meta = jnp.where(lane == TOP_K + k, rank, meta)
        gate = jnp.where(lane == k, exps[k] / denom, gate)
    meta_ref[...] = meta
    gate_ref[...] = gate
    total = carry_ref[...] + jnp.sum(member, axis=0, keepdims=True)
    carry_ref[...] = total
    cnt_ref[...] = total.astype(I32)


def _router_call(x1, g2, rw, rb, tri):
    t, d = x1.shape
    tm = tri.shape[0]
    full2 = lambda i: (0, 0)
    tile = lambda w: pl.BlockSpec((tm, w), lambda i: (i, 0))
    out_shape = (
        jax.ShapeDtypeStruct((t, d), F32),
        jax.ShapeDtypeStruct((t, LANES), I32),
        jax.ShapeDtypeStruct((t, LANES), F32),
        jax.ShapeDtypeStruct((1, LANES), I32),
    )
    return pl.pallas_call(
        _router_kernel, grid=(t // tm,),
        in_specs=[tile(d), pl.BlockSpec((1, d), full2), pl.BlockSpec((d, LANES), full2),
                  pl.BlockSpec((1, LANES), full2), pl.BlockSpec((tm, tm), full2)],
        out_specs=(tile(d), tile(LANES), tile(LANES), pl.BlockSpec((1, LANES), full2)),
        out_shape=out_shape,
        scratch_shapes=[pltpu.VMEM((1, LANES), F32)],
        compiler_params=_cparams("arbitrary"), name="router",
    )(x1, g2, rw, rb, tri)


def _row_copy(src, src_row, dst, dst_row, sem):
    return pltpu.make_async_copy(src.at[pl.ds(src_row, 1)], dst.at[pl.ds(dst_row, 1)], sem)


def _dispatch_kernel(pstart_ref, pfill_ref, pend_ref, meta_ref, h_ref, xs_ref, zero_ref, sem, zsem, *, tm):
    i = pl.program_id(0)
    base = i * tm

    def issue(j, carry):
        for k in range(TOP_K):
            dest = pstart_ref[meta_ref[j * 2 * TOP_K + k]] + meta_ref[j * 2 * TOP_K + TOP_K + k]
            _row_copy(h_ref, base + j, xs_ref, dest, sem).start()
        return carry

    lax.fori_loop(0, tm, issue, 0)

    @pl.when(i == 0)
    def _():
        zero_ref[...] = jnp.zeros(zero_ref.shape, F32)

        def fill_expert(e, carry):
            def fill_row(rw, c):
                _row_copy(zero_ref, 0, xs_ref, rw, zsem).start()
                return c
            lax.fori_loop(pfill_ref[e], pend_ref[e], fill_row, 0)

            def wait_row(rw, c):
                _row_copy(zero_ref, 0, xs_ref, rw, zsem).wait()
                return c
            lax.fori_loop(pfill_ref[e], pend_ref[e], wait_row, 0)
            return carry

        lax.fori_loop(0, N_EXPERTS, fill_expert, 0)

    def drain(j, carry):
        for k in range(TOP_K):
            _row_copy(h_ref, 0, xs_ref, 0, sem).wait()
        return carry

    lax.fori_loop(0, tm, drain, 0)


def _dispatch_call(pstart, pfill, pend, meta_flat, h2, n_rows):
    t, d = h2.shape
    tm = min(256, t)
    return pl.pallas_call(
        functools.partial(_dispatch_kernel, tm=tm),
        grid_spec=pltpu.PrefetchScalarGridSpec(
            num_scalar_prefetch=3, grid=(t // tm,),
            in_specs=[pl.BlockSpec((tm * 2 * TOP_K,), lambda i, *_: (i,), memory_space=pltpu.SMEM),
                      pl.BlockSpec(memory_space=pl.ANY)],
            out_specs=pl.BlockSpec(memory_space=pl.ANY),
            scratch_shapes=[pltpu.VMEM((8, d), F32), pltpu.SemaphoreType.DMA(()),
                            pltpu.SemaphoreType.DMA(())]),
        out_shape=jax.ShapeDtypeStruct((n_rows, d), F32),
        compiler_params=_cparams("arbitrary"), name="dispatch",
    )(pstart, pfill, pend, meta_flat, h2)


def _ffn_kernel(bexp_ref, nused_ref, xs_ref, wgu_ref, bgu_ref, wd_ref, bd_ref, o_ref, wgu_bf, wd_bf):
    j = pl.program_id(0)
    prev = bexp_ref[jnp.maximum(j - 1, 0)]
    fresh = jnp.logical_or(j == 0, bexp_ref[j] != prev)

    @pl.when(jnp.logical_and(fresh, j < nused_ref[0]))
    def _():
        wgu_bf[...] = wgu_ref[0].astype(BF16)
        wd_bf[...] = wd_ref[0].astype(BF16)

    @pl.when(j < nused_ref[0])
    def _():
        xb = xs_ref[...].astype(BF16)
        gu = jnp.dot(xb, wgu_bf[...], preferred_element_type=F32) + bgu_ref[0]
        gate = jnp.minimum(gu[:, 0:EXPERT_DIM], SWIGLU_LIMIT)
        up = jnp.clip(gu[:, EXPERT_DIM:2 * EXPERT_DIM], -SWIGLU_LIMIT, SWIGLU_LIMIT)
        act = gate * jax.nn.sigmoid(SWIGLU_ALPHA * gate) * (up + 1.0)
        o_ref[...] = jnp.dot(act.astype(BF16), wd_bf[...], preferred_element_type=F32) + bd_ref[0]


def _ffn_call(bexp, nused, xs, wgu, bgu, wd, bd):
    n_rows, d = xs.shape
    nb = n_rows // ROW_BLOCK
    rows = lambda j, be, nu: (jnp.minimum(j, nu[0] - 1), 0)
    exp3 = lambda j, be, nu: (be[j], 0, 0)
    return pl.pallas_call(
        _ffn_kernel,
        grid_spec=pltpu.PrefetchScalarGridSpec(
            num_scalar_prefetch=2, grid=(nb,),
            in_specs=[pl.BlockSpec((ROW_BLOCK, d), rows),
                      pl.BlockSpec((1, d, 2 * EXPERT_DIM), exp3),
                      pl.BlockSpec((1, 1, 2 * EXPERT_DIM), exp3),
                      pl.BlockSpec((1, EXPERT_DIM, d), exp3),
                      pl.BlockSpec((1, 1, d), exp3)],
            out_specs=pl.BlockSpec((ROW_BLOCK, d), rows),
            scratch_shapes=[pltpu.VMEM((d, 2 * EXPERT_DIM), BF16), pltpu.VMEM((EXPERT_DIM, d), BF16)]),
        out_shape=jax.ShapeDtypeStruct((n_rows, d), F32),
        compiler_params=_cparams("arbitrary"), name="ffn",
    )(bexp, nused, xs, wgu, bgu, wd, bd)


def _combine_kernel(pstart_ref, meta_ref, ys_ref, gate_ref, x_ref, o_ref, buf_ref, sem, *, tm):
    def issue(j, carry):
        for k in range(TOP_K):
            src = pstart_ref[meta_ref[j * 2 * TOP_K + k]] + meta_ref[j * 2 * TOP_K + TOP_K + k]
            pltpu.make_async_copy(ys_ref.at[pl.ds(src, 1)], buf_ref.at[k, pl.ds(j, 1)], sem).start()
        return carry

    lax.fori_loop(0, tm, issue, 0)

    def drain(j, carry):
        for k in range(TOP_K):
            pltpu.make_async_copy(ys_ref.at[pl.ds(0, 1)], buf_ref.at[k, pl.ds(0, 1)], sem).wait()
        return carry

    lax.fori_loop(0, tm, drain, 0)
    gate = gate_ref[...]
    y = x_ref[...]
    for k in range(TOP_K):
        y = y + gate[:, k:k + 1] * buf_ref[k]
    o_ref[...] = y


def _combine_call(pstart, meta_flat, ys, gates, x1):
    t, d = x1.shape
    tm = min(256, t)
    tile = lambda w: pl.BlockSpec((tm, w), lambda i, *_: (i, 0))
    return pl.pallas_call(
        functools.partial(_combine_kernel, tm=tm),
        grid_spec=pltpu.PrefetchScalarGridSpec(
            num_scalar_prefetch=1, grid=(t // tm,),
            in_specs=[pl.BlockSpec((tm * 2 * TOP_K,), lambda i, *_: (i,), memory_space=pltpu.SMEM),
                      pl.BlockSpec(memory_space=pl.ANY), tile(LANES), tile(d)],
            out_specs=tile(d),
            scratch_shapes=[pltpu.VMEM((TOP_K, tm, d), F32), pltpu.SemaphoreType.DMA(())]),
        out_shape=jax.ShapeDtypeStruct((t, d), F32),
        compiler_params=_cparams("arbitrary"), name="combine",
    )(pstart, meta_flat, ys, gates, x1)


def _pack_w_in(w):
    d = w.shape[0]
    o = np.cumsum([0, ATTN_WIDTH, KV_WIDTH, KV_WIDTH, IDX_HEADS * IDX_DIM, IDX_DIM, IDX_HEADS,
                   POOL_WIDTH, D_MODEL, D_MODEL])
    pad = jnp.zeros((d, LANES - IDX_DIM - IDX_HEADS), w.dtype)
    parts = [w[:, o[0]:o[4]], w[:, o[4]:o[6]], pad, w[:, o[6]:o[9]]]
    return jnp.concatenate(parts, axis=1).astype(BF16)


def _const_rows(q_g, k_g, i_g, i_b):
    lane = np.arange(LANES)
    inv_m = ROPE_THETA ** (-jnp.arange(0, HEAD_DIM, 2, dtype=F32) / HEAD_DIM)
    inv_i = ROPE_THETA ** (-jnp.arange(0, IDX_ROPE_DIM, 2, dtype=F32) / IDX_ROPE_DIM)
    in_rope = (lane % IDX_DIM) < IDX_ROPE_DIM
    zeros = jnp.zeros((LANES - IDX_DIM,), F32)
    rows = [
        inv_m[lane % (HEAD_DIM // 2)],
        jnp.asarray(np.where(lane % HEAD_DIM < HEAD_DIM // 2, -1.0, 1.0), F32),
        jnp.where(jnp.asarray(in_rope), inv_i[lane % (IDX_ROPE_DIM // 2)], 0.0),
        jnp.asarray(np.where(lane % IDX_ROPE_DIM < IDX_ROPE_DIM // 2, -1.0, 1.0), F32),
        jnp.tile(q_g.astype(F32), LANES // HEAD_DIM),
        jnp.tile(k_g.astype(F32), LANES // HEAD_DIM),
        jnp.concatenate([i_g.astype(F32), zeros]),
        jnp.concatenate([i_b.astype(F32), zeros]),
    ]
    return jnp.stack(rows, axis=0)


def _layer(x, positions, norm1_g, w_in, q_norm_g, k_norm_g, idx_k_norm_g, idx_k_norm_b, w_branch_attn,
           pool_mix_w, pool_scale, w_branch_pool, w_out, norm2_g, router_w, router_b, w_gate_up,
           b_gate_up, w_down, b_down):
    b, s, d = x.shape
    t = b * s
    lane = np.arange(LANES)
    gsum = jnp.asarray(lane[:, None] // HEAD_DIM == lane[None, :] // HEAD_DIM, BF16)

    q, kt, v, qi, kit, wi, pool, gates = _proj_call(
        x, positions.reshape(b, s, 1), norm1_g.reshape(1, d), _pack_w_in(w_in),
        _const_rows(q_norm_g, k_norm_g, idx_k_norm_g, idx_k_norm_b), gsum)
    attn = _attn_call(q, kt, v, qi, kit, wi)
    x1 = _merge_call(x, attn, pool, gates, w_branch_attn.astype(BF16), pool_mix_w.astype(BF16),
                     pool_scale.reshape(1, POOL_WIDTH), w_branch_pool.astype(BF16), w_out.astype(BF16))
    x1 = x1.reshape(t, d)

    tm_r = min(512, t)
    tri = jnp.asarray(np.tril(np.ones((tm_r, tm_r), np.float32), -1), BF16)
    rw = jnp.pad(router_w, ((0, 0), (0, LANES - N_EXPERTS))).astype(BF16)
    rb = jnp.pad(router_b, (0, LANES - N_EXPERTS)).reshape(1, LANES)
    h2, meta, rgate, counts = _router_call(x1, norm2_g.reshape(1, d), rw, rb, tri)

    counts = counts[0, :N_EXPERTS]
    padded = (counts + ROW_BLOCK - 1) // ROW_BLOCK * ROW_BLOCK
    pend = jnp.cumsum(padded).astype(I32)
    pstart = pend - padded
    pfill = pstart + counts
    n_blocks = (t * TOP_K + N_EXPERTS * (ROW_BLOCK - 1)) // ROW_BLOCK + 1
    bexp = jnp.minimum(
        jnp.searchsorted(pend, jnp.arange(n_blocks, dtype=I32) * ROW_BLOCK, side="right"),
        N_EXPERTS - 1).astype(I32)
    nused = (pend[-1:] // ROW_BLOCK).astype(I32)
    meta_flat = meta[:, :2 * TOP_K].reshape(t * 2 * TOP_K)

    xs = _dispatch_call(pstart, pfill, pend, meta_flat, h2, n_blocks * ROW_BLOCK)
    ys = _ffn_call(bexp, nused, xs, w_gate_up, b_gate_up.reshape(N_EXPERTS, 1, 2 * EXPERT_DIM),
                   w_down, b_down.reshape(N_EXPERTS, 1, d))
    out = _combine_call(pstart, meta_flat, ys, rgate, x1)
    return out.reshape(b, s, d)


def kernel(x, positions, norm1_g, w_in, q_norm_g, k_norm_g, idx_k_norm_g, idx_k_norm_b, w_branch_attn,
           pool_mix_w, pool_scale, w_branch_pool, w_out, norm2_g, router_w, router_b, w_gate_up,
           b_gate_up, w_down, b_down):
    for l in range(norm1_g.shape[0]):
        x = _layer(x, positions, norm1_g[l], w_in[l], q_norm_g[l], k_norm_g[l], idx_k_norm_g[l],
                   idx_k_norm_b[l], w_branch_attn[l], pool_mix_w[l], pool_scale[l], w_branch_pool[l],
                   w_out[l], norm2_g[l], router_w[l], router_b[l], w_gate_up[l], b_gate_up[l],
                   w_down[l], b_down[l])
    return x
```

```python
import functools

import numpy as np
import jax
import jax.numpy as jnp
from jax import lax
from jax.experimental import pallas as pl
from jax.experimental.pallas import tpu as pltpu

F32 = jnp.float32
BF16 = jnp.bfloat16
I32 = jnp.int32

D_MODEL = 1024
CHUNK = 64
CHUNK_SHIFT = CHUNK.bit_length() - 1
ATTN_HEADS = 8
ATTN_KV_HEADS = 2
HEAD_DIM = 64
GROUPS = ATTN_HEADS // ATTN_KV_HEADS
ATTN_WIDTH = ATTN_HEADS * HEAD_DIM
KV_WIDTH = ATTN_KV_HEADS * HEAD_DIM
ATTN_SCALE = HEAD_DIM ** -0.5
IDX_HEADS = 4
IDX_DIM = 64
IDX_ROPE_DIM = 32
IDX_SCALE = (IDX_HEADS ** -0.5) * (IDX_DIM ** -0.5)
IDX_TOPK_MAX = 256
QUERY_BLOCK = 128
POOL_WINDOWS = (2, 4, 8, 16)
POOL_WIDTH = 512
POOL_GROUP_DIM = 128
POOL_HALO = 16
N_EXPERTS = 32
TOP_K = 4
EXPERT_DIM = 1024
SWIGLU_ALPHA = 1.702
SWIGLU_LIMIT = 7.0
ROPE_THETA = 10000.0
NORM_EPS = 1e-6

LANES = 128
SUBLANES = 8
VMEM_LIMIT = 56 * 1024 * 1024
FLT_MAX = float(np.finfo(np.float32).max)
MASKED = -1e30

C_Q = 0
C_K = 512
C_V = 640
C_QI = 768
C_KIW = 1024
C_POOL = 1152
C_GATE = 1664
W_PACKED = 3712
C_SMALL_END = C_POOL

ROW_BLOCK = 512
FOLD_ROWS = 8 * SUBLANES
VALUE_BISECT_ITERS = 16
BISECT_CAP = 64
BISECT_UNROLL = 3
ATT_CK = 256
VT_ROWS = HEAD_DIM + 2 * SUBLANES
LOG2E = float(np.log2(np.e))


def _cparams(*sem):
    return pltpu.CompilerParams(dimension_semantics=sem, vmem_limit_bytes=VMEM_LIMIT)


def _swap_halves(xc, first, half):
    return jnp.where(first, pltpu.roll(xc, LANES - half, 1), pltpu.roll(xc, half, 1))


def _proj_kernel(x_ref, pos_ref, g1_ref, w_ref, rows_ref, gsum_ref,
                 qt_ref, k_ref, vt_ref, qit_ref, ki_ref, wit_ref, pool_ref, gate_ref):
    x = x_ref[0]
    tm = x.shape[0]
    ms = jnp.mean(x * x, axis=-1, keepdims=True)
    h = (x * lax.rsqrt(ms + NORM_EPS) * g1_ref[...]).astype(BF16)
    d1 = jnp.dot(h, w_ref[:, 0:C_SMALL_END], preferred_element_type=F32)
    pool_ref[0] = jnp.dot(h, w_ref[:, C_POOL:C_GATE], preferred_element_type=F32)
    gate_ref[0] = jnp.dot(h, w_ref[:, C_GATE:W_PACKED], preferred_element_type=F32)

    pos = pos_ref[0].astype(F32)
    rows = rows_ref[...]
    lane = lax.broadcasted_iota(I32, (tm, LANES), 1)
    first_m = (lane & (HEAD_DIM - 1)) < HEAD_DIM // 2
    first_i = (lane & (IDX_ROPE_DIM - 1)) < IDX_ROPE_DIM // 2
    ang_m = pos * rows[0:1]
    cos_m = jnp.cos(ang_m)
    sin_m = jnp.sin(ang_m) * rows[1:2]
    ang_i = pos * rows[2:3]
    cos_i = jnp.cos(ang_i)
    sin_i = jnp.sin(ang_i) * rows[3:4]
    gsum = gsum_ref[...]

    def head_rms(xc, grow):
        sq = xc * xc
        hi = sq.astype(BF16)
        lo = (sq - hi.astype(F32)).astype(BF16)
        ssum = (jnp.dot(hi, gsum, preferred_element_type=F32)
                + jnp.dot(lo, gsum, preferred_element_type=F32))
        return xc * lax.rsqrt(ssum * (1.0 / HEAD_DIM) + NORM_EPS) * grow

    def rope_m(xc):
        return xc * cos_m + _swap_halves(xc, first_m, HEAD_DIM // 2) * sin_m

    def rope_i(xc):
        return xc * cos_i + _swap_halves(xc, first_i, IDX_ROPE_DIM // 2) * sin_i

    for c in range(ATTN_WIDTH // LANES):
        qc = d1[:, C_Q + c * LANES:C_Q + (c + 1) * LANES]
        qr = rope_m(head_rms(qc, rows[4:5])) * (ATTN_SCALE * LOG2E)
        qt_ref[0, c * LANES:(c + 1) * LANES, :] = qr.T.astype(BF16)

    kr = rope_m(head_rms(d1[:, C_K:C_K + KV_WIDTH], rows[5:6]))
    vt = d1[:, C_V:C_V + KV_WIDTH].T
    for r in range(ATTN_KV_HEADS):
        k_ref[0, r] = kr[:, r * HEAD_DIM:(r + 1) * HEAD_DIM].astype(BF16)
        vt_ref[0, r, 0:HEAD_DIM, :] = vt[r * HEAD_DIM:(r + 1) * HEAD_DIM].astype(BF16)
        vt_ref[0, r, HEAD_DIM:VT_ROWS, :] = jnp.ones((VT_ROWS - HEAD_DIM, tm), BF16)

    for c in range(IDX_HEADS * IDX_DIM // LANES):
        qc = d1[:, C_QI + c * LANES:C_QI + (c + 1) * LANES]
        qit_ref[0, c * LANES:(c + 1) * LANES, :] = rope_i(qc).T.astype(BF16)

    kiw = d1[:, C_KIW:C_KIW + LANES]
    in_ki = lane < IDX_DIM
    mu = jnp.sum(jnp.where(in_ki, kiw, 0.0), axis=-1, keepdims=True) * (1.0 / IDX_DIM)
    dv = jnp.where(in_ki, kiw - mu, 0.0)
    var = jnp.sum(dv * dv, axis=-1, keepdims=True) * (1.0 / IDX_DIM)
    kin = dv * lax.rsqrt(var + NORM_EPS) * rows[6:7] + rows[7:8]
    ki_ref[0] = rope_i(kin)[:, 0:IDX_DIM].astype(BF16)
    wit_ref[0] = (kiw * IDX_SCALE).T[IDX_DIM:IDX_DIM + SUBLANES]


def _proj_call(x, pos3, g1, w_packed, rows, gsum):
    b, s, d = x.shape
    tm = min(512, s)
    grid = (b, s // tm)
    full2 = lambda i, j: (0, 0)
    out_shape = (
        jax.ShapeDtypeStruct((b, ATTN_WIDTH, s), BF16),
        jax.ShapeDtypeStruct((b, ATTN_KV_HEADS, s, HEAD_DIM), BF16),
        jax.ShapeDtypeStruct((b, ATTN_KV_HEADS, VT_ROWS, s), BF16),
        jax.ShapeDtypeStruct((b, IDX_HEADS * IDX_DIM, s), BF16),
        jax.ShapeDtypeStruct((b, s, IDX_DIM), BF16),
        jax.ShapeDtypeStruct((b, SUBLANES, s), F32),
        jax.ShapeDtypeStruct((b, s, POOL_WIDTH), F32),
        jax.ShapeDtypeStruct((b, s, 2 * D_MODEL), F32),
    )
    in_specs = [
        pl.BlockSpec((1, tm, d), lambda i, j: (i, j, 0)),
        pl.BlockSpec((1, tm, 1), lambda i, j: (i, j, 0)),
        pl.BlockSpec((1, d), full2),
        pl.BlockSpec((d, W_PACKED), full2),
        pl.BlockSpec((8, LANES), full2),
        pl.BlockSpec((LANES, LANES), full2),
    ]
    out_specs = (
        pl.BlockSpec((1, ATTN_WIDTH, tm), lambda i, j: (i, 0, j)),
        pl.BlockSpec((1, ATTN_KV_HEADS, tm, HEAD_DIM), lambda i, j: (i, 0, j, 0)),
        pl.BlockSpec((1, ATTN_KV_HEADS, VT_ROWS, tm), lambda i, j: (i, 0, 0, j)),
        pl.BlockSpec((1, IDX_HEADS * IDX_DIM, tm), lambda i, j: (i, 0, j)),
        pl.BlockSpec((1, tm, IDX_DIM), lambda i, j: (i, j, 0)),
        pl.BlockSpec((1, SUBLANES, tm), lambda i, j: (i, 0, j)),
        pl.BlockSpec((1, tm, POOL_WIDTH), lambda i, j: (i, j, 0)),
        pl.BlockSpec((1, tm, 2 * D_MODEL), lambda i, j: (i, j, 0)),
    )
    return pl.pallas_call(
        _proj_kernel, grid=grid, in_specs=in_specs, out_specs=out_specs, out_shape=out_shape,
        compiler_params=_cparams("parallel", "parallel"), name="proj",
    )(x, pos3, g1, w_packed, rows, gsum)


def _sort_key(x):
    b = lax.bitcast_convert_type(x, I32)
    return b ^ ((b >> 31) & 0x7FFFFFFF)


def _unsort_key(k):
    return lax.bitcast_convert_type(k ^ ((k >> 31) & 0x7FFFFFFF), F32)


def _attn_kernel(qt_ref, qit_ref, wit_ref, ki_ref, k_ref, vt_ref, o_ref, sc_ref, sa_ref, sb_ref,
                 *, n_sel, ck):
    qb = QUERY_BLOCK
    t0 = pl.program_id(1) * qb
    nck = (t0 + qb + ck - 1) // ck
    qpos = t0 + lax.broadcasted_iota(I32, (1, qb), 1)
    cend = ((qpos >> CHUNK_SHIFT) + 1) * CHUNK
    key_ck = lax.broadcasted_iota(I32, (ck, qb), 0)
    k_sel = float(n_sel)

    def fold(x, op):
        return op(x.reshape(ck // FOLD_ROWS, FOLD_ROWS, qb), axis=0)

    qit = qit_ref[0]
    rhs_i = jnp.concatenate([qit[h * IDX_DIM:(h + 1) * IDX_DIM] for h in range(IDX_HEADS)], axis=1)
    wit = wit_ref[0]

    def score_body(c, carry):
        mxp, mnp, c0p, c1p = carry
        off = pl.multiple_of(c * ck, ck)
        lg = jnp.dot(ki_ref[0, pl.ds(off, ck), :], rhs_i, preferred_element_type=F32)
        sc = jnp.maximum(lg[:, 0:qb], 0.0) * wit[0:1]
        for h in range(1, IDX_HEADS):
            sc = sc + jnp.maximum(lg[:, h * qb:(h + 1) * qb], 0.0) * wit[h:h + 1]
        adm = (off + key_ck) < cend
        sc = jnp.where(adm, sc, -jnp.inf)
        sc_ref[pl.ds(off, ck), :] = sc
        mxp = jnp.maximum(mxp, fold(sc, jnp.max))
        mnp = jnp.minimum(mnp, fold(jnp.where(adm, sc, jnp.inf), jnp.min))
        c0p = c0p + fold(jnp.where(sc >= 0.0, 1.0, 0.0), jnp.sum)
        c1p = c1p + fold(jnp.where(sc > 0.0, 1.0, 0.0), jnp.sum)
        return mxp, mnp, c0p, c1p

    init = (jnp.full((FOLD_ROWS, qb), -jnp.inf, F32), jnp.full((FOLD_ROWS, qb), jnp.inf, F32),
            jnp.zeros((FOLD_ROWS, qb), F32), jnp.zeros((FOLD_ROWS, qb), F32))
    mxp, mnp, c0p, c1p = lax.fori_loop(0, nck, score_body, init)
    mx = jnp.max(mxp, axis=0, keepdims=True)
    mn = jnp.min(mnp, axis=0, keepdims=True)
    c0 = jnp.sum(c0p, axis=0, keepdims=True)
    c1 = jnp.sum(c1p, axis=0, keepdims=True)

    def count(pred):
        def body(c, acc):
            off = pl.multiple_of(c * ck, ck)
            hit = pred(sc_ref[pl.ds(off, ck), :], off + key_ck)
            return acc + fold(jnp.where(hit, 1.0, 0.0), jnp.sum)
        acc = lax.fori_loop(0, nck, body, jnp.zeros((FOLD_ROWS, qb), F32))
        return jnp.sum(acc, axis=0, keepdims=True)

    small = cend.astype(F32) <= k_sel
    at_zero = jnp.logical_and(jnp.logical_not(small), jnp.logical_and(c1 < k_sel, c0 >= k_sel))
    positive = jnp.logical_and(jnp.logical_not(small), c1 >= k_sel)
    lo0 = jnp.where(positive, 0.0, mn)
    hi0 = jnp.where(positive, jnp.minimum(2.0 * mx, FLT_MAX), 0.0)
    thr0 = jnp.where(small, -FLT_MAX, 0.0)
    done0 = jnp.where(jnp.logical_or(small, at_zero), 1.0, 0.0)
    tie0 = jnp.where(jnp.logical_and(at_zero, c0 > k_sel), 1.0, 0.0)
    left0 = jnp.sum(1.0 - done0)

    def bisect_cond(st):
        it, left = st[0], st[1]
        return jnp.logical_and(it < BISECT_CAP, left > 0.0)

    def bisect_step(it, lo, hi, thr, done, tie):
        mid_v = 0.5 * lo + 0.5 * hi
        klo = _sort_key(lo)
        khi = _sort_key(hi)
        mid_k = _unsort_key((klo & khi) + ((klo ^ khi) >> 1))
        mid = jnp.where(it < VALUE_BISECT_ITERS, mid_v, mid_k)
        stuck = jnp.logical_or(mid <= lo, mid >= hi)
        cnt = count(lambda s, _: s >= mid)
        active = done == 0.0
        moving = jnp.logical_and(active, jnp.logical_not(stuck))
        hit = jnp.logical_and(moving, cnt == k_sel)
        new_tie = jnp.logical_and(active, stuck)
        thr = jnp.where(hit, mid, jnp.where(new_tie, lo, thr))
        tie = jnp.where(new_tie, 1.0, tie)
        done = jnp.where(jnp.logical_or(hit, new_tie), 1.0, done)
        upd = jnp.logical_and(moving, jnp.logical_not(hit))
        lo = jnp.where(jnp.logical_and(upd, cnt >= k_sel), mid, lo)
        hi = jnp.where(jnp.logical_and(upd, cnt < k_sel), mid, hi)
        return lo, hi, thr, done, tie

    def bisect_body(st):
        it, _, lo, hi, thr, done, tie = st
        for u in range(BISECT_UNROLL):
            lo, hi, thr, done, tie = bisect_step(it + u, lo, hi, thr, done, tie)
        return it + BISECT_UNROLL, jnp.sum(1.0 - done), lo, hi, thr, done, tie

    st = lax.while_loop(bisect_cond, bisect_body, (jnp.int32(0), left0, lo0, hi0, thr0, done0, tie0))
    thr, tie = st[4], st[6]

    @pl.when(jnp.sum(tie) > 0.0)
    def _():
        need = k_sel - count(lambda s, _: s > thr)
        nbits = sc_ref.shape[0].bit_length()

        def bit_body(i, jsel):
            cand = jsel + lax.shift_left(jnp.int32(1), nbits - 1 - i)
            f = count(lambda s, kp: jnp.logical_and(s == thr, kp < cand))
            return jnp.where(f <= need, cand, jsel)

        jsel = lax.fori_loop(0, nbits, bit_body, jnp.zeros((1, qb), I32))
        jsel = jnp.where(tie > 0.0, jsel, jnp.int32(2 ** 30))

        def drop_body(c, carry):
            off = pl.multiple_of(c * ck, ck)
            blk = sc_ref[pl.ds(off, ck), :]
            drop = jnp.logical_and(blk == thr, (off + key_ck) >= jsel)
            sc_ref[pl.ds(off, ck), :] = jnp.where(drop, -jnp.inf, blk)
            return carry

        lax.fori_loop(0, nck, drop_body, 0)

    qt = qt_ref[0]
    rhs = [jnp.concatenate(
        [qt[(r * GROUPS + g) * HEAD_DIM:(r * GROUPS + g + 1) * HEAD_DIM] for g in range(GROUPS)],
        axis=1) for r in range(ATTN_KV_HEADS)]

    n_att = (t0 + qb + ATT_CK - 1) // ATT_CK

    def step_offset(step):
        return pl.multiple_of(jnp.minimum(step, n_att - 1) * ATT_CK, ATT_CK)

    def qk_logits(step, buf):
        off = step_offset(step)
        for r in range(ATTN_KV_HEADS):
            buf[r] = jnp.dot(k_ref[0, r, pl.ds(off, ATT_CK), :], rhs[r], preferred_element_type=F32)

    def softmax_pv(step, buf, carry):
        off = step_offset(step)
        thr_step = jnp.where(step < n_att, thr, jnp.inf)
        bias = jnp.where(sc_ref[pl.ds(off, ATT_CK), :] >= thr_step, 0.0, MASKED)
        bias = jnp.concatenate([bias] * GROUPS, axis=1)
        out = []
        for r in range(ATTN_KV_HEADS):
            m, acc = carry[r]
            s = buf[r] + bias
            m_new = jnp.maximum(m, jnp.max(s, axis=0, keepdims=True))
            p = jnp.exp2(s - m_new).astype(BF16)
            pv = jnp.dot(vt_ref[0, r, :, pl.ds(off, ATT_CK)], p, preferred_element_type=F32)
            out.append((m_new, jnp.exp2(m - m_new) * acc + pv))
        return tuple(out)

    def att_body(i, carry):
        qk_logits(2 * i + 1, sb_ref)
        carry = softmax_pv(2 * i, sa_ref, carry)
        qk_logits(2 * i + 2, sa_ref)
        return softmax_pv(2 * i + 1, sb_ref, carry)

    init_a = tuple((jnp.full((1, GROUPS * qb), MASKED, F32), jnp.zeros((VT_ROWS, GROUPS * qb), F32))
                   for _ in range(ATTN_KV_HEADS))
    qk_logits(0, sa_ref)
    fin = lax.fori_loop(0, (n_att + 1) // 2, att_body, init_a)
    outs = []
    for r in range(ATTN_KV_HEADS):
        acc = fin[r][1]
        o = acc[0:HEAD_DIM] / acc[HEAD_DIM:HEAD_DIM + 1]
        outs.extend(o[:, g * qb:(g + 1) * qb] for g in range(GROUPS))
    o_ref[0] = jnp.concatenate(outs, axis=0).T.astype(BF16)


def _attn_call(qt, k, vt, qit, ki, wit):
    b, _, s = qt.shape
    n_sel = min(IDX_TOPK_MAX, s // 4)
    ck = min(512, s)
    grid = (b, s // QUERY_BLOCK)
    in_specs = [
        pl.BlockSpec((1, ATTN_WIDTH, QUERY_BLOCK), lambda i, j: (i, 0, j)),
        pl.BlockSpec((1, IDX_HEADS * IDX_DIM, QUERY_BLOCK), lambda i, j: (i, 0, j)),
        pl.BlockSpec((1, SUBLANES, QUERY_BLOCK), lambda i, j: (i, 0, j)),
        pl.BlockSpec((1, s, IDX_DIM), lambda i, j: (i, 0, 0)),
        pl.BlockSpec((1, ATTN_KV_HEADS, s, HEAD_DIM), lambda i, j: (i, 0, 0, 0)),
        pl.BlockSpec((1, ATTN_KV_HEADS, VT_ROWS, s), lambda i, j: (i, 0, 0, 0)),
    ]
    return pl.pallas_call(
        functools.partial(_attn_kernel, n_sel=n_sel, ck=ck),
        grid=grid, in_specs=in_specs,
        out_specs=pl.BlockSpec((1, QUERY_BLOCK, ATTN_WIDTH), lambda i, j: (i, j, 0)),
        out_shape=jax.ShapeDtypeStruct((b, s, ATTN_WIDTH), BF16),
        scratch_shapes=[pltpu.VMEM((s, QUERY_BLOCK), F32),
                        pltpu.VMEM((ATTN_KV_HEADS, ATT_CK, GROUPS * QUERY_BLOCK), F32),
                        pltpu.VMEM((ATTN_KV_HEADS, ATT_CK, GROUPS * QUERY_BLOCK), F32)],
        compiler_params=_cparams("parallel", "parallel"), name="attn",
    )(qt, qit, wit, ki, k, vt)


def _merge_kernel(x_ref, attn_ref, pool_ref, gate_ref, wba_ref, mix_ref, pscale_ref, wbp_ref, wout_ref,
                  o_ref, ext_ref):
    tm = x_ref.shape[1]
    j = pl.program_id(1)

    @pl.when(j == 0)
    def _():
        ext_ref[0:POOL_HALO] = jnp.zeros((POOL_HALO, POOL_WIDTH), F32)

    @pl.when(j > 0)
    def _():
        ext_ref[0:POOL_HALO] = ext_ref[tm:tm + POOL_HALO]

    ext_ref[POOL_HALO:POOL_HALO + tm] = pool_ref[0]
    t = j * tm + lax.broadcasted_iota(I32, (tm, 1), 0)
    mixed = []
    for g, w in enumerate(POOL_WINDOWS):
        cols = slice(g * POOL_GROUP_DIM, (g + 1) * POOL_GROUP_DIM)
        cur = ext_ref[POOL_HALO:POOL_HALO + tm, cols]
        wsum = cur
        for i in range(1, w):
            wsum = wsum + ext_ref[POOL_HALO - i:POOL_HALO - i + tm, cols]
        cnt = jnp.minimum(t + 1, w).astype(F32)
        dev = (wsum / cnt - cur).astype(BF16)
        mixed.append(jnp.dot(dev, mix_ref[g], preferred_element_type=F32))
    pooled = (jnp.concatenate(mixed, axis=1) * pscale_ref[...]).astype(BF16)
    branch_a = jnp.dot(attn_ref[0], wba_ref[...], preferred_element_type=F32)
    branch_p = jnp.dot(pooled, wbp_ref[...], preferred_element_type=F32)
    gates = gate_ref[0]
    merged = (jax.nn.sigmoid(gates[:, 0:D_MODEL]) * branch_a
              + jax.nn.sigmoid(gates[:, D_MODEL:2 * D_MODEL]) * branch_p)
    o_ref[0] = x_ref[0] + jnp.dot(merged.astype(BF16), wout_ref[...], preferred_element_type=F32)


def _merge_call(x, attn, pool, gates, wba, mix, pscale, wbp, wout):
    b, s, d = x.shape
    tm = min(256, s)
    grid = (b, s // tm)
    tile = lambda w: pl.BlockSpec((1, tm, w), lambda i, j: (i, j, 0))
    full2 = lambda i, j: (0, 0)
    in_specs = [
        tile(d), tile(ATTN_WIDTH), tile(POOL_WIDTH), tile(2 * D_MODEL),
        pl.BlockSpec((ATTN_WIDTH, d), full2),
        pl.BlockSpec((len(POOL_WINDOWS), POOL_GROUP_DIM, POOL_GROUP_DIM), lambda i, j: (0, 0, 0)),
        pl.BlockSpec((1, POOL_WIDTH), full2),
        pl.BlockSpec((POOL_WIDTH, d), full2),
        pl.BlockSpec((d, d), full2),
    ]
    return pl.pallas_call(
        _merge_kernel, grid=grid, in_specs=in_specs, out_specs=tile(d),
        out_shape=jax.ShapeDtypeStruct((b, s, d), F32),
        scratch_shapes=[pltpu.VMEM((POOL_HALO + tm, POOL_WIDTH), F32)],
        compiler_params=_cparams("parallel", "arbitrary"), name="merge",
    )(x, attn, pool, gates, wba, mix, pscale, wbp, wout)


def _router_kernel(x_ref, g2_ref, rw_ref, rb_ref, tri_ref, h_ref, meta_ref, gate_ref, cnt_ref, carry_ref):
    i = pl.program_id(0)
    x = x_ref[...]
    tm = x.shape[0]

    @pl.when(i == 0)
    def _():
        carry_ref[...] = jnp.zeros((1, LANES), F32)

    ms = jnp.mean(x * x, axis=-1, keepdims=True)
    h = x * lax.rsqrt(ms + NORM_EPS) * g2_ref[...]
    h_ref[...] = h
    logits = jnp.dot(h.astype(BF16), rw_ref[...], preferred_element_type=F32) + rb_ref[...]
    lane = lax.broadcasted_iota(I32, (tm, LANES), 1).astype(F32)
    work = jnp.where(lane < N_EXPERTS, logits, -jnp.inf)
    vals, idxs = [], []
    for _ in range(TOP_K):
        m = jnp.max(work, axis=-1, keepdims=True)
        idx = jnp.min(jnp.where(work == m, lane, float(LANES)), axis=-1, keepdims=True)
        vals.append(m)
        idxs.append(idx)
        work = jnp.where(lane == idx, -jnp.inf, work)
    exps = [jnp.exp(v - vals[0]) for v in vals]
    denom = exps[0] + exps[1] + exps[2] + exps[3]
    member = jnp.zeros((tm, LANES), F32)
    for idx in idxs:
        member = member + jnp.where(lane == idx, 1.0, 0.0)
    before = jnp.dot(tri_ref[...], member.astype(BF16), preferred_element_type=F32) + carry_ref[...]
    meta = jnp.zeros((tm, LANES), F32)
    gate = jnp.zeros((tm, LANES), F32)
    for k in range(TOP_K):
        rank = jnp.sum(jnp.where(lane == idxs[k], before, 0.0), axis=-1, keepdims=True)
        meta = jnp.where(lane == float(k), idxs[k], meta)
        meta = jnp.where(lane == float(TOP_K + k), rank, meta)
        gate = jnp.where(lane == float(k), exps[k] / denom, gate)
    meta_ref[...] = meta.astype(I32)
    gate_ref[...] = gate
    total = carry_ref[...] + jnp.sum(member, axis=0, keepdims=True)
    carry_ref[...] = total
    cnt_ref[...] = total.astype(I32)


def _router_call(x1, g2, rw, rb, tri):
    t, d = x1.shape
    tm = tri.shape[0]
    full2 = lambda i: (0, 0)
    tile = lambda w: pl.BlockSpec((tm, w), lambda i: (i, 0))
    out_shape = (
        jax.ShapeDtypeStruct((t, d), F32),
        jax.ShapeDtypeStruct((t, LANES), I32),
        jax.ShapeDtypeStruct((t, LANES), F32),
        jax.ShapeDtypeStruct((1, LANES), I32),
    )
    return pl.pallas_call(
        _router_kernel, grid=(t // tm,),
        in_specs=[tile(d), pl.BlockSpec((1, d), full2), pl.BlockSpec((d, LANES), full2),
                  pl.BlockSpec((1, LANES), full2), pl.BlockSpec((tm, tm), full2)],
        out_specs=(tile(d), tile(LANES), tile(LANES), pl.BlockSpec((1, LANES), full2)),
        out_shape=out_shape,
        scratch_shapes=[pltpu.VMEM((1, LANES), F32)],
        compiler_params=_cparams("arbitrary"), name="router",
    )(x1, g2, rw, rb, tri)


def _slot(gstart_ref, meta_ref, j, k):
    return gstart_ref[meta_ref[j * 2 * TOP_K + k]] + meta_ref[j * 2 * TOP_K + TOP_K + k]


def _dispatch_kernel(gstart_ref, meta_ref, h_ref, xs_ref, sem, *, tm):
    def issue(j, carry):
        for k in range(TOP_K):
            pltpu.make_async_copy(h_ref.at[pl.ds(j, 1)],
                                  xs_ref.at[pl.ds(_slot(gstart_ref, meta_ref, j, k), 1)], sem).start()
        return carry

    lax.fori_loop(0, tm, issue, 0)

    def drain(j, carry):
        for k in range(TOP_K):
            pltpu.make_async_copy(h_ref.at[pl.ds(0, 1)], xs_ref.at[pl.ds(0, 1)], sem).wait()
        return carry

    lax.fori_loop(0, tm, drain, 0)


def _dispatch_call(gstart, meta_flat, h2):
    t, d = h2.shape
    tm = min(256, t)
    return pl.pallas_call(
        functools.partial(_dispatch_kernel, tm=tm),
        grid_spec=pltpu.PrefetchScalarGridSpec(
            num_scalar_prefetch=1, grid=(t // tm,),
            in_specs=[pl.BlockSpec((tm * 2 * TOP_K,), lambda i, *_: (i,), memory_space=pltpu.SMEM),
                      pl.BlockSpec((tm, d), lambda i, *_: (i, 0))],
            out_specs=pl.BlockSpec(memory_space=pl.ANY),
            scratch_shapes=[pltpu.SemaphoreType.DMA(())]),
        out_shape=jax.ShapeDtypeStruct((t * TOP_K, d), F32),
        compiler_params=_cparams("arbitrary"), name="dispatch",
    )(gstart, meta_flat, h2)


def _ffn_kernel(item_e_ref, item_blk_ref, nact_ref, gstart_ref, gend_ref,
                xs_ref, wgu_ref, bgu_ref, wd_ref, bd_ref, o_ref, wgu_bf, wd_bf):
    w = pl.program_id(0)
    e = item_e_ref[w]
    blk = item_blk_ref[w]
    prev = jnp.maximum(w - 1, 0)
    active = w < nact_ref[0]
    new_expert = jnp.logical_or(w == 0, e != item_e_ref[prev])
    first_visit = jnp.logical_or(w == 0, blk != item_blk_ref[prev])

    @pl.when(jnp.logical_and(active, new_expert))
    def _():
        wgu_bf[...] = wgu_ref[0].astype(BF16)
        wd_bf[...] = wd_ref[0].astype(BF16)

    @pl.when(active)
    def _():
        xb = xs_ref[...].astype(BF16)
        gu = jnp.dot(xb, wgu_bf[...], preferred_element_type=F32) + bgu_ref[0]
        gate = jnp.minimum(gu[:, 0:EXPERT_DIM], SWIGLU_LIMIT)
        up = jnp.clip(gu[:, EXPERT_DIM:2 * EXPERT_DIM], -SWIGLU_LIMIT, SWIGLU_LIMIT)
        act = gate * jax.nn.sigmoid(SWIGLU_ALPHA * gate) * (up + 1.0)
        res = jnp.dot(act.astype(BF16), wd_bf[...], preferred_element_type=F32) + bd_ref[0]
        row = blk * ROW_BLOCK + lax.broadcasted_iota(I32, (ROW_BLOCK, 1), 0)
        mine = jnp.logical_and(row >= gstart_ref[e], row < gend_ref[e])

        @pl.when(first_visit)
        def _():
            o_ref[...] = jnp.where(mine, res, 0.0)

        @pl.when(jnp.logical_not(first_visit))
        def _():
            o_ref[...] = jnp.where(mine, res, o_ref[...])


def _ffn_call(item_e, item_blk, nact, gstart, gend, xs, wgu, bgu, wd, bd):
    n_rows, d = xs.shape
    rows = lambda w, ie, ib, *_: (ib[w], 0)
    exp3 = lambda w, ie, *_: (ie[w], 0, 0)
    return pl.pallas_call(
        _ffn_kernel,
        grid_spec=pltpu.PrefetchScalarGridSpec(
            num_scalar_prefetch=5, grid=(item_e.shape[0],),
            in_specs=[pl.BlockSpec((ROW_BLOCK, d), rows),
                      pl.BlockSpec((1, d, 2 * EXPERT_DIM), exp3),
                      pl.BlockSpec((1, 1, 2 * EXPERT_DIM), exp3),
                      pl.BlockSpec((1, EXPERT_DIM, d), exp3),
                      pl.BlockSpec((1, 1, d), exp3)],
            out_specs=pl.BlockSpec((ROW_BLOCK, d), rows),
            scratch_shapes=[pltpu.VMEM((d, 2 * EXPERT_DIM), BF16), pltpu.VMEM((EXPERT_DIM, d), BF16)]),
        out_shape=jax.ShapeDtypeStruct((n_rows, d), F32),
        compiler_params=_cparams("arbitrary"), name="ffn",
    )(item_e, item_blk, nact, gstart, gend, xs, wgu, bgu, wd, bd)


def _combine_kernel(gstart_ref, meta_ref, ys_ref, gate_ref, x_ref, o_ref, buf_ref, sem, *, tm):
    def issue(j, carry):
        for k in range(TOP_K):
            pltpu.make_async_copy(ys_ref.at[pl.ds(_slot(gstart_ref, meta_ref, j, k), 1)],
                                  buf_ref.at[k, pl.ds(j, 1)], sem).start()
        return carry

    lax.fori_loop(0, tm, issue, 0)

    def drain(j, carry):
        for k in range(TOP_K):
            pltpu.make_async_copy(ys_ref.at[pl.ds(0, 1)], buf_ref.at[k, pl.ds(0, 1)], sem).wait()
        return carry

    lax.fori_loop(0, tm, drain, 0)
    gate = gate_ref[...]
    y = x_ref[...]
    for k in range(TOP_K):
        y = y + gate[:, k:k + 1] * buf_ref[k]
    o_ref[...] = y


def _combine_call(gstart, meta_flat, ys, gates, x1):
    t, d = x1.shape
    tm = min(256, t)
    tile = lambda w: pl.BlockSpec((tm, w), lambda i, *_: (i, 0))
    return pl.pallas_call(
        functools.partial(_combine_kernel, tm=tm),
        grid_spec=pltpu.PrefetchScalarGridSpec(
            num_scalar_prefetch=1, grid=(t // tm,),
            in_specs=[pl.BlockSpec((tm * 2 * TOP_K,), lambda i, *_: (i,), memory_space=pltpu.SMEM),
                      pl.BlockSpec(memory_space=pl.ANY), tile(LANES), tile(d)],
            out_specs=tile(d),
            scratch_shapes=[pltpu.VMEM((TOP_K, tm, d), F32), pltpu.SemaphoreType.DMA(())]),
        out_shape=jax.ShapeDtypeStruct((t, d), F32),
        compiler_params=_cparams("arbitrary"), name="combine",
    )(gstart, meta_flat, ys, gates, x1)


def _pack_w_in(w):
    d = w.shape[0]
    o = np.cumsum([0, ATTN_WIDTH, KV_WIDTH, KV_WIDTH, IDX_HEADS * IDX_DIM, IDX_DIM, IDX_HEADS,
                   POOL_WIDTH, D_MODEL, D_MODEL])
    pad = jnp.zeros((d, LANES - IDX_DIM - IDX_HEADS), w.dtype)
    parts = [w[:, o[0]:o[4]], w[:, o[4]:o[6]], pad, w[:, o[6]:o[9]]]
    return jnp.concatenate(parts, axis=1).astype(BF16)


def _const_rows(q_g, k_g, i_g, i_b):
    lane = np.arange(LANES)
    inv_m = ROPE_THETA ** (-jnp.arange(0, HEAD_DIM, 2, dtype=F32) / HEAD_DIM)
    inv_i = ROPE_THETA ** (-jnp.arange(0, IDX_ROPE_DIM, 2, dtype=F32) / IDX_ROPE_DIM)
    in_rope = (lane % IDX_DIM) < IDX_ROPE_DIM
    zeros = jnp.zeros((LANES - IDX_DIM,), F32)
    rows = [
        inv_m[lane % (HEAD_DIM // 2)],
        jnp.asarray(np.where(lane % HEAD_DIM < HEAD_DIM // 2, -1.0, 1.0), F32),
        jnp.where(jnp.asarray(in_rope), inv_i[lane % (IDX_ROPE_DIM // 2)], 0.0),
        jnp.asarray(np.where(lane % IDX_ROPE_DIM < IDX_ROPE_DIM // 2, -1.0, 1.0), F32),
        jnp.tile(q_g.astype(F32), LANES // HEAD_DIM),
        jnp.tile(k_g.astype(F32), LANES // HEAD_DIM),
        jnp.concatenate([i_g.astype(F32), zeros]),
        jnp.concatenate([i_b.astype(F32), zeros]),
    ]
    return jnp.stack(rows, axis=0)


def _ffn_schedule(counts, n_blocks):
    gend = jnp.cumsum(counts).astype(I32)
    gstart = gend - counts
    first_blk = gstart // ROW_BLOCK
    last_blk = (jnp.maximum(gend, 1) - 1) // ROW_BLOCK
    n_items = jnp.where(counts > 0, last_blk - first_blk + 1, 0)
    item_end = jnp.cumsum(n_items).astype(I32)
    item_start = item_end - n_items
    total = item_end[-1]
    w = jnp.minimum(jnp.arange(n_blocks + N_EXPERTS - 1, dtype=I32), total - 1)
    item_e = jnp.sum((item_end[None, :] <= w[:, None]).astype(I32), axis=1)
    item_blk = first_blk[item_e] + w - item_start[item_e]
    return gstart, gend, item_e, item_blk, total.reshape(1)


def _layer(x, positions, norm1_g, w_in, q_norm_g, k_norm_g, idx_k_norm_g, idx_k_norm_b, w_branch_attn,
           pool_mix_w, pool_scale, w_branch_pool, w_out, norm2_g, router_w, router_b, w_gate_up,
           b_gate_up, w_down, b_down):
    b, s, d = x.shape
    t = b * s
    assert (t * TOP_K) % ROW_BLOCK == 0 and s % QUERY_BLOCK == 0
    lane = np.arange(LANES)
    gsum = jnp.asarray(lane[:, None] // HEAD_DIM == lane[None, :] // HEAD_DIM, BF16)

    qt, k, vt, qit, ki, wit, pool, gates = _proj_call(
        x, positions.reshape(b, s, 1), norm1_g.reshape(1, d), _pack_w_in(w_in),
        _const_rows(q_norm_g, k_norm_g, idx_k_norm_g, idx_k_norm_b), gsum)
    attn = _attn_call(qt, k, vt, qit, ki, wit)
    x1 = _merge_call(x, attn, pool, gates, w_branch_attn.astype(BF16), pool_mix_w.astype(BF16),
                     pool_scale.reshape(1, POOL_WIDTH), w_branch_pool.astype(BF16), w_out.astype(BF16))
    x1 = x1.reshape(t, d)

    tm_r = min(512, t)
    tri = jnp.asarray(np.tril(np.ones((tm_r, tm_r), np.float32), -1), BF16)
    rw = jnp.pad(router_w, ((0, 0), (0, LANES - N_EXPERTS))).astype(BF16)
    rb = jnp.pad(router_b, (0, LANES - N_EXPERTS)).reshape(1, LANES)
    h2, meta, rgate, counts = _router_call(x1, norm2_g.reshape(1, d), rw, rb, tri)

    gstart, gend, item_e, item_blk, nact = _ffn_schedule(counts[0, :N_EXPERTS], t * TOP_K // ROW_BLOCK)
    meta_flat = meta[:, :2 * TOP_K].reshape(t * 2 * TOP_K)
    xs = _dispatch_call(gstart, meta_flat, h2)
    ys = _ffn_call(item_e, item_blk, nact, gstart, gend, xs, w_gate_up,
                   b_gate_up.reshape(N_EXPERTS, 1, 2 * EXPERT_DIM), w_down, b_down.reshape(N_EXPERTS, 1, d))
    out = _combine_call(gstart, meta_flat, ys, rgate, x1)
    return out.reshape(b, s, d)


def kernel(x, positions, norm1_g, w_in, q_norm_g, k_norm_g, idx_k_norm_g, idx_k_norm_b, w_branch_attn,
           pool_mix_w, pool_scale, w_branch_pool, w_out, norm2_g, router_w, router_b, w_gate_up,
           b_gate_up, w_down, b_down):
    for l in range(norm1_g.shape[0]):
        x = _layer(x, positions, norm1_g[l], w_in[l], q_norm_g[l], k_norm_g[l], idx_k_norm_g[l],
                   idx_k_norm_b[l], w_branch_attn[l], pool_mix_w[l], pool_scale[l], w_branch_pool[l],
                   w_out[l], norm2_g[l], router_w[l], router_b[l], w_gate_up[l], b_gate_up[l],
                   w_down[l], b_down[l])
    return x
```

```python
import functools

import numpy as np
import jax
import jax.numpy as jnp
from jax import lax
from jax.experimental import pallas as pl
from jax.experimental.pallas import tpu as pltpu

F32 = jnp.float32
BF16 = jnp.bfloat16
I32 = jnp.int32

D_MODEL = 1024
CHUNK = 64
CHUNK_SHIFT = CHUNK.bit_length() - 1
ATTN_HEADS = 8
ATTN_KV_HEADS = 2
HEAD_DIM = 64
GROUPS = ATTN_HEADS // ATTN_KV_HEADS
ATTN_WIDTH = ATTN_HEADS * HEAD_DIM
KV_WIDTH = ATTN_KV_HEADS * HEAD_DIM
ATTN_SCALE = HEAD_DIM ** -0.5
IDX_HEADS = 4
IDX_DIM = 64
IDX_ROPE_DIM = 32
IDX_SCALE = (IDX_HEADS ** -0.5) * (IDX_DIM ** -0.5)
IDX_TOPK_MAX = 256
QUERY_BLOCK = 128
POOL_WINDOWS = (2, 4, 8, 16)
POOL_WIDTH = 512
POOL_GROUP_DIM = 128
POOL_HALO = 16
N_EXPERTS = 32
TOP_K = 4
EXPERT_DIM = 1024
SWIGLU_ALPHA = 1.702
SWIGLU_LIMIT = 7.0
ROPE_THETA = 10000.0
NORM_EPS = 1e-6

LANES = 128
SUBLANES = 8
VMEM_LIMIT = 56 * 1024 * 1024
FLT_MAX = float(np.finfo(np.float32).max)
MASKED = -1e30

C_Q = 0
C_K = 512
C_V = 640
C_QI = 768
C_KIW = 1024
C_POOL = 1152
C_GATE = 1664
W_PACKED = 3712
C_SMALL_END = C_POOL

ROW_BLOCK = 512
FOLD_ROWS = 8 * SUBLANES
VALUE_BISECT_ITERS = 16
BISECT_CAP = 64
BISECT_HEAD = 12
BISECT_UNROLL = 2
ATT_CK = 256
VT_ROWS = HEAD_DIM + 2 * SUBLANES
LOG2E = float(np.log2(np.e))


def _cparams(*sem):
    return pltpu.CompilerParams(dimension_semantics=sem, vmem_limit_bytes=VMEM_LIMIT)


def _swap_halves(xc, first, half):
    return jnp.where(first, pltpu.roll(xc, LANES - half, 1), pltpu.roll(xc, half, 1))


def _proj_kernel(x_ref, pos_ref, g1_ref, w_ref, rows_ref, gsum_ref,
                 qt_ref, k_ref, vt_ref, qit_ref, ki_ref, wit_ref, pool_ref, gate_ref):
    x = x_ref[0]
    tm = x.shape[0]
    ms = jnp.mean(x * x, axis=-1, keepdims=True)
    h = (x * lax.rsqrt(ms + NORM_EPS) * g1_ref[...]).astype(BF16)
    d1 = jnp.dot(h, w_ref[:, 0:C_SMALL_END], preferred_element_type=F32)
    pool_ref[0] = jnp.dot(h, w_ref[:, C_POOL:C_GATE], preferred_element_type=F32)
    gate_ref[0] = jnp.dot(h, w_ref[:, C_GATE:W_PACKED], preferred_element_type=F32)

    pos = pos_ref[0].astype(F32)
    rows = rows_ref[...]
    lane = lax.broadcasted_iota(I32, (tm, LANES), 1)
    first_m = (lane & (HEAD_DIM - 1)) < HEAD_DIM // 2
    first_i = (lane & (IDX_ROPE_DIM - 1)) < IDX_ROPE_DIM // 2
    ang_m = pos * rows[0:1]
    cos_m = jnp.cos(ang_m)
    sin_m = jnp.sin(ang_m) * rows[1:2]
    ang_i = pos * rows[2:3]
    cos_i = jnp.cos(ang_i)
    sin_i = jnp.sin(ang_i) * rows[3:4]
    gsum = gsum_ref[...]

    def head_rms(xc, grow):
        sq = xc * xc
        hi = sq.astype(BF16)
        lo = (sq - hi.astype(F32)).astype(BF16)
        ssum = (jnp.dot(hi, gsum, preferred_element_type=F32)
                + jnp.dot(lo, gsum, preferred_element_type=F32))
        return xc * lax.rsqrt(ssum * (1.0 / HEAD_DIM) + NORM_EPS) * grow

    def rope_m(xc):
        return xc * cos_m + _swap_halves(xc, first_m, HEAD_DIM // 2) * sin_m

    def rope_i(xc):
        return xc * cos_i + _swap_halves(xc, first_i, IDX_ROPE_DIM // 2) * sin_i

    for c in range(ATTN_WIDTH // LANES):
        qc = d1[:, C_Q + c * LANES:C_Q + (c + 1) * LANES]
        qr = rope_m(head_rms(qc, rows[4:5])) * (ATTN_SCALE * LOG2E)
        qt_ref[0, c * LANES:(c + 1) * LANES, :] = qr.T.astype(BF16)

    kr = rope_m(head_rms(d1[:, C_K:C_K + KV_WIDTH], rows[5:6]))
    vt = d1[:, C_V:C_V + KV_WIDTH].T
    for r in range(ATTN_KV_HEADS):
        k_ref[0, r] = kr[:, r * HEAD_DIM:(r + 1) * HEAD_DIM].astype(BF16)
        vt_ref[0, r, 0:HEAD_DIM, :] = vt[r * HEAD_DIM:(r + 1) * HEAD_DIM].astype(BF16)
        vt_ref[0, r, HEAD_DIM:VT_ROWS, :] = jnp.ones((VT_ROWS - HEAD_DIM, tm), BF16)

    for c in range(IDX_HEADS * IDX_DIM // LANES):
        qc = d1[:, C_QI + c * LANES:C_QI + (c + 1) * LANES]
        qit_ref[0, c * LANES:(c + 1) * LANES, :] = rope_i(qc).T.astype(BF16)

    kiw = d1[:, C_KIW:C_KIW + LANES]
    in_ki = lane < IDX_DIM
    mu = jnp.sum(jnp.where(in_ki, kiw, 0.0), axis=-1, keepdims=True) * (1.0 / IDX_DIM)
    dv = jnp.where(in_ki, kiw - mu, 0.0)
    var = jnp.sum(dv * dv, axis=-1, keepdims=True) * (1.0 / IDX_DIM)
    kin = dv * lax.rsqrt(var + NORM_EPS) * rows[6:7] + rows[7:8]
    ki_ref[0] = rope_i(kin)[:, 0:IDX_DIM].astype(BF16)
    wit_ref[0] = (kiw * IDX_SCALE).T[IDX_DIM:IDX_DIM + SUBLANES]


def _proj_call(x, pos3, g1, w_packed, rows, gsum):
    b, s, d = x.shape
    tm = min(512, s)
    grid = (b, s // tm)
    full2 = lambda i, j: (0, 0)
    out_shape = (
        jax.ShapeDtypeStruct((b, ATTN_WIDTH, s), BF16),
        jax.ShapeDtypeStruct((b, ATTN_KV_HEADS, s, HEAD_DIM), BF16),
        jax.ShapeDtypeStruct((b, ATTN_KV_HEADS, VT_ROWS, s), BF16),
        jax.ShapeDtypeStruct((b, IDX_HEADS * IDX_DIM, s), BF16),
        jax.ShapeDtypeStruct((b, s, IDX_DIM), BF16),
        jax.ShapeDtypeStruct((b, SUBLANES, s), F32),
        jax.ShapeDtypeStruct((b, s, POOL_WIDTH), F32),
        jax.ShapeDtypeStruct((b, s, 2 * D_MODEL), F32),
    )
    in_specs = [
        pl.BlockSpec((1, tm, d), lambda i, j: (i, j, 0)),
        pl.BlockSpec((1, tm, 1), lambda i, j: (i, j, 0)),
        pl.BlockSpec((1, d), full2),
        pl.BlockSpec((d, W_PACKED), full2),
        pl.BlockSpec((8, LANES), full2),
        pl.BlockSpec((LANES, LANES), full2),
    ]
    out_specs = (
        pl.BlockSpec((1, ATTN_WIDTH, tm), lambda i, j: (i, 0, j)),
        pl.BlockSpec((1, ATTN_KV_HEADS, tm, HEAD_DIM), lambda i, j: (i, 0, j, 0)),
        pl.BlockSpec((1, ATTN_KV_HEADS, VT_ROWS, tm), lambda i, j: (i, 0, 0, j)),
        pl.BlockSpec((1, IDX_HEADS * IDX_DIM, tm), lambda i, j: (i, 0, j)),
        pl.BlockSpec((1, tm, IDX_DIM), lambda i, j: (i, j, 0)),
        pl.BlockSpec((1, SUBLANES, tm), lambda i, j: (i, 0, j)),
        pl.BlockSpec((1, tm, POOL_WIDTH), lambda i, j: (i, j, 0)),
        pl.BlockSpec((1, tm, 2 * D_MODEL), lambda i, j: (i, j, 0)),
    )
    return pl.pallas_call(
        _proj_kernel, grid=grid, in_specs=in_specs, out_specs=out_specs, out_shape=out_shape,
        compiler_params=_cparams("parallel", "parallel"), name="proj",
    )(x, pos3, g1, w_packed, rows, gsum)


def _sort_key(x):
    b = lax.bitcast_convert_type(x, I32)
    return b ^ ((b >> 31) & 0x7FFFFFFF)


def _unsort_key(k):
    return lax.bitcast_convert_type(k ^ ((k >> 31) & 0x7FFFFFFF), F32)


def _attn_kernel(qt_ref, qit_ref, wit_ref, ki_ref, k_ref, vt_ref, tri_ref, o_ref, sc_ref, sa_ref, sb_ref,
                 *, n_sel, ck):
    qb = QUERY_BLOCK
    t0 = pl.program_id(1) * qb
    nck = (t0 + qb + ck - 1) // ck
    qpos = t0 + lax.broadcasted_iota(I32, (1, qb), 1)
    cend = ((qpos >> CHUNK_SHIFT) + 1) * CHUNK
    key_ck = lax.broadcasted_iota(I32, (ck, qb), 0)
    k_sel = float(n_sel)

    def fold(x, op):
        return op(x.reshape(ck // FOLD_ROWS, FOLD_ROWS, qb), axis=0)

    qit = qit_ref[0]
    rhs_i = jnp.concatenate([qit[h * IDX_DIM:(h + 1) * IDX_DIM] for h in range(IDX_HEADS)], axis=1)
    wit = wit_ref[0]

    def score_body(c, carry):
        mxp, mnp, c0p, c1p = carry
        off = pl.multiple_of(c * ck, ck)
        lg = jnp.dot(ki_ref[0, pl.ds(off, ck), :], rhs_i, preferred_element_type=F32)
        sc = jnp.maximum(lg[:, 0:qb], 0.0) * wit[0:1]
        for h in range(1, IDX_HEADS):
            sc = sc + jnp.maximum(lg[:, h * qb:(h + 1) * qb], 0.0) * wit[h:h + 1]
        adm = (off + key_ck) < cend
        sc = jnp.where(adm, sc, -jnp.inf)
        sc_ref[pl.ds(off, ck), :] = sc
        mxp = jnp.maximum(mxp, fold(sc, jnp.max))
        mnp = jnp.minimum(mnp, fold(jnp.where(adm, sc, jnp.inf), jnp.min))
        c0p = c0p + fold(jnp.where(sc >= 0.0, 1.0, 0.0), jnp.sum)
        c1p = c1p + fold(jnp.where(sc > 0.0, 1.0, 0.0), jnp.sum)
        return mxp, mnp, c0p, c1p

    init = (jnp.full((FOLD_ROWS, qb), -jnp.inf, F32), jnp.full((FOLD_ROWS, qb), jnp.inf, F32),
            jnp.zeros((FOLD_ROWS, qb), F32), jnp.zeros((FOLD_ROWS, qb), F32))
    mxp, mnp, c0p, c1p = lax.fori_loop(0, nck, score_body, init)
    mx = jnp.max(mxp, axis=0, keepdims=True)
    mn = jnp.min(mnp, axis=0, keepdims=True)
    c0 = jnp.sum(c0p, axis=0, keepdims=True)
    c1 = jnp.sum(c1p, axis=0, keepdims=True)

    def count(pred):
        def body(c, acc):
            off = pl.multiple_of(c * ck, ck)
            hit = pred(sc_ref[pl.ds(off, ck), :], off + key_ck)
            return acc + fold(jnp.where(hit, 1.0, 0.0), jnp.sum)
        acc = lax.fori_loop(0, nck, body, jnp.zeros((FOLD_ROWS, qb), F32))
        return jnp.sum(acc, axis=0, keepdims=True)

    small = cend.astype(F32) <= k_sel
    at_zero = jnp.logical_and(jnp.logical_not(small), jnp.logical_and(c1 < k_sel, c0 >= k_sel))
    positive = jnp.logical_and(jnp.logical_not(small), c1 >= k_sel)
    lo0 = jnp.where(positive, 0.0, mn)
    hi0 = jnp.where(positive, jnp.minimum(2.0 * mx, FLT_MAX), 0.0)
    thr0 = jnp.where(small, -FLT_MAX, 0.0)
    done0 = jnp.where(jnp.logical_or(small, at_zero), 1.0, 0.0)
    tie0 = jnp.where(jnp.logical_and(at_zero, c0 > k_sel), 1.0, 0.0)
    left0 = jnp.sum(1.0 - done0)

    def bisect_cond(st):
        it, left = st[0], st[1]
        return jnp.logical_and(it < BISECT_CAP, left > 0.0)

    def bisect_step(it, lo, hi, thr, done, tie):
        mid_v = 0.5 * lo + 0.5 * hi
        klo = _sort_key(lo)
        khi = _sort_key(hi)
        mid_k = _unsort_key((klo & khi) + ((klo ^ khi) >> 1))
        mid = jnp.where(it < VALUE_BISECT_ITERS, mid_v, mid_k)
        stuck = jnp.logical_or(mid <= lo, mid >= hi)
        cnt = count(lambda s, _: s >= mid)
        active = done == 0.0
        moving = jnp.logical_and(active, jnp.logical_not(stuck))
        hit = jnp.logical_and(moving, cnt == k_sel)
        new_tie = jnp.logical_and(active, stuck)
        thr = jnp.where(hit, mid, jnp.where(new_tie, lo, thr))
        tie = jnp.where(new_tie, 1.0, tie)
        done = jnp.where(jnp.logical_or(hit, new_tie), 1.0, done)
        upd = jnp.logical_and(moving, jnp.logical_not(hit))
        lo = jnp.where(jnp.logical_and(upd, cnt >= k_sel), mid, lo)
        hi = jnp.where(jnp.logical_and(upd, cnt < k_sel), mid, hi)
        return lo, hi, thr, done, tie

    def bisect_body(steps, st):
        it, _, lo, hi, thr, done, tie = st
        for u in range(steps):
            lo, hi, thr, done, tie = bisect_step(it + u, lo, hi, thr, done, tie)
        return it + steps, jnp.sum(1.0 - done), lo, hi, thr, done, tie

    st = (jnp.int32(0), left0, lo0, hi0, thr0, done0, tie0)
    st = lax.while_loop(lambda s: jnp.logical_and(s[0] < BISECT_HEAD, s[1] > 0.0),
                        functools.partial(bisect_body, BISECT_HEAD), st)
    st = lax.while_loop(bisect_cond, functools.partial(bisect_body, BISECT_UNROLL), st)
    thr, tie = st[4], st[6]

    @pl.when(jnp.sum(tie) > 0.0)
    def _():
        need = jnp.where(tie > 0.0, k_sel - count(lambda s, _: s > thr), FLT_MAX)

        def drop_body(c, before):
            off = pl.multiple_of(c * ck, ck)
            blk = sc_ref[pl.ds(off, ck), :]
            eq = blk == thr
            eqf = jnp.where(eq, 1.0, 0.0)
            rank = before + jnp.dot(tri_ref[...], eqf.astype(BF16), preferred_element_type=F32)
            sc_ref[pl.ds(off, ck), :] = jnp.where(jnp.logical_and(eq, rank >= need), -jnp.inf, blk)
            return before + jnp.sum(fold(eqf, jnp.sum), axis=0, keepdims=True)

        lax.fori_loop(0, nck, drop_body, jnp.zeros((1, qb), F32))

    qt = qt_ref[0]
    rhs = [jnp.concatenate(
        [qt[(r * GROUPS + g) * HEAD_DIM:(r * GROUPS + g + 1) * HEAD_DIM] for g in range(GROUPS)],
        axis=1) for r in range(ATTN_KV_HEADS)]

    n_att = (t0 + qb + ATT_CK - 1) // ATT_CK

    def step_offset(step):
        return pl.multiple_of(jnp.minimum(step, n_att - 1) * ATT_CK, ATT_CK)

    def qk_logits(step, buf):
        off = step_offset(step)
        for r in range(ATTN_KV_HEADS):
            buf[r] = jnp.dot(k_ref[0, r, pl.ds(off, ATT_CK), :], rhs[r], preferred_element_type=F32)

    def softmax_pv(step, buf, carry):
        off = step_offset(step)
        thr_step = jnp.where(step < n_att, thr, jnp.inf)
        bias = jnp.where(sc_ref[pl.ds(off, ATT_CK), :] >= thr_step, 0.0, MASKED)
        bias = jnp.concatenate([bias] * GROUPS, axis=1)
        out = []
        for r in range(ATTN_KV_HEADS):
            m, acc = carry[r]
            s = buf[r] + bias
            m_new = jnp.maximum(m, jnp.max(s, axis=0, keepdims=True))
            p = jnp.exp2(s - m_new).astype(BF16)
            pv = jnp.dot(vt_ref[0, r, :, pl.ds(off, ATT_CK)], p, preferred_element_type=F32)
            out.append((m_new, jnp.exp2(m - m_new) * acc + pv))
        return tuple(out)

    def att_body(i, carry):
        qk_logits(2 * i + 1, sb_ref)
        carry = softmax_pv(2 * i, sa_ref, carry)
        qk_logits(2 * i + 2, sa_ref)
        return softmax_pv(2 * i + 1, sb_ref, carry)

    init_a = tuple((jnp.full((1, GROUPS * qb), MASKED, F32), jnp.zeros((VT_ROWS, GROUPS * qb), F32))
                   for _ in range(ATTN_KV_HEADS))
    qk_logits(0, sa_ref)
    fin = lax.fori_loop(0, (n_att + 1) // 2, att_body, init_a)
    outs = []
    for r in range(ATTN_KV_HEADS):
        acc = fin[r][1]
        o = acc[0:HEAD_DIM] / acc[HEAD_DIM:HEAD_DIM + 1]
        outs.extend(o[:, g * qb:(g + 1) * qb] for g in range(GROUPS))
    o_ref[0] = jnp.concatenate(outs, axis=0).T.astype(BF16)


def _attn_call(qt, k, vt, qit, ki, wit):
    b, _, s = qt.shape
    n_sel = min(IDX_TOPK_MAX, s // 4)
    ck = min(512, s)
    grid = (b, s // QUERY_BLOCK)
    in_specs = [
        pl.BlockSpec((1, ATTN_WIDTH, QUERY_BLOCK), lambda i, j: (i, 0, j)),
        pl.BlockSpec((1, IDX_HEADS * IDX_DIM, QUERY_BLOCK), lambda i, j: (i, 0, j)),
        pl.BlockSpec((1, SUBLANES, QUERY_BLOCK), lambda i, j: (i, 0, j)),
        pl.BlockSpec((1, s, IDX_DIM), lambda i, j: (i, 0, 0)),
        pl.BlockSpec((1, ATTN_KV_HEADS, s, HEAD_DIM), lambda i, j: (i, 0, 0, 0)),
        pl.BlockSpec((1, ATTN_KV_HEADS, VT_ROWS, s), lambda i, j: (i, 0, 0, 0)),
        pl.BlockSpec((ck, ck), lambda i, j: (0, 0)),
    ]
    tri = jnp.asarray(np.tril(np.ones((ck, ck), np.float32), -1), BF16)
    return pl.pallas_call(
        functools.partial(_attn_kernel, n_sel=n_sel, ck=ck),
        grid=grid, in_specs=in_specs,
        out_specs=pl.BlockSpec((1, QUERY_BLOCK, ATTN_WIDTH), lambda i, j: (i, j, 0)),
        out_shape=jax.ShapeDtypeStruct((b, s, ATTN_WIDTH), BF16),
        scratch_shapes=[pltpu.VMEM((s, QUERY_BLOCK), F32),
                        pltpu.VMEM((ATTN_KV_HEADS, ATT_CK, GROUPS * QUERY_BLOCK), F32),
                        pltpu.VMEM((ATTN_KV_HEADS, ATT_CK, GROUPS * QUERY_BLOCK), F32)],
        compiler_params=_cparams("parallel", "parallel"), name="attn",
    )(qt, qit, wit, ki, k, vt, tri)


def _merge_kernel(x_ref, attn_ref, pool_ref, gate_ref, wba_ref, mix_ref, pscale_ref, wbp_ref, wout_ref,
                  o_ref, ext_ref):
    tm = x_ref.shape[1]
    j = pl.program_id(1)

    @pl.when(j == 0)
    def _():
        ext_ref[0:POOL_HALO] = jnp.zeros((POOL_HALO, POOL_WIDTH), F32)

    @pl.when(j > 0)
    def _():
        ext_ref[0:POOL_HALO] = ext_ref[tm:tm + POOL_HALO]

    ext_ref[POOL_HALO:POOL_HALO + tm] = pool_ref[0]
    t = j * tm + lax.broadcasted_iota(I32, (tm, 1), 0)
    mixed = []
    for g, w in enumerate(POOL_WINDOWS):
        cols = slice(g * POOL_GROUP_DIM, (g + 1) * POOL_GROUP_DIM)
        cur = ext_ref[POOL_HALO:POOL_HALO + tm, cols]
        wsum = cur
        for i in range(1, w):
            wsum = wsum + ext_ref[POOL_HALO - i:POOL_HALO - i + tm, cols]
        cnt = jnp.minimum(t + 1, w).astype(F32)
        dev = (wsum / cnt - cur).astype(BF16)
        mixed.append(jnp.dot(dev, mix_ref[g], preferred_element_type=F32))
    pooled = (jnp.concatenate(mixed, axis=1) * pscale_ref[...]).astype(BF16)
    branch_a = jnp.dot(attn_ref[0], wba_ref[...], preferred_element_type=F32)
    branch_p = jnp.dot(pooled, wbp_ref[...], preferred_element_type=F32)
    gates = gate_ref[0]
    merged = (jax.nn.sigmoid(gates[:, 0:D_MODEL]) * branch_a
              + jax.nn.sigmoid(gates[:, D_MODEL:2 * D_MODEL]) * branch_p)
    o_ref[0] = x_ref[0] + jnp.dot(merged.astype(BF16), wout_ref[...], preferred_element_type=F32)


def _merge_call(x, attn, pool, gates, wba, mix, pscale, wbp, wout):
    b, s, d = x.shape
    tm = min(256, s)
    grid = (b, s // tm)
    tile = lambda w: pl.BlockSpec((1, tm, w), lambda i, j: (i, j, 0))
    full2 = lambda i, j: (0, 0)
    in_specs = [
        tile(d), tile(ATTN_WIDTH), tile(POOL_WIDTH), tile(2 * D_MODEL),
        pl.BlockSpec((ATTN_WIDTH, d), full2),
        pl.BlockSpec((len(POOL_WINDOWS), POOL_GROUP_DIM, POOL_GROUP_DIM), lambda i, j: (0, 0, 0)),
        pl.BlockSpec((1, POOL_WIDTH), full2),
        pl.BlockSpec((POOL_WIDTH, d), full2),
        pl.BlockSpec((d, d), full2),
    ]
    return pl.pallas_call(
        _merge_kernel, grid=grid, in_specs=in_specs, out_specs=tile(d),
        out_shape=jax.ShapeDtypeStruct((b, s, d), F32),
        scratch_shapes=[pltpu.VMEM((POOL_HALO + tm, POOL_WIDTH), F32)],
        compiler_params=_cparams("parallel", "arbitrary"), name="merge",
    )(x, attn, pool, gates, wba, mix, pscale, wbp, wout)


def _router_kernel(x_ref, g2_ref, rw_ref, rb_ref, tri_ref, h_ref, meta_ref, gate_ref, cnt_ref, carry_ref):
    i = pl.program_id(0)
    x = x_ref[...]
    tm = x.shape[0]

    @pl.when(i == 0)
    def _():
        carry_ref[...] = jnp.zeros((1, LANES), F32)

    ms = jnp.mean(x * x, axis=-1, keepdims=True)
    h = x * lax.rsqrt(ms + NORM_EPS) * g2_ref[...]
    h_ref[...] = h
    logits = jnp.dot(h.astype(BF16), rw_ref[...], preferred_element_type=F32) + rb_ref[...]
    lane = lax.broadcasted_iota(I32, (tm, LANES), 1).astype(F32)
    work = jnp.where(lane < N_EXPERTS, logits, -jnp.inf)
    vals, idxs = [], []
    for _ in range(TOP_K):
        m = jnp.max(work, axis=-1, keepdims=True)
        idx = jnp.min(jnp.where(work == m, lane, float(LANES)), axis=-1, keepdims=True)
        vals.append(m)
        idxs.append(idx)
        work = jnp.where(lane == idx, -jnp.inf, work)
    exps = [jnp.exp(v - vals[0]) for v in vals]
    denom = exps[0] + exps[1] + exps[2] + exps[3]
    member = jnp.zeros((tm, LANES), F32)
    for idx in idxs:
        member = member + jnp.where(lane == idx, 1.0, 0.0)
    before = jnp.dot(tri_ref[...], member.astype(BF16), preferred_element_type=F32) + carry_ref[...]
    meta = jnp.zeros((tm, LANES), F32)
    gate = jnp.zeros((tm, LANES), F32)
    for k in range(TOP_K):
        rank = jnp.sum(jnp.where(lane == idxs[k], before, 0.0), axis=-1, keepdims=True)
        meta = jnp.where(lane == float(k), idxs[k], meta)
        meta = jnp.where(lane == float(TOP_K + k), rank, meta)
        gate = jnp.where(lane == float(k), exps[k] / denom, gate)
    meta_ref[...] = meta.astype(I32)
    gate_ref[...] = gate
    total = carry_ref[...] + jnp.sum(member, axis=0, keepdims=True)
    carry_ref[...] = total
    cnt_ref[...] = total.astype(I32)


def _router_call(x1, g2, rw, rb, tri):
    t, d = x1.shape
    tm = tri.shape[0]
    full2 = lambda i: (0, 0)
    tile = lambda w: pl.BlockSpec((tm, w), lambda i: (i, 0))
    out_shape = (
        jax.ShapeDtypeStruct((t, d), F32),
        jax.ShapeDtypeStruct((t, LANES), I32),
        jax.ShapeDtypeStruct((t, LANES), F32),
        jax.ShapeDtypeStruct((1, LANES), I32),
    )
    return pl.pallas_call(
        _router_kernel, grid=(t // tm,),
        in_specs=[tile(d), pl.BlockSpec((1, d), full2), pl.BlockSpec((d, LANES), full2),
                  pl.BlockSpec((1, LANES), full2), pl.BlockSpec((tm, tm), full2)],
        out_specs=(tile(d), tile(LANES), tile(LANES), pl.BlockSpec((1, LANES), full2)),
        out_shape=out_shape,
        scratch_shapes=[pltpu.VMEM((1, LANES), F32)],
        compiler_params=_cparams("arbitrary"), name="router",
    )(x1, g2, rw, rb, tri)


ROW_UNROLL = SUBLANES


def _dispatch_kernel(slot_ref, h_ref, xs_ref, sem, *, tm):
    def issue(i, carry):
        for u in range(ROW_UNROLL):
            j = i * ROW_UNROLL + u
            for k in range(TOP_K):
                pltpu.make_async_copy(h_ref.at[pl.ds(j, 1)], xs_ref.at[pl.ds(slot_ref[j * TOP_K + k], 1)],
                                      sem).start(priority=k % 2)
        return carry

    lax.fori_loop(0, tm // ROW_UNROLL, issue, 0)

    def drain(j, carry):
        for k in range(TOP_K):
            pltpu.make_async_copy(h_ref.at[pl.ds(0, 1)], xs_ref.at[pl.ds(0, 1)], sem).wait()
        return carry

    lax.fori_loop(0, tm, drain, 0)


def _dispatch_call(slots, h2):
    t, d = h2.shape
    tm = min(256, t)
    return pl.pallas_call(
        functools.partial(_dispatch_kernel, tm=tm),
        grid=(t // tm,),
        in_specs=[pl.BlockSpec((tm * TOP_K,), lambda i: (i,), memory_space=pltpu.SMEM),
                  pl.BlockSpec((tm, d), lambda i: (i, 0))],
        out_specs=pl.BlockSpec(memory_space=pl.ANY),
        scratch_shapes=[pltpu.SemaphoreType.DMA(())],
        out_shape=jax.ShapeDtypeStruct((t * TOP_K, d), F32),
        compiler_params=_cparams("arbitrary"), name="dispatch",
    )(slots, h2)


def _ffn_kernel(item_e_ref, item_blk_ref, nact_ref, gstart_ref, gend_ref,
                xs_ref, wgu_ref, bgu_ref, wd_ref, bd_ref, o_ref, wgu_bf, wd_bf):
    w = pl.program_id(0)
    e = item_e_ref[w]
    blk = item_blk_ref[w]
    prev = jnp.maximum(w - 1, 0)
    active = w < nact_ref[0]
    new_expert = jnp.logical_or(w == 0, e != item_e_ref[prev])
    first_visit = jnp.logical_or(w == 0, blk != item_blk_ref[prev])

    @pl.when(jnp.logical_and(active, new_expert))
    def _():
        wgu_bf[...] = wgu_ref[0].astype(BF16)
        wd_bf[...] = wd_ref[0].astype(BF16)

    @pl.when(active)
    def _():
        xb = xs_ref[...].astype(BF16)
        gu = jnp.dot(xb, wgu_bf[...], preferred_element_type=F32) + bgu_ref[0]
        gate = jnp.minimum(gu[:, 0:EXPERT_DIM], SWIGLU_LIMIT)
        up = jnp.clip(gu[:, EXPERT_DIM:2 * EXPERT_DIM], -SWIGLU_LIMIT, SWIGLU_LIMIT)
        act = gate * jax.nn.sigmoid(SWIGLU_ALPHA * gate) * (up + 1.0)
        res = jnp.dot(act.astype(BF16), wd_bf[...], preferred_element_type=F32) + bd_ref[0]
        row = blk * ROW_BLOCK + lax.broadcasted_iota(I32, (ROW_BLOCK, 1), 0)
        mine = jnp.logical_and(row >= gstart_ref[e], row < gend_ref[e])

        @pl.when(first_visit)
        def _():
            o_ref[...] = jnp.where(mine, res, 0.0)

        @pl.when(jnp.logical_not(first_visit))
        def _():
            o_ref[...] = jnp.where(mine, res, o_ref[...])


def _ffn_call(item_e, item_blk, nact, gstart, gend, xs, wgu, bgu, wd, bd):
    n_rows, d = xs.shape
    rows = lambda w, ie, ib, *_: (ib[w], 0)
    exp3 = lambda w, ie, *_: (ie[w], 0, 0)
    return pl.pallas_call(
        _ffn_kernel,
        grid_spec=pltpu.PrefetchScalarGridSpec(
            num_scalar_prefetch=5, grid=(item_e.shape[0],),
            in_specs=[pl.BlockSpec((ROW_BLOCK, d), rows),
                      pl.BlockSpec((1, d, 2 * EXPERT_DIM), exp3),
                      pl.BlockSpec((1, 1, 2 * EXPERT_DIM), exp3),
                      pl.BlockSpec((1, EXPERT_DIM, d), exp3),
                      pl.BlockSpec((1, 1, d), exp3)],
            out_specs=pl.BlockSpec((ROW_BLOCK, d), rows),
            scratch_shapes=[pltpu.VMEM((d, 2 * EXPERT_DIM), BF16), pltpu.VMEM((EXPERT_DIM, d), BF16)]),
        out_shape=jax.ShapeDtypeStruct((n_rows, d), F32),
        compiler_params=_cparams("arbitrary"), name="ffn",
    )(item_e, item_blk, nact, gstart, gend, xs, wgu, bgu, wd, bd)


def _combine_kernel(slot_ref, slot_next_ref, ys_ref, gate_ref, x_ref, o_ref, buf_ref, sem, *, tm, n_steps):
    i = pl.program_id(0)
    half = i % 2

    def gather(slots, dst_half):
        def issue(it, carry):
            for u in range(ROW_UNROLL):
                j = it * ROW_UNROLL + u
                for k in range(TOP_K):
                    pltpu.make_async_copy(ys_ref.at[pl.ds(slots[j * TOP_K + k], 1)],
                                          buf_ref.at[dst_half, k, pl.ds(j, 1)],
                                          sem.at[dst_half]).start(priority=k % 2)
            return carry
        lax.fori_loop(0, tm // ROW_UNROLL, issue, 0)

    @pl.when(i == 0)
    def _():
        gather(slot_ref, 0)

    @pl.when(i + 1 < n_steps)
    def _():
        gather(slot_next_ref, 1 - half)

    def drain(j, carry):
        for k in range(TOP_K):
            pltpu.make_async_copy(ys_ref.at[pl.ds(0, 1)], buf_ref.at[half, k, pl.ds(0, 1)],
                                  sem.at[half]).wait()
        return carry

    lax.fori_loop(0, tm, drain, 0)
    gate = gate_ref[...]
    y = x_ref[...]
    for k in range(TOP_K):
        y = y + gate[:, k:k + 1] * buf_ref[half, k]
    o_ref[...] = y


def _combine_call(slots, ys, gates, x1):
    t, d = x1.shape
    tm = min(256, t)
    n_steps = t // tm
    tile = lambda w: pl.BlockSpec((tm, w), lambda i: (i, 0))
    return pl.pallas_call(
        functools.partial(_combine_kernel, tm=tm, n_steps=n_steps),
        grid=(n_steps,),
        in_specs=[pl.BlockSpec((tm * TOP_K,), lambda i: (i,), memory_space=pltpu.SMEM),
                  pl.BlockSpec((tm * TOP_K,), lambda i: (jnp.minimum(i + 1, n_steps - 1),),
                               memory_space=pltpu.SMEM),
                  pl.BlockSpec(memory_space=pl.ANY), tile(LANES), tile(d)],
        out_specs=tile(d),
        scratch_shapes=[pltpu.VMEM((2, TOP_K, tm, d), F32), pltpu.SemaphoreType.DMA((2,))],
        out_shape=jax.ShapeDtypeStruct((t, d), F32),
        compiler_params=_cparams("arbitrary"), name="combine",
    )(slots, slots, ys, gates, x1)


def _pack_w_in(w):
    d = w.shape[0]
    o = np.cumsum([0, ATTN_WIDTH, KV_WIDTH, KV_WIDTH, IDX_HEADS * IDX_DIM, IDX_DIM, IDX_HEADS,
                   POOL_WIDTH, D_MODEL, D_MODEL])
    pad = jnp.zeros((d, LANES - IDX_DIM - IDX_HEADS), w.dtype)
    parts = [w[:, o[0]:o[4]], w[:, o[4]:o[6]], pad, w[:, o[6]:o[9]]]
    return jnp.concatenate(parts, axis=1).astype(BF16)


def _const_rows(q_g, k_g, i_g, i_b):
    lane = np.arange(LANES)
    inv_m = ROPE_THETA ** (-jnp.arange(0, HEAD_DIM, 2, dtype=F32) / HEAD_DIM)
    inv_i = ROPE_THETA ** (-jnp.arange(0, IDX_ROPE_DIM, 2, dtype=F32) / IDX_ROPE_DIM)
    in_rope = (lane % IDX_DIM) < IDX_ROPE_DIM
    zeros = jnp.zeros((LANES - IDX_DIM,), F32)
    rows = [
        inv_m[lane % (HEAD_DIM // 2)],
        jnp.asarray(np.where(lane % HEAD_DIM < HEAD_DIM // 2, -1.0, 1.0), F32),
        jnp.where(jnp.asarray(in_rope), inv_i[lane % (IDX_ROPE_DIM // 2)], 0.0),
        jnp.asarray(np.where(lane % IDX_ROPE_DIM < IDX_ROPE_DIM // 2, -1.0, 1.0), F32),
        jnp.tile(q_g.astype(F32), LANES // HEAD_DIM),
        jnp.tile(k_g.astype(F32), LANES // HEAD_DIM),
        jnp.concatenate([i_g.astype(F32), zeros]),
        jnp.concatenate([i_b.astype(F32), zeros]),
    ]
    return jnp.stack(rows, axis=0)


def _ffn_schedule(counts, n_blocks):
    gend = jnp.cumsum(counts).astype(I32)
    gstart = gend - counts
    first_blk = gstart // ROW_BLOCK
    last_blk = (jnp.maximum(gend, 1) - 1) // ROW_BLOCK
    n_items = jnp.where(counts > 0, last_blk - first_blk + 1, 0)
    item_end = jnp.cumsum(n_items).astype(I32)
    item_start = item_end - n_items
    total = item_end[-1]
    w = jnp.minimum(jnp.arange(n_blocks + N_EXPERTS - 1, dtype=I32), total - 1)
    item_e = jnp.sum((item_end[None, :] <= w[:, None]).astype(I32), axis=1)
    item_blk = first_blk[item_e] + w - item_start[item_e]
    return gstart, gend, item_e, item_blk, total.reshape(1)


def _layer(x, positions, norm1_g, w_in, q_norm_g, k_norm_g, idx_k_norm_g, idx_k_norm_b, w_branch_attn,
           pool_mix_w, pool_scale, w_branch_pool, w_out, norm2_g, router_w, router_b, w_gate_up,
           b_gate_up, w_down, b_down):
    b, s, d = x.shape
    t = b * s
    assert (t * TOP_K) % ROW_BLOCK == 0 and s % QUERY_BLOCK == 0
    lane = np.arange(LANES)
    gsum = jnp.asarray(lane[:, None] // HEAD_DIM == lane[None, :] // HEAD_DIM, BF16)

    qt, k, vt, qit, ki, wit, pool, gates = _proj_call(
        x, positions.reshape(b, s, 1), norm1_g.reshape(1, d), _pack_w_in(w_in),
        _const_rows(q_norm_g, k_norm_g, idx_k_norm_g, idx_k_norm_b), gsum)
    attn = _attn_call(qt, k, vt, qit, ki, wit)
    x1 = _merge_call(x, attn, pool, gates, w_branch_attn.astype(BF16), pool_mix_w.astype(BF16),
                     pool_scale.reshape(1, POOL_WIDTH), w_branch_pool.astype(BF16), w_out.astype(BF16))
    x1 = x1.reshape(t, d)

    tm_r = min(512, t)
    tri = jnp.asarray(np.tril(np.ones((tm_r, tm_r), np.float32), -1), BF16)
    rw = jnp.pad(router_w, ((0, 0), (0, LANES - N_EXPERTS))).astype(BF16)
    rb = jnp.pad(router_b, (0, LANES - N_EXPERTS)).reshape(1, LANES)
    h2, meta, rgate, counts = _router_call(x1, norm2_g.reshape(1, d), rw, rb, tri)

    gstart, gend, item_e, item_blk, nact = _ffn_schedule(counts[0, :N_EXPERTS], t * TOP_K // ROW_BLOCK)
    slots = (gstart[meta[:, 0:TOP_K]] + meta[:, TOP_K:2 * TOP_K]).reshape(t * TOP_K)
    xs = _dispatch_call(slots, h2)
    ys = _ffn_call(item_e, item_blk, nact, gstart, gend, xs, w_gate_up,
                   b_gate_up.reshape(N_EXPERTS, 1, 2 * EXPERT_DIM), w_down, b_down.reshape(N_EXPERTS, 1, d))
    out = _combine_call(slots, ys, rgate, x1)
    return out.reshape(b, s, d)


def kernel(x, positions, norm1_g, w_in, q_norm_g, k_norm_g, idx_k_norm_g, idx_k_norm_b, w_branch_attn,
           pool_mix_w, pool_scale, w_branch_pool, w_out, norm2_g, router_w, router_b, w_gate_up,
           b_gate_up, w_down, b_down):
    for l in range(norm1_g.shape[0]):
        x = _layer(x, positions, norm1_g[l], w_in[l], q_norm_g[l], k_norm_g[l], idx_k_norm_g[l],
                   idx_k_norm_b[l], w_branch_attn[l], pool_mix_w[l], pool_scale[l], w_branch_pool[l],
                   w_out[l], norm2_g[l], router_w[l], router_b[l], w_gate_up[l], b_gate_up[l],
                   w_down[l], b_down[l])
    return x
```

```python
import functools

import numpy as np
import jax
import jax.numpy as jnp
from jax import lax
from jax.experimental import pallas as pl
from jax.experimental.pallas import tpu as pltpu

F32 = jnp.float32
BF16 = jnp.bfloat16
I32 = jnp.int32

D_MODEL = 1024
CHUNK = 64
CHUNK_SHIFT = CHUNK.bit_length() - 1
ATTN_HEADS = 8
ATTN_KV_HEADS = 2
HEAD_DIM = 64
GROUPS = ATTN_HEADS // ATTN_KV_HEADS
ATTN_WIDTH = ATTN_HEADS * HEAD_DIM
KV_WIDTH = ATTN_KV_HEADS * HEAD_DIM
ATTN_SCALE = HEAD_DIM ** -0.5
IDX_HEADS = 4
IDX_DIM = 64
IDX_ROPE_DIM = 32
IDX_SCALE = (IDX_HEADS ** -0.5) * (IDX_DIM ** -0.5)
IDX_TOPK_MAX = 256
QUERY_BLOCK = 128
POOL_WINDOWS = (2, 4, 8, 16)
POOL_WIDTH = 512
POOL_GROUP_DIM = 128
POOL_HALO = 16
N_EXPERTS = 32
TOP_K = 4
EXPERT_DIM = 1024
SWIGLU_ALPHA = 1.702
SWIGLU_LIMIT = 7.0
ROPE_THETA = 10000.0
NORM_EPS = 1e-6

LANES = 128
SUBLANES = 8
VMEM_LIMIT = 56 * 1024 * 1024
FLT_MAX = float(np.finfo(np.float32).max)
MASKED = -1e30

C_Q = 0
C_K = 512
C_V = 640
C_QI = 768
C_KIW = 1024
C_POOL = 1152
C_GATE = 1664
W_PACKED = 3712
C_SMALL_END = C_POOL

ROW_BLOCK = 512
FOLD_ROWS = 8 * SUBLANES
VALUE_BISECT_ITERS = 16
BISECT_CAP = 64
BISECT_HEAD = 12
BISECT_UNROLL = 2
ATT_CK = 256
VT_ROWS = HEAD_DIM + 2 * SUBLANES
LOG2E = float(np.log2(np.e))


def _cparams(*sem):
    return pltpu.CompilerParams(dimension_semantics=sem, vmem_limit_bytes=VMEM_LIMIT)


def _swap_halves(xc, first, half):
    return jnp.where(first, pltpu.roll(xc, LANES - half, 1), pltpu.roll(xc, half, 1))


def _proj_kernel(x_ref, pos_ref, g1_ref, w_ref, rows_ref, gsum_ref,
                 qt_ref, k_ref, vt_ref, qit_ref, ki_ref, wit_ref, pool_ref, gate_ref):
    x = x_ref[0]
    tm = x.shape[0]
    ms = jnp.mean(x * x, axis=-1, keepdims=True)
    h = (x * lax.rsqrt(ms + NORM_EPS) * g1_ref[...]).astype(BF16)
    d1 = jnp.dot(h, w_ref[:, 0:C_SMALL_END], preferred_element_type=F32)
    pool_ref[0] = jnp.dot(h, w_ref[:, C_POOL:C_GATE], preferred_element_type=F32)
    gate_ref[0] = jnp.dot(h, w_ref[:, C_GATE:W_PACKED], preferred_element_type=F32)

    pos = pos_ref[0].astype(F32)
    rows = rows_ref[...]
    lane = lax.broadcasted_iota(I32, (tm, LANES), 1)
    first_m = (lane & (HEAD_DIM - 1)) < HEAD_DIM // 2
    first_i = (lane & (IDX_ROPE_DIM - 1)) < IDX_ROPE_DIM // 2
    ang_m = pos * rows[0:1]
    cos_m = jnp.cos(ang_m)
    sin_m = jnp.sin(ang_m) * rows[1:2]
    ang_i = pos * rows[2:3]
    cos_i = jnp.cos(ang_i)
    sin_i = jnp.sin(ang_i) * rows[3:4]
    gsum = gsum_ref[...]

    def head_rms(xc, grow):
        sq = xc * xc
        hi = sq.astype(BF16)
        lo = (sq - hi.astype(F32)).astype(BF16)
        ssum = (jnp.dot(hi, gsum, preferred_element_type=F32)
                + jnp.dot(lo, gsum, preferred_element_type=F32))
        return xc * lax.rsqrt(ssum * (1.0 / HEAD_DIM) + NORM_EPS) * grow

    def rope_m(xc):
        return xc * cos_m + _swap_halves(xc, first_m, HEAD_DIM // 2) * sin_m

    def rope_i(xc):
        return xc * cos_i + _swap_halves(xc, first_i, IDX_ROPE_DIM // 2) * sin_i

    for c in range(ATTN_WIDTH // LANES):
        qc = d1[:, C_Q + c * LANES:C_Q + (c + 1) * LANES]
        qr = rope_m(head_rms(qc, rows[4:5])) * (ATTN_SCALE * LOG2E)
        qt_ref[0, c * LANES:(c + 1) * LANES, :] = qr.T.astype(BF16)

    kr = rope_m(head_rms(d1[:, C_K:C_K + KV_WIDTH], rows[5:6]))
    vt = d1[:, C_V:C_V + KV_WIDTH].T
    for r in range(ATTN_KV_HEADS):
        k_ref[0, r] = kr[:, r * HEAD_DIM:(r + 1) * HEAD_DIM].astype(BF16)
        vt_ref[0, r, 0:HEAD_DIM, :] = vt[r * HEAD_DIM:(r + 1) * HEAD_DIM].astype(BF16)
        vt_ref[0, r, HEAD_DIM:VT_ROWS, :] = jnp.ones((VT_ROWS - HEAD_DIM, tm), BF16)

    for c in range(IDX_HEADS * IDX_DIM // LANES):
        qc = d1[:, C_QI + c * LANES:C_QI + (c + 1) * LANES]
        qit_ref[0, c * LANES:(c + 1) * LANES, :] = rope_i(qc).T.astype(BF16)

    kiw = d1[:, C_KIW:C_KIW + LANES]
    in_ki = lane < IDX_DIM
    mu = jnp.sum(jnp.where(in_ki, kiw, 0.0), axis=-1, keepdims=True) * (1.0 / IDX_DIM)
    dv = jnp.where(in_ki, kiw - mu, 0.0)
    var = jnp.sum(dv * dv, axis=-1, keepdims=True) * (1.0 / IDX_DIM)
    kin = dv * lax.rsqrt(var + NORM_EPS) * rows[6:7] + rows[7:8]
    ki_ref[0] = rope_i(kin)[:, 0:IDX_DIM].astype(BF16)
    wit_ref[0] = (kiw * IDX_SCALE).T[IDX_DIM:IDX_DIM + SUBLANES]


def _proj_call(x, pos3, g1, w_packed, rows, gsum):
    b, s, d = x.shape
    tm = min(512, s)
    grid = (b, s // tm)
    full2 = lambda i, j: (0, 0)
    out_shape = (
        jax.ShapeDtypeStruct((b, ATTN_WIDTH, s), BF16),
        jax.ShapeDtypeStruct((b, ATTN_KV_HEADS, s, HEAD_DIM), BF16),
        jax.ShapeDtypeStruct((b, ATTN_KV_HEADS, VT_ROWS, s), BF16),
        jax.ShapeDtypeStruct((b, IDX_HEADS * IDX_DIM, s), BF16),
        jax.ShapeDtypeStruct((b, s, IDX_DIM), BF16),
        jax.ShapeDtypeStruct((b, SUBLANES, s), F32),
        jax.ShapeDtypeStruct((b, s, POOL_WIDTH), F32),
        jax.ShapeDtypeStruct((b, s, 2 * D_MODEL), F32),
    )
    in_specs = [
        pl.BlockSpec((1, tm, d), lambda i, j: (i, j, 0)),
        pl.BlockSpec((1, tm, 1), lambda i, j: (i, j, 0)),
        pl.BlockSpec((1, d), full2),
        pl.BlockSpec((d, W_PACKED), full2),
        pl.BlockSpec((8, LANES), full2),
        pl.BlockSpec((LANES, LANES), full2),
    ]
    out_specs = (
        pl.BlockSpec((1, ATTN_WIDTH, tm), lambda i, j: (i, 0, j)),
        pl.BlockSpec((1, ATTN_KV_HEADS, tm, HEAD_DIM), lambda i, j: (i, 0, j, 0)),
        pl.BlockSpec((1, ATTN_KV_HEADS, VT_ROWS, tm), lambda i, j: (i, 0, 0, j)),
        pl.BlockSpec((1, IDX_HEADS * IDX_DIM, tm), lambda i, j: (i, 0, j)),
        pl.BlockSpec((1, tm, IDX_DIM), lambda i, j: (i, j, 0)),
        pl.BlockSpec((1, SUBLANES, tm), lambda i, j: (i, 0, j)),
        pl.BlockSpec((1, tm, POOL_WIDTH), lambda i, j: (i, j, 0)),
        pl.BlockSpec((1, tm, 2 * D_MODEL), lambda i, j: (i, j, 0)),
    )
    return pl.pallas_call(
        _proj_kernel, grid=grid, in_specs=in_specs, out_specs=out_specs, out_shape=out_shape,
        compiler_params=_cparams("parallel", "parallel"), name="proj",
    )(x, pos3, g1, w_packed, rows, gsum)


def _sort_key(x):
    b = lax.bitcast_convert_type(x, I32)
    return b ^ ((b >> 31) & 0x7FFFFFFF)


def _unsort_key(k):
    return lax.bitcast_convert_type(k ^ ((k >> 31) & 0x7FFFFFFF), F32)


def _attn_kernel(qt_ref, qit_ref, wit_ref, ki_ref, k_ref, vt_ref, tri_ref, o_ref, sc_ref, sa_ref, sb_ref,
                 *, n_sel, ck):
    qb = QUERY_BLOCK
    t0 = pl.program_id(1) * qb
    nck = (t0 + qb + ck - 1) // ck
    qpos = t0 + lax.broadcasted_iota(I32, (1, qb), 1)
    cend = ((qpos >> CHUNK_SHIFT) + 1) * CHUNK
    key_ck = lax.broadcasted_iota(I32, (ck, qb), 0)
    k_sel = float(n_sel)

    def fold(x, op):
        return op(x.reshape(ck // FOLD_ROWS, FOLD_ROWS, qb), axis=0)

    qit = qit_ref[0]
    rhs_i = jnp.concatenate([qit[h * IDX_DIM:(h + 1) * IDX_DIM] for h in range(IDX_HEADS)], axis=1)
    wit = wit_ref[0]

    def score_body(c, carry):
        mxp, mnp, c0p, c1p = carry
        off = pl.multiple_of(c * ck, ck)
        lg = jnp.dot(ki_ref[0, pl.ds(off, ck), :], rhs_i, preferred_element_type=F32)
        sc = jnp.maximum(lg[:, 0:qb], 0.0) * wit[0:1]
        for h in range(1, IDX_HEADS):
            sc = sc + jnp.maximum(lg[:, h * qb:(h + 1) * qb], 0.0) * wit[h:h + 1]
        adm = (off + key_ck) < cend
        sc = jnp.where(adm, sc, -jnp.inf)
        sc_ref[pl.ds(off, ck), :] = sc
        mxp = jnp.maximum(mxp, fold(sc, jnp.max))
        mnp = jnp.minimum(mnp, fold(jnp.where(adm, sc, jnp.inf), jnp.min))
        c0p = c0p + fold(jnp.where(sc >= 0.0, 1.0, 0.0), jnp.sum)
        c1p = c1p + fold(jnp.where(sc > 0.0, 1.0, 0.0), jnp.sum)
        return mxp, mnp, c0p, c1p

    init = (jnp.full((FOLD_ROWS, qb), -jnp.inf, F32), jnp.full((FOLD_ROWS, qb), jnp.inf, F32),
            jnp.zeros((FOLD_ROWS, qb), F32), jnp.zeros((FOLD_ROWS, qb), F32))
    mxp, mnp, c0p, c1p = lax.fori_loop(0, nck, score_body, init)
    mx = jnp.max(mxp, axis=0, keepdims=True)
    mn = jnp.min(mnp, axis=0, keepdims=True)
    c0 = jnp.sum(c0p, axis=0, keepdims=True)
    c1 = jnp.sum(c1p, axis=0, keepdims=True)

    def count(pred):
        def body(c, acc):
            off = pl.multiple_of(c * ck, ck)
            hit = pred(sc_ref[pl.ds(off, ck), :], off + key_ck)
            return acc + fold(jnp.where(hit, 1.0, 0.0), jnp.sum)
        acc = lax.fori_loop(0, nck, body, jnp.zeros((FOLD_ROWS, qb), F32))
        return jnp.sum(acc, axis=0, keepdims=True)

    small = cend.astype(F32) <= k_sel
    at_zero = jnp.logical_and(jnp.logical_not(small), jnp.logical_and(c1 < k_sel, c0 >= k_sel))
    positive = jnp.logical_and(jnp.logical_not(small), c1 >= k_sel)
    lo0 = jnp.where(positive, 0.0, mn)
    hi0 = jnp.where(positive, jnp.minimum(2.0 * mx, FLT_MAX), 0.0)
    thr0 = jnp.where(small, -FLT_MAX, 0.0)
    done0 = jnp.where(jnp.logical_or(small, at_zero), 1.0, 0.0)
    tie0 = jnp.where(jnp.logical_and(at_zero, c0 > k_sel), 1.0, 0.0)
    left0 = jnp.sum(1.0 - done0)

    def bisect_cond(st):
        it, left = st[0], st[1]
        return jnp.logical_and(it < BISECT_CAP, left > 0.0)

    def bisect_step(it, lo, hi, thr, done, tie):
        mid_v = 0.5 * lo + 0.5 * hi
        klo = _sort_key(lo)
        khi = _sort_key(hi)
        mid_k = _unsort_key((klo & khi) + ((klo ^ khi) >> 1))
        mid = jnp.where(it < VALUE_BISECT_ITERS, mid_v, mid_k)
        stuck = jnp.logical_or(mid <= lo, mid >= hi)
        cnt = count(lambda s, _: s >= mid)
        active = done == 0.0
        moving = jnp.logical_and(active, jnp.logical_not(stuck))
        hit = jnp.logical_and(moving, cnt == k_sel)
        new_tie = jnp.logical_and(active, stuck)
        thr = jnp.where(hit, mid, jnp.where(new_tie, lo, thr))
        tie = jnp.where(new_tie, 1.0, tie)
        done = jnp.where(jnp.logical_or(hit, new_tie), 1.0, done)
        upd = jnp.logical_and(moving, jnp.logical_not(hit))
        lo = jnp.where(jnp.logical_and(upd, cnt >= k_sel), mid, lo)
        hi = jnp.where(jnp.logical_and(upd, cnt < k_sel), mid, hi)
        return lo, hi, thr, done, tie

    def bisect_body(steps, st):
        it, _, lo, hi, thr, done, tie = st
        for u in range(steps):
            lo, hi, thr, done, tie = bisect_step(it + u, lo, hi, thr, done, tie)
        return it + steps, jnp.sum(1.0 - done), lo, hi, thr, done, tie

    st = (jnp.int32(0), left0, lo0, hi0, thr0, done0, tie0)
    st = lax.while_loop(lambda s: jnp.logical_and(s[0] < BISECT_HEAD, s[1] > 0.0),
                        functools.partial(bisect_body, BISECT_HEAD), st)
    st = lax.while_loop(bisect_cond, functools.partial(bisect_body, BISECT_UNROLL), st)
    thr, tie = st[4], st[6]

    @pl.when(jnp.sum(tie) > 0.0)
    def _():
        need = jnp.where(tie > 0.0, k_sel - count(lambda s, _: s > thr), FLT_MAX)

        def drop_body(c, before):
            off = pl.multiple_of(c * ck, ck)
            blk = sc_ref[pl.ds(off, ck), :]
            eq = blk == thr
            eqf = jnp.where(eq, 1.0, 0.0)
            rank = before + jnp.dot(tri_ref[...], eqf.astype(BF16), preferred_element_type=F32)
            sc_ref[pl.ds(off, ck), :] = jnp.where(jnp.logical_and(eq, rank >= need), -jnp.inf, blk)
            return before + jnp.sum(fold(eqf, jnp.sum), axis=0, keepdims=True)

        lax.fori_loop(0, nck, drop_body, jnp.zeros((1, qb), F32))

    qt = qt_ref[0]
    rhs = [jnp.concatenate(
        [qt[(r * GROUPS + g) * HEAD_DIM:(r * GROUPS + g + 1) * HEAD_DIM] for g in range(GROUPS)],
        axis=1) for r in range(ATTN_KV_HEADS)]

    n_att = (t0 + qb + ATT_CK - 1) // ATT_CK

    def step_offset(step):
        return pl.multiple_of(jnp.minimum(step, n_att - 1) * ATT_CK, ATT_CK)

    def qk_logits(step, buf):
        off = step_offset(step)
        for r in range(ATTN_KV_HEADS):
            buf[r] = jnp.dot(k_ref[0, r, pl.ds(off, ATT_CK), :], rhs[r], preferred_element_type=F32)

    def softmax_pv(step, buf, carry):
        off = step_offset(step)
        thr_step = jnp.where(step < n_att, thr, jnp.inf)
        bias = jnp.where(sc_ref[pl.ds(off, ATT_CK), :] >= thr_step, 0.0, MASKED)
        bias = jnp.concatenate([bias] * GROUPS, axis=1)
        out = []
        for r in range(ATTN_KV_HEADS):
            m, acc = carry[r]
            s = buf[r] + bias
            m_new = jnp.maximum(m, jnp.max(s, axis=0, keepdims=True))
            p = jnp.exp2(s - m_new).astype(BF16)
            pv = jnp.dot(vt_ref[0, r, :, pl.ds(off, ATT_CK)], p, preferred_element_type=F32)
            out.append((m_new, jnp.exp2(m - m_new) * acc + pv))
        return tuple(out)

    def att_body(i, carry):
        qk_logits(2 * i + 1, sb_ref)
        carry = softmax_pv(2 * i, sa_ref, carry)
        qk_logits(2 * i + 2, sa_ref)
        return softmax_pv(2 * i + 1, sb_ref, carry)

    init_a = tuple((jnp.full((1, GROUPS * qb), MASKED, F32), jnp.zeros((VT_ROWS, GROUPS * qb), F32))
                   for _ in range(ATTN_KV_HEADS))
    qk_logits(0, sa_ref)
    fin = lax.fori_loop(0, (n_att + 1) // 2, att_body, init_a)
    outs = []
    for r in range(ATTN_KV_HEADS):
        acc = fin[r][1]
        o = acc[0:HEAD_DIM] / acc[HEAD_DIM:HEAD_DIM + 1]
        outs.extend(o[:, g * qb:(g + 1) * qb] for g in range(GROUPS))
    o_ref[0] = jnp.concatenate(outs, axis=0).T.astype(BF16)


def _attn_call(qt, k, vt, qit, ki, wit):
    b, _, s = qt.shape
    n_sel = min(IDX_TOPK_MAX, s // 4)
    ck = min(512, s)
    grid = (b, s // QUERY_BLOCK)
    in_specs = [
        pl.BlockSpec((1, ATTN_WIDTH, QUERY_BLOCK), lambda i, j: (i, 0, j)),
        pl.BlockSpec((1, IDX_HEADS * IDX_DIM, QUERY_BLOCK), lambda i, j: (i, 0, j)),
        pl.BlockSpec((1, SUBLANES, QUERY_BLOCK), lambda i, j: (i, 0, j)),
        pl.BlockSpec((1, s, IDX_DIM), lambda i, j: (i, 0, 0)),
        pl.BlockSpec((1, ATTN_KV_HEADS, s, HEAD_DIM), lambda i, j: (i, 0, 0, 0)),
        pl.BlockSpec((1, ATTN_KV_HEADS, VT_ROWS, s), lambda i, j: (i, 0, 0, 0)),
        pl.BlockSpec((ck, ck), lambda i, j: (0, 0)),
    ]
    tri = jnp.asarray(np.tril(np.ones((ck, ck), np.float32), -1), BF16)
    return pl.pallas_call(
        functools.partial(_attn_kernel, n_sel=n_sel, ck=ck),
        grid=grid, in_specs=in_specs,
        out_specs=pl.BlockSpec((1, QUERY_BLOCK, ATTN_WIDTH), lambda i, j: (i, j, 0)),
        out_shape=jax.ShapeDtypeStruct((b, s, ATTN_WIDTH), BF16),
        scratch_shapes=[pltpu.VMEM((s, QUERY_BLOCK), F32),
                        pltpu.VMEM((ATTN_KV_HEADS, ATT_CK, GROUPS * QUERY_BLOCK), F32),
                        pltpu.VMEM((ATTN_KV_HEADS, ATT_CK, GROUPS * QUERY_BLOCK), F32)],
        compiler_params=_cparams("parallel", "parallel"), name="attn",
    )(qt, qit, wit, ki, k, vt, tri)


def _merge_kernel(x_ref, attn_ref, pool_ref, gate_ref, wba_ref, mix_ref, pscale_ref, wbp_ref, wout_ref,
                  o_ref, ext_ref):
    tm = x_ref.shape[1]
    j = pl.program_id(1)

    @pl.when(j == 0)
    def _():
        ext_ref[0:POOL_HALO] = jnp.zeros((POOL_HALO, POOL_WIDTH), F32)

    @pl.when(j > 0)
    def _():
        ext_ref[0:POOL_HALO] = ext_ref[tm:tm + POOL_HALO]

    ext_ref[POOL_HALO:POOL_HALO + tm] = pool_ref[0]
    t = j * tm + lax.broadcasted_iota(I32, (tm, 1), 0)
    mixed = []
    for g, w in enumerate(POOL_WINDOWS):
        cols = slice(g * POOL_GROUP_DIM, (g + 1) * POOL_GROUP_DIM)
        cur = ext_ref[POOL_HALO:POOL_HALO + tm, cols]
        wsum = cur
        for i in range(1, w):
            wsum = wsum + ext_ref[POOL_HALO - i:POOL_HALO - i + tm, cols]
        cnt = jnp.minimum(t + 1, w).astype(F32)
        dev = (wsum / cnt - cur).astype(BF16)
        mixed.append(jnp.dot(dev, mix_ref[g], preferred_element_type=F32))
    pooled = (jnp.concatenate(mixed, axis=1) * pscale_ref[...]).astype(BF16)
    branch_a = jnp.dot(attn_ref[0], wba_ref[...], preferred_element_type=F32)
    branch_p = jnp.dot(pooled, wbp_ref[...], preferred_element_type=F32)
    gates = gate_ref[0]
    merged = (jax.nn.sigmoid(gates[:, 0:D_MODEL]) * branch_a
              + jax.nn.sigmoid(gates[:, D_MODEL:2 * D_MODEL]) * branch_p)
    o_ref[0] = x_ref[0] + jnp.dot(merged.astype(BF16), wout_ref[...], preferred_element_type=F32)


def _merge_call(x, attn, pool, gates, wba, mix, pscale, wbp, wout):
    b, s, d = x.shape
    tm = min(256, s)
    grid = (b, s // tm)
    tile = lambda w: pl.BlockSpec((1, tm, w), lambda i, j: (i, j, 0))
    full2 = lambda i, j: (0, 0)
    in_specs = [
        tile(d), tile(ATTN_WIDTH), tile(POOL_WIDTH), tile(2 * D_MODEL),
        pl.BlockSpec((ATTN_WIDTH, d), full2),
        pl.BlockSpec((len(POOL_WINDOWS), POOL_GROUP_DIM, POOL_GROUP_DIM), lambda i, j: (0, 0, 0)),
        pl.BlockSpec((1, POOL_WIDTH), full2),
        pl.BlockSpec((POOL_WIDTH, d), full2),
        pl.BlockSpec((d, d), full2),
    ]
    return pl.pallas_call(
        _merge_kernel, grid=grid, in_specs=in_specs, out_specs=tile(d),
        out_shape=jax.ShapeDtypeStruct((b, s, d), F32),
        scratch_shapes=[pltpu.VMEM((POOL_HALO + tm, POOL_WIDTH), F32)],
        compiler_params=_cparams("parallel", "arbitrary"), name="merge",
    )(x, attn, pool, gates, wba, mix, pscale, wbp, wout)


ROW_TILES = D_MODEL // LANES


def _load_rows(ref, n, *lead):
    return jnp.concatenate(
        [ref[(*lead, pl.ds(c, n, stride=ROW_TILES), slice(None))] for c in range(ROW_TILES)], axis=1)


def _store_rows(ref, val):
    for c in range(ROW_TILES):
        ref[pl.ds(c, val.shape[0], stride=ROW_TILES), :] = val[:, c * LANES:(c + 1) * LANES]


def _row_tile(ref, i):
    return ref.at[pl.ds(pl.multiple_of(i * ROW_TILES, ROW_TILES), ROW_TILES)]


def _router_kernel(x_ref, g2_ref, rw_ref, rb_ref, tri_ref, h_ref, meta_ref, gate_ref, cnt_ref, carry_ref):
    i = pl.program_id(0)
    x = x_ref[...]
    tm = x.shape[0]

    @pl.when(i == 0)
    def _():
        carry_ref[...] = jnp.zeros((1, LANES), F32)

    ms = jnp.mean(x * x, axis=-1, keepdims=True)
    h = x * lax.rsqrt(ms + NORM_EPS) * g2_ref[...]
    _store_rows(h_ref, h)
    logits = jnp.dot(h.astype(BF16), rw_ref[...], preferred_element_type=F32) + rb_ref[...]
    lane = lax.broadcasted_iota(I32, (tm, LANES), 1).astype(F32)
    work = jnp.where(lane < N_EXPERTS, logits, -jnp.inf)
    vals, idxs = [], []
    for _ in range(TOP_K):
        m = jnp.max(work, axis=-1, keepdims=True)
        idx = jnp.min(jnp.where(work == m, lane, float(LANES)), axis=-1, keepdims=True)
        vals.append(m)
        idxs.append(idx)
        work = jnp.where(lane == idx, -jnp.inf, work)
    exps = [jnp.exp(v - vals[0]) for v in vals]
    denom = exps[0] + exps[1] + exps[2] + exps[3]
    member = jnp.zeros((tm, LANES), F32)
    for idx in idxs:
        member = member + jnp.where(lane == idx, 1.0, 0.0)
    before = jnp.dot(tri_ref[...], member.astype(BF16), preferred_element_type=F32) + carry_ref[...]
    meta = jnp.zeros((tm, LANES), F32)
    gate = jnp.zeros((tm, LANES), F32)
    for k in range(TOP_K):
        rank = jnp.sum(jnp.where(lane == idxs[k], before, 0.0), axis=-1, keepdims=True)
        meta = jnp.where(lane == float(k), idxs[k], meta)
        meta = jnp.where(lane == float(TOP_K + k), rank, meta)
        gate = jnp.where(lane == float(k), exps[k] / denom, gate)
    meta_ref[...] = meta.astype(I32)
    gate_ref[...] = gate
    total = carry_ref[...] + jnp.sum(member, axis=0, keepdims=True)
    carry_ref[...] = total
    cnt_ref[...] = total.astype(I32)


def _router_call(x1, g2, rw, rb, tri):
    t, d = x1.shape
    tm = tri.shape[0]
    full2 = lambda i: (0, 0)
    tile = lambda w: pl.BlockSpec((tm, w), lambda i: (i, 0))
    out_shape = (
        jax.ShapeDtypeStruct((t * ROW_TILES, LANES), F32),
        jax.ShapeDtypeStruct((t, LANES), I32),
        jax.ShapeDtypeStruct((t, LANES), F32),
        jax.ShapeDtypeStruct((1, LANES), I32),
    )
    return pl.pallas_call(
        _router_kernel, grid=(t // tm,),
        in_specs=[tile(d), pl.BlockSpec((1, d), full2), pl.BlockSpec((d, LANES), full2),
                  pl.BlockSpec((1, LANES), full2), pl.BlockSpec((tm, tm), full2)],
        out_specs=(pl.BlockSpec((tm * ROW_TILES, LANES), lambda i: (i, 0)), tile(LANES), tile(LANES),
                   pl.BlockSpec((1, LANES), full2)),
        out_shape=out_shape,
        scratch_shapes=[pltpu.VMEM((1, LANES), F32)],
        compiler_params=_cparams("arbitrary"), name="router",
    )(x1, g2, rw, rb, tri)


ROW_UNROLL = SUBLANES


def _dispatch_kernel(slot_ref, h_ref, xs_ref, sem, *, tm):
    def issue(i, carry):
        for u in range(ROW_UNROLL):
            j = i * ROW_UNROLL + u
            for k in range(TOP_K):
                pltpu.make_async_copy(_row_tile(h_ref, j), _row_tile(xs_ref, slot_ref[j * TOP_K + k]),
                                      sem).start(priority=k % 2)
        return carry

    lax.fori_loop(0, tm // ROW_UNROLL, issue, 0)

    def drain(j, carry):
        for k in range(TOP_K):
            pltpu.make_async_copy(_row_tile(h_ref, 0), _row_tile(xs_ref, 0), sem).wait()
        return carry

    lax.fori_loop(0, tm, drain, 0)


def _dispatch_call(slots, h2):
    t = h2.shape[0] // ROW_TILES
    tm = min(256, t)
    return pl.pallas_call(
        functools.partial(_dispatch_kernel, tm=tm),
        grid=(t // tm,),
        in_specs=[pl.BlockSpec((tm * TOP_K,), lambda i: (i,), memory_space=pltpu.SMEM),
                  pl.BlockSpec((tm * ROW_TILES, LANES), lambda i: (i, 0))],
        out_specs=pl.BlockSpec(memory_space=pl.ANY),
        scratch_shapes=[pltpu.SemaphoreType.DMA(())],
        out_shape=jax.ShapeDtypeStruct((t * TOP_K * ROW_TILES, LANES), F32),
        compiler_params=_cparams("arbitrary"), name="dispatch",
    )(slots, h2)


def _ffn_kernel(item_e_ref, item_blk_ref, nact_ref, gstart_ref, gend_ref,
                xs_ref, wgu_ref, bgu_ref, wd_ref, bd_ref, o_ref, wgu_bf, wd_bf):
    w = pl.program_id(0)
    e = item_e_ref[w]
    blk = item_blk_ref[w]
    prev = jnp.maximum(w - 1, 0)
    active = w < nact_ref[0]
    new_expert = jnp.logical_or(w == 0, e != item_e_ref[prev])
    first_visit = jnp.logical_or(w == 0, blk != item_blk_ref[prev])

    @pl.when(jnp.logical_and(active, new_expert))
    def _():
        wgu_bf[...] = wgu_ref[0].astype(BF16)
        wd_bf[...] = wd_ref[0].astype(BF16)

    @pl.when(active)
    def _():
        xb = _load_rows(xs_ref, ROW_BLOCK).astype(BF16)
        gu = jnp.dot(xb, wgu_bf[...], preferred_element_type=F32) + bgu_ref[0]
        gate = jnp.minimum(gu[:, 0:EXPERT_DIM], SWIGLU_LIMIT)
        up = jnp.clip(gu[:, EXPERT_DIM:2 * EXPERT_DIM], -SWIGLU_LIMIT, SWIGLU_LIMIT)
        act = gate * jax.nn.sigmoid(SWIGLU_ALPHA * gate) * (up + 1.0)
        res = jnp.dot(act.astype(BF16), wd_bf[...], preferred_element_type=F32) + bd_ref[0]
        row = blk * ROW_BLOCK + lax.broadcasted_iota(I32, (ROW_BLOCK, 1), 0)
        mine = jnp.logical_and(row >= gstart_ref[e], row < gend_ref[e])

        @pl.when(first_visit)
        def _():
            _store_rows(o_ref, jnp.where(mine, res, 0.0))

        @pl.when(jnp.logical_not(first_visit))
        def _():
            _store_rows(o_ref, jnp.where(mine, res, _load_rows(o_ref, ROW_BLOCK)))


def _ffn_call(item_e, item_blk, nact, gstart, gend, xs, wgu, bgu, wd, bd):
    d = D_MODEL
    rows = lambda w, ie, ib, *_: (ib[w], 0)
    exp3 = lambda w, ie, *_: (ie[w], 0, 0)
    return pl.pallas_call(
        _ffn_kernel,
        grid_spec=pltpu.PrefetchScalarGridSpec(
            num_scalar_prefetch=5, grid=(item_e.shape[0],),
            in_specs=[pl.BlockSpec((ROW_BLOCK * ROW_TILES, LANES), rows),
                      pl.BlockSpec((1, d, 2 * EXPERT_DIM), exp3),
                      pl.BlockSpec((1, 1, 2 * EXPERT_DIM), exp3),
                      pl.BlockSpec((1, EXPERT_DIM, d), exp3),
                      pl.BlockSpec((1, 1, d), exp3)],
            out_specs=pl.BlockSpec((ROW_BLOCK * ROW_TILES, LANES), rows),
            scratch_shapes=[pltpu.VMEM((d, 2 * EXPERT_DIM), BF16), pltpu.VMEM((EXPERT_DIM, d), BF16)]),
        out_shape=jax.ShapeDtypeStruct(xs.shape, F32),
        compiler_params=_cparams("arbitrary"), name="ffn",
    )(item_e, item_blk, nact, gstart, gend, xs, wgu, bgu, wd, bd)


def _combine_kernel(slot_ref, slot_next_ref, ys_ref, gate_ref, x_ref, o_ref, buf_ref, sem, *, tm, n_steps):
    i = pl.program_id(0)
    half = i % 2

    def gather(slots, dst_half):
        def issue(it, carry):
            for u in range(ROW_UNROLL):
                j = it * ROW_UNROLL + u
                for k in range(TOP_K):
                    pltpu.make_async_copy(_row_tile(ys_ref, slots[j * TOP_K + k]),
                                          _row_tile(buf_ref.at[dst_half, k], j),
                                          sem.at[dst_half]).start(priority=k % 2)
            return carry
        lax.fori_loop(0, tm // ROW_UNROLL, issue, 0)

    @pl.when(i == 0)
    def _():
        gather(slot_ref, 0)

    @pl.when(i + 1 < n_steps)
    def _():
        gather(slot_next_ref, 1 - half)

    def drain(j, carry):
        for k in range(TOP_K):
            pltpu.make_async_copy(_row_tile(ys_ref, 0), _row_tile(buf_ref.at[half, k], 0),
                                  sem.at[half]).wait()
        return carry

    lax.fori_loop(0, tm, drain, 0)
    gate = gate_ref[...]
    y = x_ref[...]
    for k in range(TOP_K):
        y = y + gate[:, k:k + 1] * _load_rows(buf_ref, tm, half, k)
    o_ref[...] = y


def _combine_call(slots, ys, gates, x1):
    t, d = x1.shape
    tm = min(256, t)
    n_steps = t // tm
    tile = lambda w: pl.BlockSpec((tm, w), lambda i: (i, 0))
    return pl.pallas_call(
        functools.partial(_combine_kernel, tm=tm, n_steps=n_steps),
        grid=(n_steps,),
        in_specs=[pl.BlockSpec((tm * TOP_K,), lambda i: (i,), memory_space=pltpu.SMEM),
                  pl.BlockSpec((tm * TOP_K,), lambda i: (jnp.minimum(i + 1, n_steps - 1),),
                               memory_space=pltpu.SMEM),
                  pl.BlockSpec(memory_space=pl.ANY), tile(LANES), tile(d)],
        out_specs=tile(d),
        scratch_shapes=[pltpu.VMEM((2, TOP_K, tm * ROW_TILES, LANES), F32), pltpu.SemaphoreType.DMA((2,))],
        out_shape=jax.ShapeDtypeStruct((t, d), F32),
        compiler_params=_cparams("arbitrary"), name="combine",
    )(slots, slots, ys, gates, x1)


def _pack_w_in(w):
    d = w.shape[0]
    o = np.cumsum([0, ATTN_WIDTH, KV_WIDTH, KV_WIDTH, IDX_HEADS * IDX_DIM, IDX_DIM, IDX_HEADS,
                   POOL_WIDTH, D_MODEL, D_MODEL])
    pad = jnp.zeros((d, LANES - IDX_DIM - IDX_HEADS), w.dtype)
    parts = [w[:, o[0]:o[4]], w[:, o[4]:o[6]], pad, w[:, o[6]:o[9]]]
    return jnp.concatenate(parts, axis=1).astype(BF16)


def _const_rows(q_g, k_g, i_g, i_b):
    lane = np.arange(LANES)
    inv_m = ROPE_THETA ** (-jnp.arange(0, HEAD_DIM, 2, dtype=F32) / HEAD_DIM)
    inv_i = ROPE_THETA ** (-jnp.arange(0, IDX_ROPE_DIM, 2, dtype=F32) / IDX_ROPE_DIM)
    in_rope = (lane % IDX_DIM) < IDX_ROPE_DIM
    zeros = jnp.zeros((LANES - IDX_DIM,), F32)
    rows = [
        inv_m[lane % (HEAD_DIM // 2)],
        jnp.asarray(np.where(lane % HEAD_DIM < HEAD_DIM // 2, -1.0, 1.0), F32),
        jnp.where(jnp.asarray(in_rope), inv_i[lane % (IDX_ROPE_DIM // 2)], 0.0),
        jnp.asarray(np.where(lane % IDX_ROPE_DIM < IDX_ROPE_DIM // 2, -1.0, 1.0), F32),
        jnp.tile(q_g.astype(F32), LANES // HEAD_DIM),
        jnp.tile(k_g.astype(F32), LANES // HEAD_DIM),
        jnp.concatenate([i_g.astype(F32), zeros]),
        jnp.concatenate([i_b.astype(F32), zeros]),
    ]
    return jnp.stack(rows, axis=0)


def _ffn_schedule(counts, n_blocks):
    gend = jnp.cumsum(counts).astype(I32)
    gstart = gend - counts
    first_blk = gstart // ROW_BLOCK
    last_blk = (jnp.maximum(gend, 1) - 1) // ROW_BLOCK
    n_items = jnp.where(counts > 0, last_blk - first_blk + 1, 0)
    item_end = jnp.cumsum(n_items).astype(I32)
    item_start = item_end - n_items
    total = item_end[-1]
    w = jnp.minimum(jnp.arange(n_blocks + N_EXPERTS - 1, dtype=I32), total - 1)
    item_e = jnp.sum((item_end[None, :] <= w[:, None]).astype(I32), axis=1)
    item_blk = first_blk[item_e] + w - item_start[item_e]
    return gstart, gend, item_e, item_blk, total.reshape(1)


def _layer(x, positions, norm1_g, w_in, q_norm_g, k_norm_g, idx_k_norm_g, idx_k_norm_b, w_branch_attn,
           pool_mix_w, pool_scale, w_branch_pool, w_out, norm2_g, router_w, router_b, w_gate_up,
           b_gate_up, w_down, b_down):
    b, s, d = x.shape
    t = b * s
    assert (t * TOP_K) % ROW_BLOCK == 0 and s % QUERY_BLOCK == 0
    lane = np.arange(LANES)
    gsum = jnp.asarray(lane[:, None] // HEAD_DIM == lane[None, :] // HEAD_DIM, BF16)

    qt, k, vt, qit, ki, wit, pool, gates = _proj_call(
        x, positions.reshape(b, s, 1), norm1_g.reshape(1, d), _pack_w_in(w_in),
        _const_rows(q_norm_g, k_norm_g, idx_k_norm_g, idx_k_norm_b), gsum)
    attn = _attn_call(qt, k, vt, qit, ki, wit)
    x1 = _merge_call(x, attn, pool, gates, w_branch_attn.astype(BF16), pool_mix_w.astype(BF16),
                     pool_scale.reshape(1, POOL_WIDTH), w_branch_pool.astype(BF16), w_out.astype(BF16))
    x1 = x1.reshape(t, d)

    tm_r = min(512, t)
    tri = jnp.asarray(np.tril(np.ones((tm_r, tm_r), np.float32), -1), BF16)
    rw = jnp.pad(router_w, ((0, 0), (0, LANES - N_EXPERTS))).astype(BF16)
    rb = jnp.pad(router_b, (0, LANES - N_EXPERTS)).reshape(1, LANES)
    h2, meta, rgate, counts = _router_call(x1, norm2_g.reshape(1, d), rw, rb, tri)

    gstart, gend, item_e, item_blk, nact = _ffn_schedule(counts[0, :N_EXPERTS], t * TOP_K // ROW_BLOCK)
    slots = (gstart[meta[:, 0:TOP_K]] + meta[:, TOP_K:2 * TOP_K]).reshape(t * TOP_K)
    xs = _dispatch_call(slots, h2)
    ys = _ffn_call(item_e, item_blk, nact, gstart, gend, xs, w_gate_up,
                   b_gate_up.reshape(N_EXPERTS, 1, 2 * EXPERT_DIM), w_down, b_down.reshape(N_EXPERTS, 1, d))
    out = _combine_call(slots, ys, rgate, x1)
    return out.reshape(b, s, d)


def kernel(x, positions, norm1_g, w_in, q_norm_g, k_norm_g, idx_k_norm_g, idx_k_norm_b, w_branch_attn,
           pool_mix_w, pool_scale, w_branch_pool, w_out, norm2_g, router_w, router_b, w_gate_up,
           b_gate_up, w_down, b_down):
    for l in range(norm1_g.shape[0]):
        x = _layer(x, positions, norm1_g[l], w_in[l], q_norm_g[l], k_norm_g[l], idx_k_norm_g[l],
                   idx_k_norm_b[l], w_branch_attn[l], pool_mix_w[l], pool_scale[l], w_branch_pool[l],
                   w_out[l], norm2_g[l], router_w[l], router_b[l], w_gate_up[l], b_gate_up[l],
                   w_down[l], b_down[l])
    return x
```

```python
import functools

import numpy as np
import jax
import jax.numpy as jnp
from jax import lax
from jax.experimental import pallas as pl
from jax.experimental.pallas import tpu as pltpu

F32 = jnp.float32
BF16 = jnp.bfloat16
I32 = jnp.int32

D_MODEL = 1024
CHUNK = 64
CHUNK_SHIFT = CHUNK.bit_length() - 1
ATTN_HEADS = 8
ATTN_KV_HEADS = 2
HEAD_DIM = 64
GROUPS = ATTN_HEADS // ATTN_KV_HEADS
ATTN_WIDTH = ATTN_HEADS * HEAD_DIM
KV_WIDTH = ATTN_KV_HEADS * HEAD_DIM
ATTN_SCALE = HEAD_DIM ** -0.5
IDX_HEADS = 4
IDX_DIM = 64
IDX_ROPE_DIM = 32
IDX_SCALE = (IDX_HEADS ** -0.5) * (IDX_DIM ** -0.5)
IDX_TOPK_MAX = 256
QUERY_BLOCK = 256
POOL_WINDOWS = (2, 4, 8, 16)
POOL_WIDTH = 512
POOL_GROUP_DIM = 128
POOL_HALO = 16
N_EXPERTS = 32
TOP_K = 4
EXPERT_DIM = 1024
SWIGLU_ALPHA = 1.702
SWIGLU_LIMIT = 7.0
ROPE_THETA = 10000.0
NORM_EPS = 1e-6

LANES = 128
SUBLANES = 8
VMEM_LIMIT = 56 * 1024 * 1024
FLT_MAX = float(np.finfo(np.float32).max)
MASKED = -1e30

C_Q = 0
C_K = 512
C_V = 640
C_QI = 768
C_KIW = 1024
C_POOL = 1152
C_GATE = 1664
W_PACKED = 3712
C_SMALL_END = C_POOL

ROW_BLOCK = 512
FFN_SPLIT = 1
FFN_COLS = EXPERT_DIM // FFN_SPLIT
FOLD_ROWS = 4 * SUBLANES
VALUE_BISECT_ITERS = 16
BISECT_CAP = 64
BISECT_HEAD = 14
BISECT_UNROLL = 2
ATT_CK = 256
VT_ROWS = HEAD_DIM + 2 * SUBLANES
LOG2E = float(np.log2(np.e))


def _cparams(*sem):
    return pltpu.CompilerParams(dimension_semantics=sem, vmem_limit_bytes=VMEM_LIMIT)


def _swap_halves(xc, first, half):
    return jnp.where(first, pltpu.roll(xc, LANES - half, 1), pltpu.roll(xc, half, 1))


def _proj_kernel(x_ref, pos_ref, g1_ref, w_ref, rows_ref, gsum_ref,
                 qt_ref, k_ref, vt_ref, qit_ref, ki_ref, wit_ref, pool_ref, gate_ref):
    x = x_ref[0]
    tm = x.shape[0]
    ms = jnp.mean(x * x, axis=-1, keepdims=True)
    h = (x * lax.rsqrt(ms + NORM_EPS) * g1_ref[...]).astype(BF16)
    d1 = jnp.dot(h, w_ref[:, 0:C_SMALL_END], preferred_element_type=F32)
    pool_ref[0] = jnp.dot(h, w_ref[:, C_POOL:C_GATE], preferred_element_type=F32)
    gate_ref[0] = jnp.dot(h, w_ref[:, C_GATE:W_PACKED], preferred_element_type=F32)

    pos = pos_ref[0].astype(F32)
    rows = rows_ref[...]
    lane = lax.broadcasted_iota(I32, (tm, LANES), 1)
    first_m = (lane & (HEAD_DIM - 1)) < HEAD_DIM // 2
    first_i = (lane & (IDX_ROPE_DIM - 1)) < IDX_ROPE_DIM // 2
    ang_m = pos * rows[0:1]
    cos_m = jnp.cos(ang_m)
    sin_m = jnp.sin(ang_m) * rows[1:2]
    ang_i = pos * rows[2:3]
    cos_i = jnp.cos(ang_i)
    sin_i = jnp.sin(ang_i) * rows[3:4]
    gsum = gsum_ref[...]

    def head_rms(xc, grow):
        sq = xc * xc
        hi = sq.astype(BF16)
        lo = (sq - hi.astype(F32)).astype(BF16)
        ssum = (jnp.dot(hi, gsum, preferred_element_type=F32)
                + jnp.dot(lo, gsum, preferred_element_type=F32))
        return xc * lax.rsqrt(ssum * (1.0 / HEAD_DIM) + NORM_EPS) * grow

    def rope_m(xc):
        return xc * cos_m + _swap_halves(xc, first_m, HEAD_DIM // 2) * sin_m

    def rope_i(xc):
        return xc * cos_i + _swap_halves(xc, first_i, IDX_ROPE_DIM // 2) * sin_i

    for c in range(ATTN_WIDTH // LANES):
        qc = d1[:, C_Q + c * LANES:C_Q + (c + 1) * LANES]
        qr = rope_m(head_rms(qc, rows[4:5])) * (ATTN_SCALE * LOG2E)
        qt_ref[0, c * LANES:(c + 1) * LANES, :] = qr.T.astype(BF16)

    kr = rope_m(head_rms(d1[:, C_K:C_K + KV_WIDTH], rows[5:6]))
    vt = d1[:, C_V:C_V + KV_WIDTH].T
    for r in range(ATTN_KV_HEADS):
        k_ref[0, r] = kr[:, r * HEAD_DIM:(r + 1) * HEAD_DIM].astype(BF16)
        vt_ref[0, r, 0:HEAD_DIM, :] = vt[r * HEAD_DIM:(r + 1) * HEAD_DIM].astype(BF16)
        vt_ref[0, r, HEAD_DIM:VT_ROWS, :] = jnp.ones((VT_ROWS - HEAD_DIM, tm), BF16)

    for c in range(IDX_HEADS * IDX_DIM // LANES):
        qc = d1[:, C_QI + c * LANES:C_QI + (c + 1) * LANES]
        qit_ref[0, c * LANES:(c + 1) * LANES, :] = rope_i(qc).T.astype(BF16)

    kiw = d1[:, C_KIW:C_KIW + LANES]
    in_ki = lane < IDX_DIM
    mu = jnp.sum(jnp.where(in_ki, kiw, 0.0), axis=-1, keepdims=True) * (1.0 / IDX_DIM)
    dv = jnp.where(in_ki, kiw - mu, 0.0)
    var = jnp.sum(dv * dv, axis=-1, keepdims=True) * (1.0 / IDX_DIM)
    kin = dv * lax.rsqrt(var + NORM_EPS) * rows[6:7] + rows[7:8]
    ki_ref[0] = rope_i(kin)[:, 0:IDX_DIM].astype(BF16)
    wit_ref[0] = (kiw * IDX_SCALE).T[IDX_DIM:IDX_DIM + SUBLANES]


def _proj_call(x, pos3, g1, w_packed, rows, gsum):
    b, s, d = x.shape
    tm = min(512, s)
    grid = (b, s // tm)
    full2 = lambda i, j: (0, 0)
    out_shape = (
        jax.ShapeDtypeStruct((b, ATTN_WIDTH, s), BF16),
        jax.ShapeDtypeStruct((b, ATTN_KV_HEADS, s, HEAD_DIM), BF16),
        jax.ShapeDtypeStruct((b, ATTN_KV_HEADS, VT_ROWS, s), BF16),
        jax.ShapeDtypeStruct((b, IDX_HEADS * IDX_DIM, s), BF16),
        jax.ShapeDtypeStruct((b, s, IDX_DIM), BF16),
        jax.ShapeDtypeStruct((b, SUBLANES, s), F32),
        jax.ShapeDtypeStruct((b, s, POOL_WIDTH), F32),
        jax.ShapeDtypeStruct((b, s, 2 * D_MODEL), F32),
    )
    in_specs = [
        pl.BlockSpec((1, tm, d), lambda i, j: (i, j, 0)),
        pl.BlockSpec((1, tm, 1), lambda i, j: (i, j, 0)),
        pl.BlockSpec((1, d), full2),
        pl.BlockSpec((d, W_PACKED), full2),
        pl.BlockSpec((8, LANES), full2),
        pl.BlockSpec((LANES, LANES), full2),
    ]
    out_specs = (
        pl.BlockSpec((1, ATTN_WIDTH, tm), lambda i, j: (i, 0, j)),
        pl.BlockSpec((1, ATTN_KV_HEADS, tm, HEAD_DIM), lambda i, j: (i, 0, j, 0)),
        pl.BlockSpec((1, ATTN_KV_HEADS, VT_ROWS, tm), lambda i, j: (i, 0, 0, j)),
        pl.BlockSpec((1, IDX_HEADS * IDX_DIM, tm), lambda i, j: (i, 0, j)),
        pl.BlockSpec((1, tm, IDX_DIM), lambda i, j: (i, j, 0)),
        pl.BlockSpec((1, SUBLANES, tm), lambda i, j: (i, 0, j)),
        pl.BlockSpec((1, tm, POOL_WIDTH), lambda i, j: (i, j, 0)),
        pl.BlockSpec((1, tm, 2 * D_MODEL), lambda i, j: (i, j, 0)),
    )
    return pl.pallas_call(
        _proj_kernel, grid=grid, in_specs=in_specs, out_specs=out_specs, out_shape=out_shape,
        compiler_params=_cparams("parallel", "parallel"), name="proj",
    )(x, pos3, g1, w_packed, rows, gsum)


def _sort_key(x):
    b = lax.bitcast_convert_type(x, I32)
    return b ^ ((b >> 31) & 0x7FFFFFFF)


def _unsort_key(k):
    return lax.bitcast_convert_type(k ^ ((k >> 31) & 0x7FFFFFFF), F32)


def _attn_kernel(qt_ref, qit_ref, wit_ref, ki_ref, k_ref, vt_ref, tri_ref, o_ref, sc_ref, sa_ref, sb_ref,
                 *, n_sel, ck):
    qb = QUERY_BLOCK
    t0 = pl.program_id(1) * qb
    nck = (t0 + qb + ck - 1) // ck
    qpos = t0 + lax.broadcasted_iota(I32, (1, qb), 1)
    cend = ((qpos >> CHUNK_SHIFT) + 1) * CHUNK
    key_ck = lax.broadcasted_iota(I32, (ck, qb), 0)
    k_sel = float(n_sel)
    cnt_rows = min(ck, 64 * SUBLANES * LANES // qb)

    def fold(x, op):
        return op(x.reshape(ck // FOLD_ROWS, FOLD_ROWS, qb), axis=0)

    qit = qit_ref[0]
    rhs_i = jnp.concatenate([qit[h * IDX_DIM:(h + 1) * IDX_DIM] for h in range(IDX_HEADS)], axis=1)
    wit = wit_ref[0]

    def score_body(c, carry):
        mxp, mnp, c0p, c1p = carry
        off = pl.multiple_of(c * ck, ck)
        lg = jnp.dot(ki_ref[0, pl.ds(off, ck), :], rhs_i, preferred_element_type=F32)
        sc = jnp.maximum(lg[:, 0:qb], 0.0) * wit[0:1]
        for h in range(1, IDX_HEADS):
            sc = sc + jnp.maximum(lg[:, h * qb:(h + 1) * qb], 0.0) * wit[h:h + 1]
        adm = (off + key_ck) < cend
        sc = jnp.where(adm, sc, -jnp.inf)
        sc_ref[pl.ds(off, ck), :] = sc
        mxp = jnp.maximum(mxp, fold(sc, jnp.max))
        mnp = jnp.minimum(mnp, fold(jnp.where(adm, sc, jnp.inf), jnp.min))
        c0p = c0p + fold(jnp.where(sc >= 0.0, 1.0, 0.0), jnp.sum)
        c1p = c1p + fold(jnp.where(sc > 0.0, 1.0, 0.0), jnp.sum)
        return mxp, mnp, c0p, c1p

    init = (jnp.full((FOLD_ROWS, qb), -jnp.inf, F32), jnp.full((FOLD_ROWS, qb), jnp.inf, F32),
            jnp.zeros((FOLD_ROWS, qb), F32), jnp.zeros((FOLD_ROWS, qb), F32))
    mxp, mnp, c0p, c1p = lax.fori_loop(0, nck, score_body, init)
    mx = jnp.max(mxp, axis=0, keepdims=True)
    mn = jnp.min(mnp, axis=0, keepdims=True)
    c0 = jnp.sum(c0p, axis=0, keepdims=True)
    c1 = jnp.sum(c1p, axis=0, keepdims=True)

    def count(pred):
        def body(c, acc):
            for sb in range(ck // cnt_rows):
                off = pl.multiple_of(c * ck + sb * cnt_rows, cnt_rows)
                hit = pred(sc_ref[pl.ds(off, cnt_rows), :], off + key_ck[0:cnt_rows])
                acc = acc + jnp.sum(jnp.where(hit, 1.0, 0.0).reshape(cnt_rows // FOLD_ROWS, FOLD_ROWS, qb),
                                    axis=0)
            return acc
        acc = lax.fori_loop(0, nck, body, jnp.zeros((FOLD_ROWS, qb), F32))
        return jnp.sum(acc, axis=0, keepdims=True)

    small = cend.astype(F32) <= k_sel
    at_zero = jnp.logical_and(jnp.logical_not(small), jnp.logical_and(c1 < k_sel, c0 >= k_sel))
    positive = jnp.logical_and(jnp.logical_not(small), c1 >= k_sel)
    lo0 = jnp.where(positive, 0.0, mn)
    hi0 = jnp.where(positive, jnp.minimum(2.0 * mx, FLT_MAX), 0.0)
    thr0 = jnp.where(small, -FLT_MAX, 0.0)
    done0 = jnp.where(jnp.logical_or(small, at_zero), 1.0, 0.0)
    tie0 = jnp.where(jnp.logical_and(at_zero, c0 > k_sel), 1.0, 0.0)
    chi0 = jnp.where(positive, 0.0, c0)
    left0 = jnp.sum(1.0 - done0)

    def bisect_cond(st):
        it, left = st[0], st[1]
        return jnp.logical_and(it < BISECT_CAP, left > 0.0)

    def bisect_step(it, lo, hi, thr, done, tie, chi):
        mid_v = 0.5 * lo + 0.5 * hi
        klo = _sort_key(lo)
        khi = _sort_key(hi)
        mid_k = _unsort_key((klo & khi) + ((klo ^ khi) >> 1))
        mid = jnp.where(it < VALUE_BISECT_ITERS, mid_v, mid_k)
        stuck = jnp.logical_or(mid <= lo, mid >= hi)
        cnt = count(lambda s, _: s >= mid)
        active = done == 0.0
        moving = jnp.logical_and(active, jnp.logical_not(stuck))
        hit = jnp.logical_and(moving, cnt == k_sel)
        new_tie = jnp.logical_and(active, stuck)
        thr = jnp.where(hit, mid, jnp.where(new_tie, lo, thr))
        tie = jnp.where(new_tie, 1.0, tie)
        done = jnp.where(jnp.logical_or(hit, new_tie), 1.0, done)
        upd = jnp.logical_and(moving, jnp.logical_not(hit))
        lo = jnp.where(jnp.logical_and(upd, cnt >= k_sel), mid, lo)
        lower = jnp.logical_and(upd, cnt < k_sel)
        hi = jnp.where(lower, mid, hi)
        chi = jnp.where(lower, cnt, chi)
        return lo, hi, thr, done, tie, chi

    def bisect_body(steps, st):
        it, _, lo, hi, thr, done, tie, chi = st
        for u in range(steps):
            lo, hi, thr, done, tie, chi = bisect_step(it + u, lo, hi, thr, done, tie, chi)
        return it + steps, jnp.sum(1.0 - done), lo, hi, thr, done, tie, chi

    st = (jnp.int32(0), left0, lo0, hi0, thr0, done0, tie0, chi0)
    st = lax.while_loop(lambda s: jnp.logical_and(s[0] < BISECT_HEAD, s[1] > 0.0),
                        functools.partial(bisect_body, BISECT_HEAD), st)
    st = lax.while_loop(bisect_cond, functools.partial(bisect_body, BISECT_UNROLL), st)
    thr, tie = st[4], st[6]
    n_above = jnp.where(at_zero, c1, st[7])

    @pl.when(jnp.sum(tie) > 0.0)
    def _():
        need = jnp.where(tie > 0.0, k_sel - n_above, FLT_MAX)

        def drop_body(c, before):
            off = pl.multiple_of(c * ck, ck)
            blk = sc_ref[pl.ds(off, ck), :]
            eq = blk == thr
            eqf = jnp.where(eq, 1.0, 0.0)
            rank = before + jnp.dot(tri_ref[...], eqf.astype(BF16), preferred_element_type=F32)
            sc_ref[pl.ds(off, ck), :] = jnp.where(jnp.logical_and(eq, rank >= need), -jnp.inf, blk)
            return before + jnp.sum(fold(eqf, jnp.sum), axis=0, keepdims=True)

        lax.fori_loop(0, nck, drop_body, jnp.zeros((1, qb), F32))

    qt = qt_ref[0]
    rhs = [jnp.concatenate(
        [qt[(r * GROUPS + g) * HEAD_DIM:(r * GROUPS + g + 1) * HEAD_DIM] for g in range(GROUPS)],
        axis=1) for r in range(ATTN_KV_HEADS)]

    n_att = (t0 + qb + ATT_CK - 1) // ATT_CK

    def step_offset(step):
        return pl.multiple_of(jnp.minimum(step, n_att - 1) * ATT_CK, ATT_CK)

    def qk_logits(step, buf):
        off = step_offset(step)
        for r in range(ATTN_KV_HEADS):
            buf[r] = jnp.dot(k_ref[0, r, pl.ds(off, ATT_CK), :], rhs[r], preferred_element_type=F32)

    def softmax_pv(step, buf, carry):
        off = step_offset(step)
        thr_step = jnp.where(step < n_att, thr, jnp.inf)
        bias = jnp.where(sc_ref[pl.ds(off, ATT_CK), :] >= thr_step, 0.0, MASKED)
        bias = jnp.concatenate([bias] * GROUPS, axis=1)
        out = []
        for r in range(ATTN_KV_HEADS):
            m, acc = carry[r]
            s = buf[r] + bias
            m_new = jnp.maximum(m, jnp.max(s, axis=0, keepdims=True))
            p = jnp.exp2(s - m_new).astype(BF16)
            pv = jnp.dot(vt_ref[0, r, :, pl.ds(off, ATT_CK)], p, preferred_element_type=F32)
            out.append((m_new, jnp.exp2(m - m_new) * acc + pv))
        return tuple(out)

    def att_body(i, carry):
        qk_logits(2 * i + 1, sb_ref)
        carry = softmax_pv(2 * i, sa_ref, carry)
        qk_logits(2 * i + 2, sa_ref)
        return softmax_pv(2 * i + 1, sb_ref, carry)

    init_a = tuple((jnp.full((1, GROUPS * qb), MASKED, F32), jnp.zeros((VT_ROWS, GROUPS * qb), F32))
                   for _ in range(ATTN_KV_HEADS))
    qk_logits(0, sa_ref)
    fin = lax.fori_loop(0, (n_att + 1) // 2, att_body, init_a)
    outs = []
    for r in range(ATTN_KV_HEADS):
        acc = fin[r][1]
        o = acc[0:HEAD_DIM] / acc[HEAD_DIM:HEAD_DIM + 1]
        outs.extend(o[:, g * qb:(g + 1) * qb] for g in range(GROUPS))
    o_ref[0] = jnp.concatenate(outs, axis=0).T.astype(BF16)


def _attn_call(qt, k, vt, qit, ki, wit):
    b, _, s = qt.shape
    n_sel = min(IDX_TOPK_MAX, s // 4)
    ck = min(512, s)
    grid = (b, s // QUERY_BLOCK)
    in_specs = [
        pl.BlockSpec((1, ATTN_WIDTH, QUERY_BLOCK), lambda i, j: (i, 0, j)),
        pl.BlockSpec((1, IDX_HEADS * IDX_DIM, QUERY_BLOCK), lambda i, j: (i, 0, j)),
        pl.BlockSpec((1, SUBLANES, QUERY_BLOCK), lambda i, j: (i, 0, j)),
        pl.BlockSpec((1, s, IDX_DIM), lambda i, j: (i, 0, 0)),
        pl.BlockSpec((1, ATTN_KV_HEADS, s, HEAD_DIM), lambda i, j: (i, 0, 0, 0)),
        pl.BlockSpec((1, ATTN_KV_HEADS, VT_ROWS, s), lambda i, j: (i, 0, 0, 0)),
        pl.BlockSpec((ck, ck), lambda i, j: (0, 0)),
    ]
    tri = jnp.asarray(np.tril(np.ones((ck, ck), np.float32), -1), BF16)
    return pl.pallas_call(
        functools.partial(_attn_kernel, n_sel=n_sel, ck=ck),
        grid=grid, in_specs=in_specs,
        out_specs=pl.BlockSpec((1, QUERY_BLOCK, ATTN_WIDTH), lambda i, j: (i, j, 0)),
        out_shape=jax.ShapeDtypeStruct((b, s, ATTN_WIDTH), BF16),
        scratch_shapes=[pltpu.VMEM((s, QUERY_BLOCK), F32),
                        pltpu.VMEM((ATTN_KV_HEADS, ATT_CK, GROUPS * QUERY_BLOCK), F32),
                        pltpu.VMEM((ATTN_KV_HEADS, ATT_CK, GROUPS * QUERY_BLOCK), F32)],
        compiler_params=_cparams("parallel", "parallel"), name="attn",
    )(qt, qit, wit, ki, k, vt, tri)


def _merge_kernel(x_ref, attn_ref, pool_ref, gate_ref, wba_ref, mix_ref, pscale_ref, wbp_ref, wout_ref,
                  o_ref, ext_ref):
    tm = x_ref.shape[1]
    j = pl.program_id(1)

    @pl.when(j == 0)
    def _():
        ext_ref[0:POOL_HALO] = jnp.zeros((POOL_HALO, POOL_WIDTH), F32)

    @pl.when(j > 0)
    def _():
        ext_ref[0:POOL_HALO] = ext_ref[tm:tm + POOL_HALO]

    ext_ref[POOL_HALO:POOL_HALO + tm] = pool_ref[0]
    t = j * tm + lax.broadcasted_iota(I32, (tm, 1), 0)
    mixed = []
    for g, w in enumerate(POOL_WINDOWS):
        cols = slice(g * POOL_GROUP_DIM, (g + 1) * POOL_GROUP_DIM)
        cur = ext_ref[POOL_HALO:POOL_HALO + tm, cols]
        wsum = cur
        for i in range(1, w):
            wsum = wsum + ext_ref[POOL_HALO - i:POOL_HALO - i + tm, cols]
        cnt = jnp.minimum(t + 1, w).astype(F32)
        dev = (wsum / cnt - cur).astype(BF16)
        mixed.append(jnp.dot(dev, mix_ref[g], preferred_element_type=F32))
    pooled = (jnp.concatenate(mixed, axis=1) * pscale_ref[...]).astype(BF16)
    branch_a = jnp.dot(attn_ref[0], wba_ref[...], preferred_element_type=F32)
    branch_p = jnp.dot(pooled, wbp_ref[...], preferred_element_type=F32)
    gates = gate_ref[0]
    merged = (jax.nn.sigmoid(gates[:, 0:D_MODEL]) * branch_a
              + jax.nn.sigmoid(gates[:, D_MODEL:2 * D_MODEL]) * branch_p)
    o_ref[0] = x_ref[0] + jnp.dot(merged.astype(BF16), wout_ref[...], preferred_element_type=F32)


def _merge_call(x, attn, pool, gates, wba, mix, pscale, wbp, wout):
    b, s, d = x.shape
    tm = min(256, s)
    grid = (b, s // tm)
    tile = lambda w: pl.BlockSpec((1, tm, w), lambda i, j: (i, j, 0))
    full2 = lambda i, j: (0, 0)
    in_specs = [
        tile(d), tile(ATTN_WIDTH), tile(POOL_WIDTH), tile(2 * D_MODEL),
        pl.BlockSpec((ATTN_WIDTH, d), full2),
        pl.BlockSpec((len(POOL_WINDOWS), POOL_GROUP_DIM, POOL_GROUP_DIM), lambda i, j: (0, 0, 0)),
        pl.BlockSpec((1, POOL_WIDTH), full2),
        pl.BlockSpec((POOL_WIDTH, d), full2),
        pl.BlockSpec((d, d), full2),
    ]
    return pl.pallas_call(
        _merge_kernel, grid=grid, in_specs=in_specs, out_specs=tile(d),
        out_shape=jax.ShapeDtypeStruct((b, s, d), F32),
        scratch_shapes=[pltpu.VMEM((POOL_HALO + tm, POOL_WIDTH), F32)],
        compiler_params=_cparams("parallel", "arbitrary"), name="merge",
    )(x, attn, pool, gates, wba, mix, pscale, wbp, wout)


ROW_TILES = D_MODEL // LANES


def _load_rows(ref, n, *lead):
    return jnp.concatenate(
        [ref[(*lead, pl.ds(c, n, stride=ROW_TILES), slice(None))] for c in range(ROW_TILES)], axis=1)


def _store_rows(ref, val):
    for c in range(ROW_TILES):
        ref[pl.ds(c, val.shape[0], stride=ROW_TILES), :] = val[:, c * LANES:(c + 1) * LANES]


def _row_tile(ref, i):
    return ref.at[pl.ds(pl.multiple_of(i * ROW_TILES, ROW_TILES), ROW_TILES)]


def _router_kernel(x_ref, g2_ref, rw_ref, rb_ref, tri_ref, h_ref, meta_ref, gate_ref, cnt_ref, carry_ref):
    i = pl.program_id(0)
    x = x_ref[...]
    tm = x.shape[0]

    @pl.when(i == 0)
    def _():
        carry_ref[...] = jnp.zeros((1, LANES), F32)

    ms = jnp.mean(x * x, axis=-1, keepdims=True)
    h = x * lax.rsqrt(ms + NORM_EPS) * g2_ref[...]
    _store_rows(h_ref, h)
    logits = jnp.dot(h.astype(BF16), rw_ref[...], preferred_element_type=F32) + rb_ref[...]
    lane = lax.broadcasted_iota(I32, (tm, LANES), 1).astype(F32)
    work = jnp.where(lane < N_EXPERTS, logits, -jnp.inf)
    vals, idxs = [], []
    for _ in range(TOP_K):
        m = jnp.max(work, axis=-1, keepdims=True)
        idx = jnp.min(jnp.where(work == m, lane, float(LANES)), axis=-1, keepdims=True)
        vals.append(m)
        idxs.append(idx)
        work = jnp.where(lane == idx, -jnp.inf, work)
    exps = [jnp.exp(v - vals[0]) for v in vals]
    denom = exps[0] + exps[1] + exps[2] + exps[3]
    member = jnp.zeros((tm, LANES), F32)
    for idx in idxs:
        member = member + jnp.where(lane == idx, 1.0, 0.0)
    before = jnp.dot(tri_ref[...], member.astype(BF16), preferred_element_type=F32) + carry_ref[...]
    meta = jnp.zeros((tm, LANES), F32)
    gate = jnp.zeros((tm, LANES), F32)
    for k in range(TOP_K):
        rank = jnp.sum(jnp.where(lane == idxs[k], before, 0.0), axis=-1, keepdims=True)
        meta = jnp.where(lane == float(k), idxs[k], meta)
        meta = jnp.where(lane == float(TOP_K + k), rank, meta)
        gate = jnp.where(lane == float(k), exps[k] / denom, gate)
    meta_ref[...] = meta.astype(I32)
    gate_ref[...] = gate
    total = carry_ref[...] + jnp.sum(member, axis=0, keepdims=True)
    carry_ref[...] = total
    cnt_ref[...] = total.astype(I32)


def _router_call(x1, g2, rw, rb, tri):
    t, d = x1.shape
    tm = tri.shape[0]
    full2 = lambda i: (0, 0)
    tile = lambda w: pl.BlockSpec((tm, w), lambda i: (i, 0))
    out_shape = (
        jax.ShapeDtypeStruct((t * ROW_TILES, LANES), F32),
        jax.ShapeDtypeStruct((t, LANES), I32),
        jax.ShapeDtypeStruct((t, LANES), F32),
        jax.ShapeDtypeStruct((1, LANES), I32),
    )
    return pl.pallas_call(
        _router_kernel, grid=(t // tm,),
        in_specs=[tile(d), pl.BlockSpec((1, d), full2), pl.BlockSpec((d, LANES), full2),
                  pl.BlockSpec((1, LANES), full2), pl.BlockSpec((tm, tm), full2)],
        out_specs=(pl.BlockSpec((tm * ROW_TILES, LANES), lambda i: (i, 0)), tile(LANES), tile(LANES),
                   pl.BlockSpec((1, LANES), full2)),
        out_shape=out_shape,
        scratch_shapes=[pltpu.VMEM((1, LANES), F32)],
        compiler_params=_cparams("arbitrary"), name="router",
    )(x1, g2, rw, rb, tri)


ROW_UNROLL = SUBLANES


def _dispatch_kernel(slot_ref, h_ref, xs_ref, sem, *, tm):
    def issue(i, carry):
        for u in range(ROW_UNROLL):
            j = i * ROW_UNROLL + u
            for k in range(TOP_K):
                pltpu.make_async_copy(_row_tile(h_ref, j), _row_tile(xs_ref, slot_ref[j * TOP_K + k]),
                                      sem).start(priority=k % 2)
        return carry

    lax.fori_loop(0, tm // ROW_UNROLL, issue, 0)

    def drain(j, carry):
        for k in range(TOP_K):
            pltpu.make_async_copy(_row_tile(h_ref, 0), _row_tile(xs_ref, 0), sem).wait()
        return carry

    lax.fori_loop(0, tm, drain, 0)


def _dispatch_call(slots, h2):
    t = h2.shape[0] // ROW_TILES
    tm = min(256, t)
    return pl.pallas_call(
        functools.partial(_dispatch_kernel, tm=tm),
        grid=(t // tm,),
        in_specs=[pl.BlockSpec((tm * TOP_K,), lambda i: (i,), memory_space=pltpu.SMEM),
                  pl.BlockSpec((tm * ROW_TILES, LANES), lambda i: (i, 0))],
        out_specs=pl.BlockSpec(memory_space=pl.ANY),
        scratch_shapes=[pltpu.SemaphoreType.DMA(())],
        out_shape=jax.ShapeDtypeStruct((t * TOP_K * ROW_TILES, LANES), F32),
        compiler_params=_cparams("arbitrary"), name="dispatch",
    )(slots, h2)


def _ffn_kernel(item_e_ref, item_blk_ref, nact_ref, gstart_ref, gend_ref,
                xs_ref, wgu_ref, bgu_ref, wd_ref, bd_ref, o_ref, wgu_bf, wd_bf):
    w = pl.program_id(0)
    e = item_e_ref[w]
    blk = item_blk_ref[w]
    prev = jnp.maximum(w - 1, 0)
    active = w < nact_ref[0]
    new_expert = jnp.logical_or(w == 0, e != item_e_ref[prev])
    first_visit = jnp.logical_or(w == 0, blk != item_blk_ref[prev])

    @pl.when(jnp.logical_and(active, new_expert))
    def _():
        wgu_bf[...] = wgu_ref[0].astype(BF16)
        wd_bf[...] = wd_ref[0].astype(BF16)

    @pl.when(active)
    def _():
        xb = _load_rows(xs_ref, ROW_BLOCK).astype(BF16)
        res = bd_ref[0]
        for c in range(FFN_SPLIT):
            g0, g1 = c * FFN_COLS, (c + 1) * FFN_COLS
            gate = jnp.dot(xb, wgu_bf[:, g0:g1], preferred_element_type=F32) + bgu_ref[0, :, g0:g1]
            up = (jnp.dot(xb, wgu_bf[:, EXPERT_DIM + g0:EXPERT_DIM + g1], preferred_element_type=F32)
                  + bgu_ref[0, :, EXPERT_DIM + g0:EXPERT_DIM + g1])
            gate = jnp.minimum(gate, SWIGLU_LIMIT)
            up = jnp.clip(up, -SWIGLU_LIMIT, SWIGLU_LIMIT)
            act = gate * jax.nn.sigmoid(SWIGLU_ALPHA * gate) * (up + 1.0)
            res = res + jnp.dot(act.astype(BF16), wd_bf[g0:g1, :], preferred_element_type=F32)
        row = blk * ROW_BLOCK + lax.broadcasted_iota(I32, (ROW_BLOCK, 1), 0)
        mine = jnp.logical_and(row >= gstart_ref[e], row < gend_ref[e])

        @pl.when(first_visit)
        def _():
            _store_rows(o_ref, jnp.where(mine, res, 0.0))

        @pl.when(jnp.logical_not(first_visit))
        def _():
            _store_rows(o_ref, jnp.where(mine, res, _load_rows(o_ref, ROW_BLOCK)))


def _ffn_call(item_e, item_blk, nact, gstart, gend, xs, wgu, bgu, wd, bd):
    d = D_MODEL
    rows = lambda w, ie, ib, *_: (ib[w], 0)
    exp3 = lambda w, ie, *_: (ie[w], 0, 0)
    return pl.pallas_call(
        _ffn_kernel,
        grid_spec=pltpu.PrefetchScalarGridSpec(
            num_scalar_prefetch=5, grid=(item_e.shape[0],),
            in_specs=[pl.BlockSpec((ROW_BLOCK * ROW_TILES, LANES), rows),
                      pl.BlockSpec((1, d, 2 * EXPERT_DIM), exp3),
                      pl.BlockSpec((1, 1, 2 * EXPERT_DIM), exp3),
                      pl.BlockSpec((1, EXPERT_DIM, d), exp3),
                      pl.BlockSpec((1, 1, d), exp3)],
            out_specs=pl.BlockSpec((ROW_BLOCK * ROW_TILES, LANES), rows),
            scratch_shapes=[pltpu.VMEM((d, 2 * EXPERT_DIM), BF16), pltpu.VMEM((EXPERT_DIM, d), BF16)]),
        out_shape=jax.ShapeDtypeStruct(xs.shape, F32),
        compiler_params=_cparams("arbitrary"), name="ffn",
    )(item_e, item_blk, nact, gstart, gend, xs, wgu, bgu, wd, bd)


def _combine_kernel(slot_ref, slot_next_ref, ys_ref, gate_ref, x_ref, o_ref, buf_ref, sem, *, tm, n_steps):
    i = pl.program_id(0)
    half = i % 2

    def gather(slots, dst_half):
        def issue(it, carry):
            for u in range(ROW_UNROLL):
                j = it * ROW_UNROLL + u
                for k in range(TOP_K):
                    pltpu.make_async_copy(_row_tile(ys_ref, slots[j * TOP_K + k]),
                                          _row_tile(buf_ref.at[dst_half, k], j),
                                          sem.at[dst_half]).start(priority=k % 2)
            return carry
        lax.fori_loop(0, tm // ROW_UNROLL, issue, 0)

    @pl.when(i == 0)
    def _():
        gather(slot_ref, 0)

    @pl.when(i + 1 < n_steps)
    def _():
        gather(slot_next_ref, 1 - half)

    def drain(j, carry):
        for k in range(TOP_K):
            pltpu.make_async_copy(_row_tile(ys_ref, 0), _row_tile(buf_ref.at[half, k], 0),
                                  sem.at[half]).wait()
        return carry

    lax.fori_loop(0, tm, drain, 0)
    gate = gate_ref[...]
    y = x_ref[...]
    for k in range(TOP_K):
        y = y + gate[:, k:k + 1] * _load_rows(buf_ref, tm, half, k)
    o_ref[...] = y


def _combine_call(slots, ys, gates, x1):
    t, d = x1.shape
    tm = min(256, t)
    n_steps = t // tm
    tile = lambda w: pl.BlockSpec((tm, w), lambda i: (i, 0))
    return pl.pallas_call(
        functools.partial(_combine_kernel, tm=tm, n_steps=n_steps),
        grid=(n_steps,),
        in_specs=[pl.BlockSpec((tm * TOP_K,), lambda i: (i,), memory_space=pltpu.SMEM),
                  pl.BlockSpec((tm * TOP_K,), lambda i: (jnp.minimum(i + 1, n_steps - 1),),
                               memory_space=pltpu.SMEM),
                  pl.BlockSpec(memory_space=pl.ANY), tile(LANES), tile(d)],
        out_specs=tile(d),
        scratch_shapes=[pltpu.VMEM((2, TOP_K, tm * ROW_TILES, LANES), F32), pltpu.SemaphoreType.DMA((2,))],
        out_shape=jax.ShapeDtypeStruct((t, d), F32),
        compiler_params=_cparams("arbitrary"), name="combine",
    )(slots, slots, ys, gates, x1)


def _pack_w_in(w):
    d = w.shape[0]
    o = np.cumsum([0, ATTN_WIDTH, KV_WIDTH, KV_WIDTH, IDX_HEADS * IDX_DIM, IDX_DIM, IDX_HEADS,
                   POOL_WIDTH, D_MODEL, D_MODEL])
    pad = jnp.zeros((d, LANES - IDX_DIM - IDX_HEADS), w.dtype)
    parts = [w[:, o[0]:o[4]], w[:, o[4]:o[6]], pad, w[:, o[6]:o[9]]]
    return jnp.concatenate(parts, axis=1).astype(BF16)


def _const_rows(q_g, k_g, i_g, i_b):
    lane = np.arange(LANES)
    inv_m = ROPE_THETA ** (-jnp.arange(0, HEAD_DIM, 2, dtype=F32) / HEAD_DIM)
    inv_i = ROPE_THETA ** (-jnp.arange(0, IDX_ROPE_DIM, 2, dtype=F32) / IDX_ROPE_DIM)
    in_rope = (lane % IDX_DIM) < IDX_ROPE_DIM
    zeros = jnp.zeros((LANES - IDX_DIM,), F32)
    rows = [
        inv_m[lane % (HEAD_DIM // 2)],
        jnp.asarray(np.where(lane % HEAD_DIM < HEAD_DIM // 2, -1.0, 1.0), F32),
        jnp.where(jnp.asarray(in_rope), inv_i[lane % (IDX_ROPE_DIM // 2)], 0.0),
        jnp.asarray(np.where(lane % IDX_ROPE_DIM < IDX_ROPE_DIM // 2, -1.0, 1.0), F32),
        jnp.tile(q_g.astype(F32), LANES // HEAD_DIM),
        jnp.tile(k_g.astype(F32), LANES // HEAD_DIM),
        jnp.concatenate([i_g.astype(F32), zeros]),
        jnp.concatenate([i_b.astype(F32), zeros]),
    ]
    return jnp.stack(rows, axis=0)


def _ffn_schedule(counts, n_blocks):
    gend = jnp.cumsum(counts).astype(I32)
    gstart = gend - counts
    first_blk = gstart // ROW_BLOCK
    last_blk = (jnp.maximum(gend, 1) - 1) // ROW_BLOCK
    n_items = jnp.where(counts > 0, last_blk - first_blk + 1, 0)
    item_end = jnp.cumsum(n_items).astype(I32)
    item_start = item_end - n_items
    total = item_end[-1]
    w = jnp.minimum(jnp.arange(n_blocks + N_EXPERTS - 1, dtype=I32), total - 1)
    item_e = jnp.sum((item_end[None, :] <= w[:, None]).astype(I32), axis=1)
    item_blk = first_blk[item_e] + w - item_start[item_e]
    return gstart, gend, item_e, item_blk, total.reshape(1)


def _layer(x, positions, norm1_g, w_in, q_norm_g, k_norm_g, idx_k_norm_g, idx_k_norm_b, w_branch_attn,
           pool_mix_w, pool_scale, w_branch_pool, w_out, norm2_g, router_w, router_b, w_gate_up,
           b_gate_up, w_down, b_down):
    b, s, d = x.shape
    t = b * s
    assert (t * TOP_K) % ROW_BLOCK == 0 and s % QUERY_BLOCK == 0
    lane = np.arange(LANES)
    gsum = jnp.asarray(lane[:, None] // HEAD_DIM == lane[None, :] // HEAD_DIM, BF16)

    qt, k, vt, qit, ki, wit, pool, gates = _proj_call(
        x, positions.reshape(b, s, 1), norm1_g.reshape(1, d), _pack_w_in(w_in),
        _const_rows(q_norm_g, k_norm_g, idx_k_norm_g, idx_k_norm_b), gsum)
    attn = _attn_call(qt, k, vt, qit, ki, wit)
    x1 = _merge_call(x, attn, pool, gates, w_branch_attn.astype(BF16), pool_mix_w.astype(BF16),
                     pool_scale.reshape(1, POOL_WIDTH), w_branch_pool.astype(BF16), w_out.astype(BF16))
    x1 = x1.reshape(t, d)

    tm_r = min(512, t)
    tri = jnp.asarray(np.tril(np.ones((tm_r, tm_r), np.float32), -1), BF16)
    rw = jnp.pad(router_w, ((0, 0), (0, LANES - N_EXPERTS))).astype(BF16)
    rb = jnp.pad(router_b, (0, LANES - N_EXPERTS)).reshape(1, LANES)
    h2, meta, rgate, counts = _router_call(x1, norm2_g.reshape(1, d), rw, rb, tri)

    gstart, gend, item_e, item_blk, nact = _ffn_schedule(counts[0, :N_EXPERTS], t * TOP_K // ROW_BLOCK)
    slots = (gstart[meta[:, 0:TOP_K]] + meta[:, TOP_K:2 * TOP_K]).reshape(t * TOP_K)
    xs = _dispatch_call(slots, h2)
    ys = _ffn_call(item_e, item_blk, nact, gstart, gend, xs, w_gate_up,
                   b_gate_up.reshape(N_EXPERTS, 1, 2 * EXPERT_DIM), w_down, b_down.reshape(N_EXPERTS, 1, d))
    out = _combine_call(slots, ys, rgate, x1)
    return out.reshape(b, s, d)


def kernel(x, positions, norm1_g, w_in, q_norm_g, k_norm_g, idx_k_norm_g, idx_k_norm_b, w_branch_attn,
           pool_mix_w, pool_scale, w_branch_pool, w_out, norm2_g, router_w, router_b, w_gate_up,
           b_gate_up, w_down, b_down):
    for l in range(norm1_g.shape[0]):
        x = _layer(x, positions, norm1_g[l], w_in[l], q_norm_g[l], k_norm_g[l], idx_k_norm_g[l],
                   idx_k_norm_b[l], w_branch_attn[l], pool_mix_w[l], pool_scale[l], w_branch_pool[l],
                   w_out[l], norm2_g[l], router_w[l], router_b[l], w_gate_up[l], b_gate_up[l],
                   w_down[l], b_down[l])
    return x
```

```python
import functools

import numpy as np
import jax
import jax.numpy as jnp
from jax import lax
from jax.experimental import pallas as pl
from jax.experimental.pallas import tpu as pltpu

F32 = jnp.float32
BF16 = jnp.bfloat16
I32 = jnp.int32

D_MODEL = 1024
CHUNK = 64
CHUNK_SHIFT = CHUNK.bit_length() - 1
ATTN_HEADS = 8
ATTN_KV_HEADS = 2
HEAD_DIM = 64
GROUPS = ATTN_HEADS // ATTN_KV_HEADS
ATTN_WIDTH = ATTN_HEADS * HEAD_DIM
KV_WIDTH = ATTN_KV_HEADS * HEAD_DIM
ATTN_SCALE = HEAD_DIM ** -0.5
IDX_HEADS = 4
IDX_DIM = 64
IDX_ROPE_DIM = 32
IDX_SCALE = (IDX_HEADS ** -0.5) * (IDX_DIM ** -0.5)
IDX_TOPK_MAX = 256
QUERY_BLOCK = 256
POOL_WINDOWS = (2, 4, 8, 16)
POOL_WIDTH = 512
POOL_GROUP_DIM = 128
POOL_HALO = 16
N_EXPERTS = 32
TOP_K = 4
EXPERT_DIM = 1024
SWIGLU_ALPHA = 1.702
SWIGLU_LIMIT = 7.0
ROPE_THETA = 10000.0
NORM_EPS = 1e-6

LANES = 128
SUBLANES = 8
VMEM_LIMIT = 56 * 1024 * 1024
FLT_MAX = float(np.finfo(np.float32).max)
MASKED = -1e30

C_Q = 0
C_K = 512
C_V = 640
C_QI = 768
C_KIW = 1024
C_POOL = 1152
C_GATE = 1664
W_PACKED = 3712
C_SMALL_END = C_POOL

ROW_BLOCK = 512
FFN_SPLIT = 1
FFN_COLS = EXPERT_DIM // FFN_SPLIT
FOLD_ROWS = 4 * SUBLANES
VALUE_BISECT_ITERS = 16
BISECT_CAP = 64
BISECT_HEAD = 14
BISECT_UNROLL = 2
ATT_CK = 256
VT_ROWS = HEAD_DIM + 2 * SUBLANES
LOG2E = float(np.log2(np.e))


def _cparams(*sem):
    return pltpu.CompilerParams(dimension_semantics=sem, vmem_limit_bytes=VMEM_LIMIT)


def _swap_halves(xc, first, half):
    return jnp.where(first, pltpu.roll(xc, LANES - half, 1), pltpu.roll(xc, half, 1))


def _proj_kernel(x_ref, pos_ref, g1_ref, w_ref, rows_ref, gsum_ref,
                 qt_ref, k_ref, vt_ref, qit_ref, ki_ref, wit_ref, pool_ref, gate_ref):
    x = x_ref[0]
    tm = x.shape[0]
    ms = jnp.mean(x * x, axis=-1, keepdims=True)
    h = (x * lax.rsqrt(ms + NORM_EPS) * g1_ref[...]).astype(BF16)
    d1 = jnp.dot(h, w_ref[:, 0:C_SMALL_END], preferred_element_type=F32)
    pool_ref[0] = jnp.dot(h, w_ref[:, C_POOL:C_GATE], preferred_element_type=F32)
    gate_ref[0] = jnp.dot(h, w_ref[:, C_GATE:W_PACKED], preferred_element_type=F32)

    pos = pos_ref[0].astype(F32)
    rows = rows_ref[...]
    lane = lax.broadcasted_iota(I32, (tm, LANES), 1)
    first_m = (lane & (HEAD_DIM - 1)) < HEAD_DIM // 2
    first_i = (lane & (IDX_ROPE_DIM - 1)) < IDX_ROPE_DIM // 2
    ang_m = pos * rows[0:1]
    cos_m = jnp.cos(ang_m)
    sin_m = jnp.sin(ang_m) * rows[1:2]
    ang_i = pos * rows[2:3]
    cos_i = jnp.cos(ang_i)
    sin_i = jnp.sin(ang_i) * rows[3:4]
    gsum = gsum_ref[...]

    def head_rms(xc, grow):
        sq = xc * xc
        hi = sq.astype(BF16)
        lo = (sq - hi.astype(F32)).astype(BF16)
        ssum = (jnp.dot(hi, gsum, preferred_element_type=F32)
                + jnp.dot(lo, gsum, preferred_element_type=F32))
        return xc * lax.rsqrt(ssum * (1.0 / HEAD_DIM) + NORM_EPS) * grow

    def rope_m(xc):
        return xc * cos_m + _swap_halves(xc, first_m, HEAD_DIM // 2) * sin_m

    def rope_i(xc):
        return xc * cos_i + _swap_halves(xc, first_i, IDX_ROPE_DIM // 2) * sin_i

    for c in range(ATTN_WIDTH // LANES):
        qc = d1[:, C_Q + c * LANES:C_Q + (c + 1) * LANES]
        qr = rope_m(head_rms(qc, rows[4:5])) * (ATTN_SCALE * LOG2E)
        qt_ref[0, c * LANES:(c + 1) * LANES, :] = qr.T.astype(BF16)

    kr = rope_m(head_rms(d1[:, C_K:C_K + KV_WIDTH], rows[5:6]))
    vt = d1[:, C_V:C_V + KV_WIDTH].T
    for r in range(ATTN_KV_HEADS):
        k_ref[0, r] = kr[:, r * HEAD_DIM:(r + 1) * HEAD_DIM].astype(BF16)
        vt_ref[0, r, 0:HEAD_DIM, :] = vt[r * HEAD_DIM:(r + 1) * HEAD_DIM].astype(BF16)
        vt_ref[0, r, HEAD_DIM:VT_ROWS, :] = jnp.ones((VT_ROWS - HEAD_DIM, tm), BF16)

    for c in range(IDX_HEADS * IDX_DIM // LANES):
        qc = d1[:, C_QI + c * LANES:C_QI + (c + 1) * LANES]
        qit_ref[0, c * LANES:(c + 1) * LANES, :] = rope_i(qc).T.astype(BF16)

    kiw = d1[:, C_KIW:C_KIW + LANES]
    in_ki = lane < IDX_DIM
    mu = jnp.sum(jnp.where(in_ki, kiw, 0.0), axis=-1, keepdims=True) * (1.0 / IDX_DIM)
    dv = jnp.where(in_ki, kiw - mu, 0.0)
    var = jnp.sum(dv * dv, axis=-1, keepdims=True) * (1.0 / IDX_DIM)
    kin = dv * lax.rsqrt(var + NORM_EPS) * rows[6:7] + rows[7:8]
    ki_ref[0] = rope_i(kin)[:, 0:IDX_DIM].astype(BF16)
    wit_ref[0] = (kiw * IDX_SCALE).T[IDX_DIM:IDX_DIM + SUBLANES]


def _proj_call(x, pos3, g1, w_packed, rows, gsum):
    b, s, d = x.shape
    tm = min(512, s)
    grid = (b, s // tm)
    full2 = lambda i, j: (0, 0)
    out_shape = (
        jax.ShapeDtypeStruct((b, ATTN_WIDTH, s), BF16),
        jax.ShapeDtypeStruct((b, ATTN_KV_HEADS, s, HEAD_DIM), BF16),
        jax.ShapeDtypeStruct((b, ATTN_KV_HEADS, VT_ROWS, s), BF16),
        jax.ShapeDtypeStruct((b, IDX_HEADS * IDX_DIM, s), BF16),
        jax.ShapeDtypeStruct((b, s, IDX_DIM), BF16),
        jax.ShapeDtypeStruct((b, SUBLANES, s), F32),
        jax.ShapeDtypeStruct((b, s, POOL_WIDTH), F32),
        jax.ShapeDtypeStruct((b, s, 2 * D_MODEL), F32),
    )
    in_specs = [
        pl.BlockSpec((1, tm, d), lambda i, j: (i, j, 0)),
        pl.BlockSpec((1, tm, 1), lambda i, j: (i, j, 0)),
        pl.BlockSpec((1, d), full2),
        pl.BlockSpec((d, W_PACKED), full2),
        pl.BlockSpec((8, LANES), full2),
        pl.BlockSpec((LANES, LANES), full2),
    ]
    out_specs = (
        pl.BlockSpec((1, ATTN_WIDTH, tm), lambda i, j: (i, 0, j)),
        pl.BlockSpec((1, ATTN_KV_HEADS, tm, HEAD_DIM), lambda i, j: (i, 0, j, 0)),
        pl.BlockSpec((1, ATTN_KV_HEADS, VT_ROWS, tm), lambda i, j: (i, 0, 0, j)),
        pl.BlockSpec((1, IDX_HEADS * IDX_DIM, tm), lambda i, j: (i, 0, j)),
        pl.BlockSpec((1, tm, IDX_DIM), lambda i, j: (i, j, 0)),
        pl.BlockSpec((1, SUBLANES, tm), lambda i, j: (i, 0, j)),
        pl.BlockSpec((1, tm, POOL_WIDTH), lambda i, j: (i, j, 0)),
        pl.BlockSpec((1, tm, 2 * D_MODEL), lambda i, j: (i, j, 0)),
    )
    return pl.pallas_call(
        _proj_kernel, grid=grid, in_specs=in_specs, out_specs=out_specs, out_shape=out_shape,
        compiler_params=_cparams("parallel", "parallel"), name="proj",
    )(x, pos3, g1, w_packed, rows, gsum)


def _sort_key(x):
    b = lax.bitcast_convert_type(x, I32)
    return b ^ ((b >> 31) & 0x7FFFFFFF)


def _unsort_key(k):
    return lax.bitcast_convert_type(k ^ ((k >> 31) & 0x7FFFFFFF), F32)


def _attn_kernel(qt_ref, qit_ref, wit_ref, ki_ref, k_ref, vt_ref, tri_ref, o_ref, sc_ref, sa_ref, sb_ref,
                 *, n_sel, ck):
    qb = QUERY_BLOCK
    t0 = pl.program_id(1) * qb
    nck = (t0 + qb + ck - 1) // ck
    qpos = t0 + lax.broadcasted_iota(I32, (1, qb), 1)
    cend = ((qpos >> CHUNK_SHIFT) + 1) * CHUNK
    key_ck = lax.broadcasted_iota(I32, (ck, qb), 0)
    k_sel = float(n_sel)
    cnt_rows = min(ck, 64 * SUBLANES * LANES // qb)

    def fold(x, op):
        return op(x.reshape(ck // FOLD_ROWS, FOLD_ROWS, qb), axis=0)

    qit = qit_ref[0]
    rhs_i = jnp.concatenate([qit[h * IDX_DIM:(h + 1) * IDX_DIM] for h in range(IDX_HEADS)], axis=1)
    wit = wit_ref[0]

    def score_body(c, carry):
        mxp, mnp, c0p, c1p = carry
        off = pl.multiple_of(c * ck, ck)
        lg = jnp.dot(ki_ref[0, pl.ds(off, ck), :], rhs_i, preferred_element_type=F32)
        sc = jnp.maximum(lg[:, 0:qb], 0.0) * wit[0:1]
        for h in range(1, IDX_HEADS):
            sc = sc + jnp.maximum(lg[:, h * qb:(h + 1) * qb], 0.0) * wit[h:h + 1]
        adm = (off + key_ck) < cend
        sc = jnp.where(adm, sc, -jnp.inf)
        sc_ref[pl.ds(off, ck), :] = sc
        mxp = jnp.maximum(mxp, fold(sc, jnp.max))
        mnp = jnp.minimum(mnp, fold(jnp.where(adm, sc, jnp.inf), jnp.min))
        c0p = c0p + fold(jnp.where(sc >= 0.0, 1.0, 0.0), jnp.sum)
        c1p = c1p + fold(jnp.where(sc > 0.0, 1.0, 0.0), jnp.sum)
        return mxp, mnp, c0p, c1p

    init = (jnp.full((FOLD_ROWS, qb), -jnp.inf, F32), jnp.full((FOLD_ROWS, qb), jnp.inf, F32),
            jnp.zeros((FOLD_ROWS, qb), F32), jnp.zeros((FOLD_ROWS, qb), F32))
    mxp, mnp, c0p, c1p = lax.fori_loop(0, nck, score_body, init)
    mx = jnp.max(mxp, axis=0, keepdims=True)
    mn = jnp.min(mnp, axis=0, keepdims=True)
    c0 = jnp.sum(c0p, axis=0, keepdims=True)
    c1 = jnp.sum(c1p, axis=0, keepdims=True)

    def count(pred):
        def body(c, acc):
            for sb in range(ck // cnt_rows):
                off = pl.multiple_of(c * ck + sb * cnt_rows, cnt_rows)
                hit = pred(sc_ref[pl.ds(off, cnt_rows), :], off + key_ck[0:cnt_rows])
                acc = acc + jnp.sum(jnp.where(hit, 1.0, 0.0).reshape(cnt_rows // FOLD_ROWS, FOLD_ROWS, qb),
                                    axis=0)
            return acc
        acc = lax.fori_loop(0, nck, body, jnp.zeros((FOLD_ROWS, qb), F32))
        return jnp.sum(acc, axis=0, keepdims=True)

    small = cend.astype(F32) <= k_sel
    at_zero = jnp.logical_and(jnp.logical_not(small), jnp.logical_and(c1 < k_sel, c0 >= k_sel))
    positive = jnp.logical_and(jnp.logical_not(small), c1 >= k_sel)
    lo0 = jnp.where(positive, 0.0, mn)
    hi0 = jnp.where(positive, jnp.minimum(2.0 * mx, FLT_MAX), 0.0)
    thr0 = jnp.where(small, -FLT_MAX, 0.0)
    done0 = jnp.where(jnp.logical_or(small, at_zero), 1.0, 0.0)
    tie0 = jnp.where(jnp.logical_and(at_zero, c0 > k_sel), 1.0, 0.0)
    chi0 = jnp.where(positive, 0.0, c0)
    left0 = jnp.sum(1.0 - done0)

    def bisect_cond(st):
        it, left = st[0], st[1]
        return jnp.logical_and(it < BISECT_CAP, left > 0.0)

    def bisect_step(it, lo, hi, thr, done, tie, chi):
        mid_v = 0.5 * lo + 0.5 * hi
        klo = _sort_key(lo)
        khi = _sort_key(hi)
        mid_k = _unsort_key((klo & khi) + ((klo ^ khi) >> 1))
        mid = jnp.where(it < VALUE_BISECT_ITERS, mid_v, mid_k)
        stuck = jnp.logical_or(mid <= lo, mid >= hi)
        cnt = count(lambda s, _: s >= mid)
        active = done == 0.0
        moving = jnp.logical_and(active, jnp.logical_not(stuck))
        hit = jnp.logical_and(moving, cnt == k_sel)
        new_tie = jnp.logical_and(active, stuck)
        thr = jnp.where(hit, mid, jnp.where(new_tie, lo, thr))
        tie = jnp.where(new_tie, 1.0, tie)
        done = jnp.where(jnp.logical_or(hit, new_tie), 1.0, done)
        upd = jnp.logical_and(moving, jnp.logical_not(hit))
        lo = jnp.where(jnp.logical_and(upd, cnt >= k_sel), mid, lo)
        lower = jnp.logical_and(upd, cnt < k_sel)
        hi = jnp.where(lower, mid, hi)
        chi = jnp.where(lower, cnt, chi)
        return lo, hi, thr, done, tie, chi

    def bisect_body(steps, st):
        it, _, lo, hi, thr, done, tie, chi = st
        for u in range(steps):
            lo, hi, thr, done, tie, chi = bisect_step(it + u, lo, hi, thr, done, tie, chi)
        return it + steps, jnp.sum(1.0 - done), lo, hi, thr, done, tie, chi

    st = (jnp.int32(0), left0, lo0, hi0, thr0, done0, tie0, chi0)
    st = lax.while_loop(lambda s: jnp.logical_and(s[0] < BISECT_HEAD, s[1] > 0.0),
                        functools.partial(bisect_body, BISECT_HEAD), st)
    st = lax.while_loop(bisect_cond, functools.partial(bisect_body, BISECT_UNROLL), st)
    thr, tie = st[4], st[6]
    n_above = jnp.where(at_zero, c1, st[7])

    @pl.when(jnp.sum(tie) > 0.0)
    def _():
        need = jnp.where(tie > 0.0, k_sel - n_above, FLT_MAX)

        def drop_body(c, before):
            off = pl.multiple_of(c * ck, ck)
            blk = sc_ref[pl.ds(off, ck), :]
            eq = blk == thr
            eqf = jnp.where(eq, 1.0, 0.0)
            rank = before + jnp.dot(tri_ref[...], eqf.astype(BF16), preferred_element_type=F32)
            sc_ref[pl.ds(off, ck), :] = jnp.where(jnp.logical_and(eq, rank >= need), -jnp.inf, blk)
            return before + jnp.sum(fold(eqf, jnp.sum), axis=0, keepdims=True)

        lax.fori_loop(0, nck, drop_body, jnp.zeros((1, qb), F32))

    qt = qt_ref[0]
    rhs = [jnp.concatenate(
        [qt[(r * GROUPS + g) * HEAD_DIM:(r * GROUPS + g + 1) * HEAD_DIM] for g in range(GROUPS)],
        axis=1) for r in range(ATTN_KV_HEADS)]

    n_att = (t0 + qb + ATT_CK - 1) // ATT_CK

    def step_offset(step):
        return pl.multiple_of(jnp.minimum(step, n_att - 1) * ATT_CK, ATT_CK)

    def qk_logits(step, buf):
        off = step_offset(step)
        thr_step = jnp.where(step < n_att, thr, jnp.inf)
        bias = jnp.where(sc_ref[pl.ds(off, ATT_CK), :] >= thr_step, 0.0, MASKED)
        for r in range(ATTN_KV_HEADS):
            lg = jnp.dot(k_ref[0, r, pl.ds(off, ATT_CK), :], rhs[r], preferred_element_type=F32)
            for g in range(GROUPS):
                buf[r, :, g * qb:(g + 1) * qb] = lg[:, g * qb:(g + 1) * qb] + bias

    def softmax_pv(step, buf, carry):
        off = step_offset(step)
        out = []
        for r in range(ATTN_KV_HEADS):
            m, acc = carry[r]
            m_new = jnp.maximum(m, jnp.max(buf[r], axis=0, keepdims=True))
            p = jnp.exp2(buf[r] - m_new).astype(BF16)
            pv = jnp.dot(vt_ref[0, r, :, pl.ds(off, ATT_CK)], p, preferred_element_type=F32)
            out.append((m_new, jnp.exp2(m - m_new) * acc + pv))
        return tuple(out)

    def att_body(i, carry):
        qk_logits(2 * i + 1, sb_ref)
        carry = softmax_pv(2 * i, sa_ref, carry)
        qk_logits(2 * i + 2, sa_ref)
        return softmax_pv(2 * i + 1, sb_ref, carry)

    init_a = tuple((jnp.full((1, GROUPS * qb), MASKED, F32), jnp.zeros((VT_ROWS, GROUPS * qb), F32))
                   for _ in range(ATTN_KV_HEADS))
    qk_logits(0, sa_ref)
    fin = lax.fori_loop(0, (n_att + 1) // 2, att_body, init_a)
    outs = []
    for r in range(ATTN_KV_HEADS):
        acc = fin[r][1]
        o = acc[0:HEAD_DIM] / acc[HEAD_DIM:HEAD_DIM + 1]
        outs.extend(o[:, g * qb:(g + 1) * qb] for g in range(GROUPS))
    o_ref[0] = jnp.concatenate(outs, axis=0).T.astype(BF16)


def _attn_call(qt, k, vt, qit, ki, wit):
    b, _, s = qt.shape
    n_sel = min(IDX_TOPK_MAX, s // 4)
    ck = min(512, s)
    grid = (b, s // QUERY_BLOCK)
    in_specs = [
        pl.BlockSpec((1, ATTN_WIDTH, QUERY_BLOCK), lambda i, j: (i, 0, j)),
        pl.BlockSpec((1, IDX_HEADS * IDX_DIM, QUERY_BLOCK), lambda i, j: (i, 0, j)),
        pl.BlockSpec((1, SUBLANES, QUERY_BLOCK), lambda i, j: (i, 0, j)),
        pl.BlockSpec((1, s, IDX_DIM), lambda i, j: (i, 0, 0)),
        pl.BlockSpec((1, ATTN_KV_HEADS, s, HEAD_DIM), lambda i, j: (i, 0, 0, 0)),
        pl.BlockSpec((1, ATTN_KV_HEADS, VT_ROWS, s), lambda i, j: (i, 0, 0, 0)),
        pl.BlockSpec((ck, ck), lambda i, j: (0, 0)),
    ]
    tri = jnp.asarray(np.tril(np.ones((ck, ck), np.float32), -1), BF16)
    return pl.pallas_call(
        functools.partial(_attn_kernel, n_sel=n_sel, ck=ck),
        grid=grid, in_specs=in_specs,
        out_specs=pl.BlockSpec((1, QUERY_BLOCK, ATTN_WIDTH), lambda i, j: (i, j, 0)),
        out_shape=jax.ShapeDtypeStruct((b, s, ATTN_WIDTH), BF16),
        scratch_shapes=[pltpu.VMEM((s, QUERY_BLOCK), F32),
                        pltpu.VMEM((ATTN_KV_HEADS, ATT_CK, GROUPS * QUERY_BLOCK), F32),
                        pltpu.VMEM((ATTN_KV_HEADS, ATT_CK, GROUPS * QUERY_BLOCK), F32)],
        compiler_params=_cparams("parallel", "parallel"), name="attn",
    )(qt, qit, wit, ki, k, vt, tri)


def _merge_kernel(x_ref, attn_ref, pool_ref, gate_ref, wba_ref, mix_ref, pscale_ref, wbp_ref, wout_ref,
                  g2_ref, rw_ref, rb_ref, tri_ref,
                  o_ref, h_ref, meta_ref, rgate_ref, cnt_ref, ext_ref, carry_ref):
    tm = x_ref.shape[1]
    j = pl.program_id(1)

    @pl.when(j == 0)
    def _():
        ext_ref[0:POOL_HALO] = jnp.zeros((POOL_HALO, POOL_WIDTH), F32)

    @pl.when(j > 0)
    def _():
        ext_ref[0:POOL_HALO] = ext_ref[tm:tm + POOL_HALO]

    ext_ref[POOL_HALO:POOL_HALO + tm] = pool_ref[0]
    t = j * tm + lax.broadcasted_iota(I32, (tm, 1), 0)
    mixed = []
    for g, w in enumerate(POOL_WINDOWS):
        cols = slice(g * POOL_GROUP_DIM, (g + 1) * POOL_GROUP_DIM)
        cur = ext_ref[POOL_HALO:POOL_HALO + tm, cols]
        wsum = cur
        for i in range(1, w):
            wsum = wsum + ext_ref[POOL_HALO - i:POOL_HALO - i + tm, cols]
        cnt = jnp.minimum(t + 1, w).astype(F32)
        dev = (wsum / cnt - cur).astype(BF16)
        mixed.append(jnp.dot(dev, mix_ref[g], preferred_element_type=F32))
    pooled = (jnp.concatenate(mixed, axis=1) * pscale_ref[...]).astype(BF16)
    branch_a = jnp.dot(attn_ref[0], wba_ref[...], preferred_element_type=F32)
    branch_p = jnp.dot(pooled, wbp_ref[...], preferred_element_type=F32)
    gates = gate_ref[0]
    merged = (jax.nn.sigmoid(gates[:, 0:D_MODEL]) * branch_a
              + jax.nn.sigmoid(gates[:, D_MODEL:2 * D_MODEL]) * branch_p)
    x1 = x_ref[0] + jnp.dot(merged.astype(BF16), wout_ref[...], preferred_element_type=F32)
    o_ref[0] = x1
    _route_tile(x1, jnp.logical_and(pl.program_id(0) == 0, j == 0), g2_ref, rw_ref, rb_ref, tri_ref,
                h_ref, meta_ref, rgate_ref, cnt_ref, carry_ref)


def _merge_call(x, attn, pool, gates, wba, mix, pscale, wbp, wout, g2, rw, rb):
    b, s, d = x.shape
    tm = min(256, s)
    nt = s // tm
    t = b * s
    tile = lambda w: pl.BlockSpec((1, tm, w), lambda i, j: (i, j, 0))
    flat = lambda rows, w: pl.BlockSpec((rows, w), lambda i, j: (i * nt + j, 0))
    full2 = lambda i, j: (0, 0)
    tri = jnp.asarray(np.tril(np.ones((tm, tm), np.float32), -1), BF16)
    in_specs = [
        tile(d), tile(ATTN_WIDTH), tile(POOL_WIDTH), tile(2 * D_MODEL),
        pl.BlockSpec((ATTN_WIDTH, d), full2),
        pl.BlockSpec((len(POOL_WINDOWS), POOL_GROUP_DIM, POOL_GROUP_DIM), lambda i, j: (0, 0, 0)),
        pl.BlockSpec((1, POOL_WIDTH), full2),
        pl.BlockSpec((POOL_WIDTH, d), full2),
        pl.BlockSpec((d, d), full2),
        pl.BlockSpec((1, d), full2),
        pl.BlockSpec((d, LANES), full2),
        pl.BlockSpec((1, LANES), full2),
        pl.BlockSpec((tm, tm), full2),
    ]
    out_shape = (
        jax.ShapeDtypeStruct((b, s, d), F32),
        jax.ShapeDtypeStruct((t * ROW_TILES, LANES), F32),
        jax.ShapeDtypeStruct((2 * TOP_K, t), I32),
        jax.ShapeDtypeStruct((t, LANES), F32),
        jax.ShapeDtypeStruct((1, LANES), I32),
    )
    out_specs = (
        tile(d), flat(tm * ROW_TILES, LANES),
        pl.BlockSpec((2 * TOP_K, tm), lambda i, j: (0, i * nt + j)),
        flat(tm, LANES), pl.BlockSpec((1, LANES), full2),
    )
    return pl.pallas_call(
        _merge_kernel, grid=(b, nt), in_specs=in_specs, out_specs=out_specs, out_shape=out_shape,
        scratch_shapes=[pltpu.VMEM((POOL_HALO + tm, POOL_WIDTH), F32), pltpu.VMEM((1, LANES), F32)],
        compiler_params=_cparams("arbitrary", "arbitrary"), name="merge",
    )(x, attn, pool, gates, wba, mix, pscale, wbp, wout, g2, rw, rb, tri)


ROW_TILES = D_MODEL // LANES


def _load_rows(ref, n, *lead):
    return jnp.concatenate(
        [ref[(*lead, pl.ds(c, n, stride=ROW_TILES), slice(None))] for c in range(ROW_TILES)], axis=1)


def _store_rows(ref, val):
    for c in range(ROW_TILES):
        ref[pl.ds(c, val.shape[0], stride=ROW_TILES), :] = val[:, c * LANES:(c + 1) * LANES]


def _row_tile(ref, i):
    return ref.at[pl.ds(pl.multiple_of(i * ROW_TILES, ROW_TILES), ROW_TILES)]


def _route_tile(x, first_tile, g2_ref, rw_ref, rb_ref, tri_ref, h_ref, meta_ref, gate_ref, cnt_ref, carry_ref):
    tm = x.shape[0]

    @pl.when(first_tile)
    def _():
        carry_ref[...] = jnp.zeros((1, LANES), F32)

    ms = jnp.mean(x * x, axis=-1, keepdims=True)
    h = x * lax.rsqrt(ms + NORM_EPS) * g2_ref[...]
    _store_rows(h_ref, h)
    logits = jnp.dot(h.astype(BF16), rw_ref[...], preferred_element_type=F32) + rb_ref[...]
    lane = lax.broadcasted_iota(I32, (tm, LANES), 1).astype(F32)
    work = jnp.where(lane < N_EXPERTS, logits, -jnp.inf)
    vals, idxs = [], []
    for _ in range(TOP_K):
        m = jnp.max(work, axis=-1, keepdims=True)
        idx = jnp.min(jnp.where(work == m, lane, float(LANES)), axis=-1, keepdims=True)
        vals.append(m)
        idxs.append(idx)
        work = jnp.where(lane == idx, -jnp.inf, work)
    exps = [jnp.exp(v - vals[0]) for v in vals]
    denom = exps[0] + exps[1] + exps[2] + exps[3]
    member = jnp.zeros((tm, LANES), F32)
    for idx in idxs:
        member = member + jnp.where(lane == idx, 1.0, 0.0)
    before = jnp.dot(tri_ref[...], member.astype(BF16), preferred_element_type=F32) + carry_ref[...]
    meta = jnp.zeros((tm, LANES), F32)
    gate = jnp.zeros((tm, LANES), F32)
    for k in range(TOP_K):
        rank = jnp.sum(jnp.where(lane == idxs[k], before, 0.0), axis=-1, keepdims=True)
        meta = jnp.where(lane == float(k), idxs[k], meta)
        meta = jnp.where(lane == float(TOP_K + k), rank, meta)
        gate = jnp.where(lane == float(k), exps[k] / denom, gate)
    meta_ref[...] = meta.T[0:2 * TOP_K].astype(I32)
    gate_ref[...] = gate
    total = carry_ref[...] + jnp.sum(member, axis=0, keepdims=True)
    carry_ref[...] = total
    cnt_ref[...] = total.astype(I32)


ROW_UNROLL = SUBLANES
ROW_DMA_TOKENS = 256


def _dispatch_kernel(slot_ref, h_ref, xs_ref, sem, *, tm):
    def issue(i, carry):
        for u in range(ROW_UNROLL):
            j = i * ROW_UNROLL + u
            for k in range(TOP_K):
                pltpu.make_async_copy(_row_tile(h_ref, j), _row_tile(xs_ref, slot_ref[k * tm + j]),
                                      sem).start(priority=k % 2)
        return carry

    lax.fori_loop(0, tm // ROW_UNROLL, issue, 0)

    def drain(j, carry):
        for k in range(TOP_K):
            pltpu.make_async_copy(_row_tile(h_ref, 0), _row_tile(xs_ref, 0), sem).wait()
        return carry

    lax.fori_loop(0, tm, drain, 0)


def _dispatch_call(slots, h2):
    t = h2.shape[0] // ROW_TILES
    tm = ROW_DMA_TOKENS
    return pl.pallas_call(
        functools.partial(_dispatch_kernel, tm=tm),
        grid=(t // tm,),
        in_specs=[pl.BlockSpec((tm * TOP_K,), lambda i: (i,), memory_space=pltpu.SMEM),
                  pl.BlockSpec((tm * ROW_TILES, LANES), lambda i: (i, 0))],
        out_specs=pl.BlockSpec(memory_space=pl.ANY),
        scratch_shapes=[pltpu.SemaphoreType.DMA(())],
        out_shape=jax.ShapeDtypeStruct((t * TOP_K * ROW_TILES, LANES), F32),
        compiler_params=_cparams("arbitrary"), name="dispatch",
    )(slots, h2)


def _ffn_kernel(item_e_ref, item_blk_ref, nact_ref, gstart_ref, gend_ref,
                xs_ref, wgu_ref, bgu_ref, wd_ref, bd_ref, o_ref, wgu_bf, wd_bf):
    w = pl.program_id(0)
    e = item_e_ref[w]
    blk = item_blk_ref[w]
    prev = jnp.maximum(w - 1, 0)
    active = w < nact_ref[0]
    new_expert = jnp.logical_or(w == 0, e != item_e_ref[prev])
    first_visit = jnp.logical_or(w == 0, blk != item_blk_ref[prev])

    @pl.when(jnp.logical_and(active, new_expert))
    def _():
        wgu_bf[...] = wgu_ref[0].astype(BF16)
        wd_bf[...] = wd_ref[0].astype(BF16)

    @pl.when(active)
    def _():
        xb = _load_rows(xs_ref, ROW_BLOCK).astype(BF16)
        res = bd_ref[0]
        for c in range(FFN_SPLIT):
            g0, g1 = c * FFN_COLS, (c + 1) * FFN_COLS
            gate = jnp.dot(xb, wgu_bf[:, g0:g1], preferred_element_type=F32) + bgu_ref[0, :, g0:g1]
            up = (jnp.dot(xb, wgu_bf[:, EXPERT_DIM + g0:EXPERT_DIM + g1], preferred_element_type=F32)
                  + bgu_ref[0, :, EXPERT_DIM + g0:EXPERT_DIM + g1])
            gate = jnp.minimum(gate, SWIGLU_LIMIT)
            up = jnp.clip(up, -SWIGLU_LIMIT, SWIGLU_LIMIT)
            act = gate * jax.nn.sigmoid(SWIGLU_ALPHA * gate) * (up + 1.0)
            res = res + jnp.dot(act.astype(BF16), wd_bf[g0:g1, :], preferred_element_type=F32)
        row = blk * ROW_BLOCK + lax.broadcasted_iota(I32, (ROW_BLOCK, 1), 0)
        mine = jnp.logical_and(row >= gstart_ref[e], row < gend_ref[e])

        @pl.when(first_visit)
        def _():
            _store_rows(o_ref, jnp.where(mine, res, 0.0))

        @pl.when(jnp.logical_not(first_visit))
        def _():
            _store_rows(o_ref, jnp.where(mine, res, _load_rows(o_ref, ROW_BLOCK)))


def _ffn_call(item_e, item_blk, nact, gstart, gend, xs, wgu, bgu, wd, bd):
    d = D_MODEL
    rows = lambda w, ie, ib, *_: (ib[w], 0)
    exp3 = lambda w, ie, *_: (ie[w], 0, 0)
    return pl.pallas_call(
        _ffn_kernel,
        grid_spec=pltpu.PrefetchScalarGridSpec(
            num_scalar_prefetch=5, grid=(item_e.shape[0],),
            in_specs=[pl.BlockSpec((ROW_BLOCK * ROW_TILES, LANES), rows),
                      pl.BlockSpec((1, d, 2 * EXPERT_DIM), exp3),
                      pl.BlockSpec((1, 1, 2 * EXPERT_DIM), exp3),
                      pl.BlockSpec((1, EXPERT_DIM, d), exp3),
                      pl.BlockSpec((1, 1, d), exp3)],
            out_specs=pl.BlockSpec((ROW_BLOCK * ROW_TILES, LANES), rows),
            scratch_shapes=[pltpu.VMEM((d, 2 * EXPERT_DIM), BF16), pltpu.VMEM((EXPERT_DIM, d), BF16)]),
        out_shape=jax.ShapeDtypeStruct(xs.shape, F32),
        compiler_params=_cparams("arbitrary"), name="ffn",
    )(item_e, item_blk, nact, gstart, gend, xs, wgu, bgu, wd, bd)


def _combine_kernel(slot_ref, slot_next_ref, ys_ref, gate_ref, x_ref, o_ref, buf_ref, sem, *, tm, n_steps):
    i = pl.program_id(0)
    half = i % 2

    def gather(slots, dst_half):
        def issue(it, carry):
            for u in range(ROW_UNROLL):
                j = it * ROW_UNROLL + u
                for k in range(TOP_K):
                    pltpu.make_async_copy(_row_tile(ys_ref, slots[k * tm + j]),
                                          _row_tile(buf_ref.at[dst_half, k], j),
                                          sem.at[dst_half]).start(priority=k % 2)
            return carry
        lax.fori_loop(0, tm // ROW_UNROLL, issue, 0)

    @pl.when(i == 0)
    def _():
        gather(slot_ref, 0)

    @pl.when(i + 1 < n_steps)
    def _():
        gather(slot_next_ref, 1 - half)

    def drain(j, carry):
        for k in range(TOP_K):
            pltpu.make_async_copy(_row_tile(ys_ref, 0), _row_tile(buf_ref.at[half, k], 0),
                                  sem.at[half]).wait()
        return carry

    lax.fori_loop(0, tm, drain, 0)
    gate = gate_ref[...]
    y = x_ref[...]
    for k in range(TOP_K):
        y = y + gate[:, k:k + 1] * _load_rows(buf_ref, tm, half, k)
    o_ref[...] = y


def _combine_call(slots, ys, gates, x1):
    t, d = x1.shape
    tm = ROW_DMA_TOKENS
    n_steps = t // tm
    tile = lambda w: pl.BlockSpec((tm, w), lambda i: (i, 0))
    return pl.pallas_call(
        functools.partial(_combine_kernel, tm=tm, n_steps=n_steps),
        grid=(n_steps,),
        in_specs=[pl.BlockSpec((tm * TOP_K,), lambda i: (i,), memory_space=pltpu.SMEM),
                  pl.BlockSpec((tm * TOP_K,), lambda i: (jnp.minimum(i + 1, n_steps - 1),),
                               memory_space=pltpu.SMEM),
                  pl.BlockSpec(memory_space=pl.ANY), tile(LANES), tile(d)],
        out_specs=tile(d),
        scratch_shapes=[pltpu.VMEM((2, TOP_K, tm * ROW_TILES, LANES), F32), pltpu.SemaphoreType.DMA((2,))],
        out_shape=jax.ShapeDtypeStruct((t, d), F32),
        compiler_params=_cparams("arbitrary"), name="combine",
    )(slots, slots, ys, gates, x1)


def _pack_w_in(w):
    d = w.shape[0]
    o = np.cumsum([0, ATTN_WIDTH, KV_WIDTH, KV_WIDTH, IDX_HEADS * IDX_DIM, IDX_DIM, IDX_HEADS,
                   POOL_WIDTH, D_MODEL, D_MODEL])
    pad = jnp.zeros((d, LANES - IDX_DIM - IDX_HEADS), w.dtype)
    parts = [w[:, o[0]:o[4]], w[:, o[4]:o[6]], pad, w[:, o[6]:o[9]]]
    return jnp.concatenate(parts, axis=1).astype(BF16)


def _const_rows(q_g, k_g, i_g, i_b):
    lane = np.arange(LANES)
    inv_m = ROPE_THETA ** (-jnp.arange(0, HEAD_DIM, 2, dtype=F32) / HEAD_DIM)
    inv_i = ROPE_THETA ** (-jnp.arange(0, IDX_ROPE_DIM, 2, dtype=F32) / IDX_ROPE_DIM)
    in_rope = (lane % IDX_DIM) < IDX_ROPE_DIM
    zeros = jnp.zeros((LANES - IDX_DIM,), F32)
    rows = [
        inv_m[lane % (HEAD_DIM // 2)],
        jnp.asarray(np.where(lane % HEAD_DIM < HEAD_DIM // 2, -1.0, 1.0), F32),
        jnp.where(jnp.asarray(in_rope), inv_i[lane % (IDX_ROPE_DIM // 2)], 0.0),
        jnp.asarray(np.where(lane % IDX_ROPE_DIM < IDX_ROPE_DIM // 2, -1.0, 1.0), F32),
        jnp.tile(q_g.astype(F32), LANES // HEAD_DIM),
        jnp.tile(k_g.astype(F32), LANES // HEAD_DIM),
        jnp.concatenate([i_g.astype(F32), zeros]),
        jnp.concatenate([i_b.astype(F32), zeros]),
    ]
    return jnp.stack(rows, axis=0)


def _ffn_schedule(counts, n_blocks):
    gend = jnp.cumsum(counts).astype(I32)
    gstart = gend - counts
    first_blk = gstart // ROW_BLOCK
    last_blk = (jnp.maximum(gend, 1) - 1) // ROW_BLOCK
    n_items = jnp.where(counts > 0, last_blk - first_blk + 1, 0)
    item_end = jnp.cumsum(n_items).astype(I32)
    item_start = item_end - n_items
    total = item_end[-1]
    w = jnp.minimum(jnp.arange(n_blocks + N_EXPERTS - 1, dtype=I32), total - 1)
    item_e = jnp.sum((item_end[None, :] <= w[:, None]).astype(I32), axis=1)
    item_blk = first_blk[item_e] + w - item_start[item_e]
    return gstart, gend, item_e, item_blk, total.reshape(1)


def _layer(x, positions, norm1_g, w_in, q_norm_g, k_norm_g, idx_k_norm_g, idx_k_norm_b, w_branch_attn,
           pool_mix_w, pool_scale, w_branch_pool, w_out, norm2_g, router_w, router_b, w_gate_up,
           b_gate_up, w_down, b_down):
    b, s, d = x.shape
    t = b * s
    assert (t * TOP_K) % ROW_BLOCK == 0 and s % QUERY_BLOCK == 0
    lane = np.arange(LANES)
    gsum = jnp.asarray(lane[:, None] // HEAD_DIM == lane[None, :] // HEAD_DIM, BF16)

    qt, k, vt, qit, ki, wit, pool, gates = _proj_call(
        x, positions.reshape(b, s, 1), norm1_g.reshape(1, d), _pack_w_in(w_in),
        _const_rows(q_norm_g, k_norm_g, idx_k_norm_g, idx_k_norm_b), gsum)
    attn = _attn_call(qt, k, vt, qit, ki, wit)
    rw = jnp.pad(router_w, ((0, 0), (0, LANES - N_EXPERTS))).astype(BF16)
    rb = jnp.pad(router_b, (0, LANES - N_EXPERTS)).reshape(1, LANES)
    x1, h2, meta, rgate, counts = _merge_call(
        x, attn, pool, gates, w_branch_attn.astype(BF16), pool_mix_w.astype(BF16),
        pool_scale.reshape(1, POOL_WIDTH), w_branch_pool.astype(BF16), w_out.astype(BF16),
        norm2_g.reshape(1, d), rw, rb)
    x1 = x1.reshape(t, d)

    gstart, gend, item_e, item_blk, nact = _ffn_schedule(counts[0, :N_EXPERTS], t * TOP_K // ROW_BLOCK)
    seg = jnp.sum(jnp.where(meta[None, 0:TOP_K] == jnp.arange(N_EXPERTS, dtype=I32)[:, None, None],
                            gstart[:, None, None], 0), axis=0)
    slots = (seg + meta[TOP_K:2 * TOP_K]).reshape(TOP_K, t // ROW_DMA_TOKENS, ROW_DMA_TOKENS)
    slots = slots.transpose(1, 0, 2).reshape(t * TOP_K)
    xs = _dispatch_call(slots, h2)
    ys = _ffn_call(item_e, item_blk, nact, gstart, gend, xs, w_gate_up,
                   b_gate_up.reshape(N_EXPERTS, 1, 2 * EXPERT_DIM), w_down, b_down.reshape(N_EXPERTS, 1, d))
    out = _combine_call(slots, ys, rgate, x1)
    return out.reshape(b, s, d)


def kernel(x, positions, norm1_g, w_in, q_norm_g, k_norm_g, idx_k_norm_g, idx_k_norm_b, w_branch_attn,
           pool_mix_w, pool_scale, w_branch_pool, w_out, norm2_g, router_w, router_b, w_gate_up,
           b_gate_up, w_down, b_down):
    for l in range(norm1_g.shape[0]):
        x = _layer(x, positions, norm1_g[l], w_in[l], q_norm_g[l], k_norm_g[l], idx_k_norm_g[l],
                   idx_k_norm_b[l], w_branch_attn[l], pool_mix_w[l], pool_scale[l], w_branch_pool[l],
                   w_out[l], norm2_g[l], router_w[l], router_b[l], w_gate_up[l], b_gate_up[l],
                   w_down[l], b_down[l])
    return x
```

```python
import functools

import numpy as np
import jax
import jax.numpy as jnp
from jax import lax
from jax.experimental import pallas as pl
from jax.experimental.pallas import tpu as pltpu

F32 = jnp.float32
BF16 = jnp.bfloat16
I32 = jnp.int32

D_MODEL = 1024
CHUNK = 64
CHUNK_SHIFT = CHUNK.bit_length() - 1
ATTN_HEADS = 8
ATTN_KV_HEADS = 2
HEAD_DIM = 64
GROUPS = ATTN_HEADS // ATTN_KV_HEADS
ATTN_WIDTH = ATTN_HEADS * HEAD_DIM
KV_WIDTH = ATTN_KV_HEADS * HEAD_DIM
ATTN_SCALE = HEAD_DIM ** -0.5
IDX_HEADS = 4
IDX_DIM = 64
IDX_ROPE_DIM = 32
IDX_SCALE = (IDX_HEADS ** -0.5) * (IDX_DIM ** -0.5)
IDX_TOPK_MAX = 256
QUERY_BLOCK = 256
POOL_WINDOWS = (2, 4, 8, 16)
POOL_WIDTH = 512
POOL_GROUP_DIM = 128
POOL_HALO = 16
N_EXPERTS = 32
TOP_K = 4
EXPERT_DIM = 1024
SWIGLU_ALPHA = 1.702
SWIGLU_LIMIT = 7.0
ROPE_THETA = 10000.0
NORM_EPS = 1e-6

LANES = 128
SUBLANES = 8
VMEM_LIMIT = 56 * 1024 * 1024
FLT_MAX = float(np.finfo(np.float32).max)
MASKED = -1e30

C_Q = 0
C_K = 512
C_V = 640
C_QI = 768
C_KIW = 1024
C_POOL = 1152
C_GATE = 1664
W_PACKED = 3712
C_SMALL_END = C_POOL

ROW_BLOCK = 512
FFN_SPLIT = 1
FFN_COLS = EXPERT_DIM // FFN_SPLIT
FOLD_ROWS = 4 * SUBLANES
VALUE_BISECT_ITERS = 16
BISECT_CAP = 64
BISECT_HEAD = 16
BISECT_UNROLL = 2
ATT_CK = 256
TIE_ROWS = 512
VT_ROWS = HEAD_DIM + 2 * SUBLANES
LOG2E = float(np.log2(np.e))
ROPE_ROWS = 64


def _cparams(*sem):
    return pltpu.CompilerParams(dimension_semantics=sem, vmem_limit_bytes=VMEM_LIMIT)


def _proj_kernel(x_ref, pos_ref, g1_ref, w_ref, rows_ref, freq_ref, gsum_ref,
                 qt_ref, k_ref, vt_ref, qit_ref, ki_ref, wit_ref, pool_ref, gate_ref):
    x = x_ref[0]
    tm = x.shape[0]
    ms = jnp.mean(x * x, axis=-1, keepdims=True)
    h = (x * lax.rsqrt(ms + NORM_EPS) * g1_ref[...]).astype(BF16)
    d1 = jnp.dot(h, w_ref[:, 0:C_SMALL_END], preferred_element_type=F32)
    pool_ref[0] = jnp.dot(h, w_ref[:, C_POOL:C_GATE], preferred_element_type=F32)
    gate_ref[0] = jnp.dot(h, w_ref[:, C_GATE:W_PACKED], preferred_element_type=F32)

    half_m, half_i = HEAD_DIM // 2, IDX_ROPE_DIM // 2
    ang = freq_ref[...] * pos_ref[0].astype(F32)
    cos_t, sin_t = jnp.cos(ang), jnp.sin(ang)
    cm, sm = cos_t[0:half_m], sin_t[0:half_m]
    ci, si = cos_t[half_m:half_m + half_i], sin_t[half_m:half_m + half_i]
    rest_one = jnp.ones((IDX_DIM - IDX_ROPE_DIM, tm), F32)
    rest_zero = jnp.zeros((IDX_DIM - IDX_ROPE_DIM, tm), F32)
    heads = LANES // HEAD_DIM
    cos_m = jnp.concatenate([cm, cm] * heads, axis=0)
    sin_m = jnp.concatenate([-sm, sm] * heads, axis=0)
    cos_i = jnp.concatenate([ci, ci, rest_one] * heads, axis=0)
    sin_i = jnp.concatenate([-si, si, rest_zero] * heads, axis=0)
    rows = rows_ref[...]
    gsum = gsum_ref[...]

    def head_rms(xc, grow):
        sq = xc * xc
        hi = sq.astype(BF16)
        lo = (sq - hi.astype(F32)).astype(BF16)
        ssum = (jnp.dot(hi, gsum, preferred_element_type=F32)
                + jnp.dot(lo, gsum, preferred_element_type=F32))
        return xc * lax.rsqrt(ssum * (1.0 / HEAD_DIM) + NORM_EPS) * grow

    def rope_t(xt, cos_tab, sin_tab, half, width):
        parts = []
        for h0 in range(0, LANES, width):
            parts += [xt[h0 + half:h0 + 2 * half], xt[h0:h0 + half]]
            if 2 * half < width:
                parts.append(xt[h0 + 2 * half:h0 + width])
        return xt * cos_tab + jnp.concatenate(parts, axis=0) * sin_tab

    for c in range(ATTN_WIDTH // LANES):
        qc = d1[:, C_Q + c * LANES:C_Q + (c + 1) * LANES]
        qr = rope_t(head_rms(qc, rows[0:1]).T, cos_m, sin_m, half_m, HEAD_DIM) * (ATTN_SCALE * LOG2E)
        qt_ref[0, c * LANES:(c + 1) * LANES, :] = qr.astype(BF16)

    kr = rope_t(head_rms(d1[:, C_K:C_K + KV_WIDTH], rows[1:2]).T, cos_m, sin_m, half_m, HEAD_DIM).T
    vt = d1[:, C_V:C_V + KV_WIDTH].T
    for r in range(ATTN_KV_HEADS):
        k_ref[0, r] = kr[:, r * HEAD_DIM:(r + 1) * HEAD_DIM].astype(BF16)
        vt_ref[0, r, 0:HEAD_DIM, :] = vt[r * HEAD_DIM:(r + 1) * HEAD_DIM].astype(BF16)
        vt_ref[0, r, HEAD_DIM:VT_ROWS, :] = jnp.ones((VT_ROWS - HEAD_DIM, tm), BF16)

    for c in range(IDX_HEADS * IDX_DIM // LANES):
        qc = d1[:, C_QI + c * LANES:C_QI + (c + 1) * LANES]
        qit_ref[0, c * LANES:(c + 1) * LANES, :] = rope_t(qc.T, cos_i, sin_i, half_i, IDX_DIM).astype(BF16)

    kiw = d1[:, C_KIW:C_KIW + LANES]
    in_ki = lax.broadcasted_iota(I32, (tm, LANES), 1) < IDX_DIM
    mu = jnp.sum(jnp.where(in_ki, kiw, 0.0), axis=-1, keepdims=True) * (1.0 / IDX_DIM)
    dv = jnp.where(in_ki, kiw - mu, 0.0)
    var = jnp.sum(dv * dv, axis=-1, keepdims=True) * (1.0 / IDX_DIM)
    kin = dv * lax.rsqrt(var + NORM_EPS) * rows[2:3] + rows[3:4]
    ki_ref[0] = rope_t(kin.T, cos_i, sin_i, half_i, IDX_DIM).T[:, 0:IDX_DIM].astype(BF16)
    wit_ref[0] = (kiw * IDX_SCALE).T[IDX_DIM:IDX_DIM + SUBLANES]


def _proj_call(x, pos_rows, g1, w_packed, rows, freq, gsum):
    b, s, d = x.shape
    tm = min(512, s)
    grid = (b, s // tm)
    full2 = lambda i, j: (0, 0)
    out_shape = (
        jax.ShapeDtypeStruct((b, ATTN_WIDTH, s), BF16),
        jax.ShapeDtypeStruct((b, ATTN_KV_HEADS, s, HEAD_DIM), BF16),
        jax.ShapeDtypeStruct((b, ATTN_KV_HEADS, VT_ROWS, s), BF16),
        jax.ShapeDtypeStruct((b, IDX_HEADS * IDX_DIM, s), BF16),
        jax.ShapeDtypeStruct((b, s, IDX_DIM), BF16),
        jax.ShapeDtypeStruct((b, SUBLANES, s), F32),
        jax.ShapeDtypeStruct((b, s, POOL_WIDTH), F32),
        jax.ShapeDtypeStruct((b, s, 2 * D_MODEL), F32),
    )
    in_specs = [
        pl.BlockSpec((1, tm, d), lambda i, j: (i, j, 0)),
        pl.BlockSpec((1, 1, tm), lambda i, j: (i, 0, j)),
        pl.BlockSpec((1, d), full2),
        pl.BlockSpec((d, W_PACKED), full2),
        pl.BlockSpec((SUBLANES, LANES), full2),
        pl.BlockSpec((ROPE_ROWS, 1), full2),
        pl.BlockSpec((LANES, LANES), full2),
    ]
    out_specs = (
        pl.BlockSpec((1, ATTN_WIDTH, tm), lambda i, j: (i, 0, j)),
        pl.BlockSpec((1, ATTN_KV_HEADS, tm, HEAD_DIM), lambda i, j: (i, 0, j, 0)),
        pl.BlockSpec((1, ATTN_KV_HEADS, VT_ROWS, tm), lambda i, j: (i, 0, 0, j)),
        pl.BlockSpec((1, IDX_HEADS * IDX_DIM, tm), lambda i, j: (i, 0, j)),
        pl.BlockSpec((1, tm, IDX_DIM), lambda i, j: (i, j, 0)),
        pl.BlockSpec((1, SUBLANES, tm), lambda i, j: (i, 0, j)),
        pl.BlockSpec((1, tm, POOL_WIDTH), lambda i, j: (i, j, 0)),
        pl.BlockSpec((1, tm, 2 * D_MODEL), lambda i, j: (i, j, 0)),
    )
    return pl.pallas_call(
        _proj_kernel, grid=grid, in_specs=in_specs, out_specs=out_specs, out_shape=out_shape,
        compiler_params=_cparams("parallel", "parallel"), name="proj",
    )(x, pos_rows, g1, w_packed, rows, freq, gsum)


def _sort_key(x):
    b = lax.bitcast_convert_type(x, I32)
    return b ^ ((b >> 31) & 0x7FFFFFFF)


def _unsort_key(k):
    return lax.bitcast_convert_type(k ^ ((k >> 31) & 0x7FFFFFFF), F32)


def _attn_kernel(qt_ref, qit_ref, wit_ref, ki_ref, k_ref, vt_ref, tri_ref, o_ref, sc_ref, sa_ref, sb_ref,
                 *, n_sel, ck):
    qb = QUERY_BLOCK
    t0 = pl.program_id(1) * qb
    nck = (t0 + qb + ck - 1) // ck
    qpos = t0 + lax.broadcasted_iota(I32, (1, qb), 1)
    cend = ((qpos >> CHUNK_SHIFT) + 1) * CHUNK
    key_ck = lax.broadcasted_iota(I32, (ck, qb), 0)
    k_sel = float(n_sel)
    cnt_rows = min(ck, 64 * SUBLANES * LANES // qb)

    def fold(x, op):
        return op(x.reshape(ck // FOLD_ROWS, FOLD_ROWS, qb), axis=0)

    qit = qit_ref[0]
    rhs_i = jnp.concatenate([qit[h * IDX_DIM:(h + 1) * IDX_DIM] for h in range(IDX_HEADS)], axis=1)
    wit = wit_ref[0]

    def score_body(c, carry):
        mxp, mnp, c0p, c1p = carry
        off = pl.multiple_of(c * ck, ck)
        lg = jnp.dot(ki_ref[0, pl.ds(off, ck), :], rhs_i, preferred_element_type=F32)
        sc = jnp.maximum(lg[:, 0:qb], 0.0) * wit[0:1]
        for h in range(1, IDX_HEADS):
            sc = sc + jnp.maximum(lg[:, h * qb:(h + 1) * qb], 0.0) * wit[h:h + 1]
        adm = (off + key_ck) < cend
        sc = jnp.where(adm, sc, -jnp.inf)
        sc_ref[pl.ds(off, ck), :] = sc
        mxp = jnp.maximum(mxp, fold(sc, jnp.max))
        mnp = jnp.minimum(mnp, fold(jnp.where(adm, sc, jnp.inf), jnp.min))
        c0p = c0p + fold(jnp.where(sc >= 0.0, 1.0, 0.0), jnp.sum)
        c1p = c1p + fold(jnp.where(sc > 0.0, 1.0, 0.0), jnp.sum)
        return mxp, mnp, c0p, c1p

    init = (jnp.full((FOLD_ROWS, qb), -jnp.inf, F32), jnp.full((FOLD_ROWS, qb), jnp.inf, F32),
            jnp.zeros((FOLD_ROWS, qb), F32), jnp.zeros((FOLD_ROWS, qb), F32))
    mxp, mnp, c0p, c1p = lax.fori_loop(0, nck, score_body, init)
    mx = jnp.max(mxp, axis=0, keepdims=True)
    mn = jnp.min(mnp, axis=0, keepdims=True)
    c0 = jnp.sum(c0p, axis=0, keepdims=True)
    c1 = jnp.sum(c1p, axis=0, keepdims=True)

    def count(pred):
        def body(c, acc):
            for sb in range(ck // cnt_rows):
                off = pl.multiple_of(c * ck + sb * cnt_rows, cnt_rows)
                hit = pred(sc_ref[pl.ds(off, cnt_rows), :], off + key_ck[0:cnt_rows])
                acc = acc + jnp.sum(jnp.where(hit, 1.0, 0.0).reshape(cnt_rows // FOLD_ROWS, FOLD_ROWS, qb),
                                    axis=0)
            return acc
        acc = lax.fori_loop(0, nck, body, jnp.zeros((FOLD_ROWS, qb), F32))
        return jnp.sum(acc, axis=0, keepdims=True)

    small = cend.astype(F32) <= k_sel
    at_zero = jnp.logical_and(jnp.logical_not(small), jnp.logical_and(c1 < k_sel, c0 >= k_sel))
    positive = jnp.logical_and(jnp.logical_not(small), c1 >= k_sel)
    lo0 = jnp.where(positive, 0.0, mn)
    hi0 = jnp.where(positive, jnp.minimum(2.0 * mx, FLT_MAX), 0.0)
    thr0 = jnp.where(small, -FLT_MAX, 0.0)
    done0 = jnp.where(jnp.logical_or(small, at_zero), 1.0, 0.0)
    tie0 = jnp.where(jnp.logical_and(at_zero, c0 > k_sel), 1.0, 0.0)
    chi0 = jnp.where(positive, 0.0, c0)
    left0 = jnp.sum(1.0 - done0)

    def bisect_cond(st):
        it, left = st[0], st[1]
        return jnp.logical_and(it < BISECT_CAP, left > 0.0)

    def bisect_step(it, lo, hi, thr, done, tie, chi):
        mid_v = 0.5 * lo + 0.5 * hi
        klo = _sort_key(lo)
        khi = _sort_key(hi)
        mid_k = _unsort_key((klo & khi) + ((klo ^ khi) >> 1))
        mid = jnp.where(it < VALUE_BISECT_ITERS, mid_v, mid_k)
        stuck = jnp.logical_or(mid <= lo, mid >= hi)
        cnt = count(lambda s, _: s >= mid)
        active = done == 0.0
        moving = jnp.logical_and(active, jnp.logical_not(stuck))
        hit = jnp.logical_and(moving, cnt == k_sel)
        new_tie = jnp.logical_and(active, stuck)
        thr = jnp.where(hit, mid, jnp.where(new_tie, lo, thr))
        tie = jnp.where(new_tie, 1.0, tie)
        done = jnp.where(jnp.logical_or(hit, new_tie), 1.0, done)
        upd = jnp.logical_and(moving, jnp.logical_not(hit))
        lo = jnp.where(jnp.logical_and(upd, cnt >= k_sel), mid, lo)
        lower = jnp.logical_and(upd, cnt < k_sel)
        hi = jnp.where(lower, mid, hi)
        chi = jnp.where(lower, cnt, chi)
        return lo, hi, thr, done, tie, chi

    def bisect_body(steps, st):
        it, _, lo, hi, thr, done, tie, chi = st
        for u in range(steps):
            lo, hi, thr, done, tie, chi = bisect_step(it + u, lo, hi, thr, done, tie, chi)
        return it + steps, jnp.sum(1.0 - done), lo, hi, thr, done, tie, chi

    st = (jnp.int32(0), left0, lo0, hi0, thr0, done0, tie0, chi0)
    st = lax.while_loop(lambda s: jnp.logical_and(s[0] < BISECT_HEAD, s[1] > 0.0),
                        functools.partial(bisect_body, BISECT_HEAD), st)
    st = lax.while_loop(bisect_cond, functools.partial(bisect_body, BISECT_UNROLL), st)
    thr, tie = st[4], st[6]
    n_above = jnp.where(at_zero, c1, st[7])

    @pl.when(jnp.sum(tie) > 0.0)
    def _():
        need = jnp.where(tie > 0.0, k_sel - n_above, FLT_MAX)

        def drop_body(c, before):
            tr = tri_ref.shape[0]
            for sb in range(ck // tr):
                off = pl.multiple_of(c * ck + sb * tr, tr)
                blk = sc_ref[pl.ds(off, tr), :]
                eq = blk == thr
                eqf = jnp.where(eq, 1.0, 0.0)
                rank = before + jnp.dot(tri_ref[...], eqf.astype(BF16), preferred_element_type=F32)
                sc_ref[pl.ds(off, tr), :] = jnp.where(jnp.logical_and(eq, rank >= need), -jnp.inf, blk)
                before = before + jnp.sum(jnp.sum(eqf.reshape(tr // FOLD_ROWS, FOLD_ROWS, qb), axis=0),
                                          axis=0, keepdims=True)
            return before

        lax.fori_loop(0, nck, drop_body, jnp.zeros((1, qb), F32))

    qt = qt_ref[0]
    rhs = [jnp.concatenate(
        [qt[(r * GROUPS + g) * HEAD_DIM:(r * GROUPS + g + 1) * HEAD_DIM] for g in range(GROUPS)],
        axis=1) for r in range(ATTN_KV_HEADS)]

    n_att = (t0 + qb + ATT_CK - 1) // ATT_CK

    def step_offset(step):
        return pl.multiple_of(jnp.minimum(step, n_att - 1) * ATT_CK, ATT_CK)

    def qk_logits(step, buf):
        off = step_offset(step)
        thr_step = jnp.where(step < n_att, thr, jnp.inf)
        bias = jnp.where(sc_ref[pl.ds(off, ATT_CK), :] >= thr_step, 0.0, MASKED)
        for r in range(ATTN_KV_HEADS):
            lg = jnp.dot(k_ref[0, r, pl.ds(off, ATT_CK), :], rhs[r], preferred_element_type=F32)
            for g in range(GROUPS):
                buf[r, :, g * qb:(g + 1) * qb] = lg[:, g * qb:(g + 1) * qb] + bias

    def softmax_pv(step, buf, carry):
        off = step_offset(step)
        out = []
        for r in range(ATTN_KV_HEADS):
            m, acc = carry[r]
            m_new = jnp.maximum(m, jnp.max(buf[r], axis=0, keepdims=True))
            p = jnp.exp2(buf[r] - m_new).astype(BF16)
            pv = jnp.dot(vt_ref[0, r, :, pl.ds(off, ATT_CK)], p, preferred_element_type=F32)
            out.append((m_new, jnp.exp2(m - m_new) * acc + pv))
        return tuple(out)

    def att_body(i, carry):
        qk_logits(2 * i + 1, sb_ref)
        carry = softmax_pv(2 * i, sa_ref, carry)
        qk_logits(2 * i + 2, sa_ref)
        return softmax_pv(2 * i + 1, sb_ref, carry)

    init_a = tuple((jnp.full((1, GROUPS * qb), MASKED, F32), jnp.zeros((VT_ROWS, GROUPS * qb), F32))
                   for _ in range(ATTN_KV_HEADS))
    qk_logits(0, sa_ref)
    fin = lax.fori_loop(0, (n_att + 1) // 2, att_body, init_a)
    outs = []
    for r in range(ATTN_KV_HEADS):
        acc = fin[r][1]
        o = acc[0:HEAD_DIM] / acc[HEAD_DIM:HEAD_DIM + 1]
        outs.extend(o[:, g * qb:(g + 1) * qb] for g in range(GROUPS))
    o_ref[0] = jnp.concatenate(outs, axis=0).T.astype(BF16)


def _attn_call(qt, k, vt, qit, ki, wit):
    b, _, s = qt.shape
    n_sel = min(IDX_TOPK_MAX, s // 4)
    ck = min(512, s)
    grid = (b, s // QUERY_BLOCK)
    in_specs = [
        pl.BlockSpec((1, ATTN_WIDTH, QUERY_BLOCK), lambda i, j: (i, 0, j)),
        pl.BlockSpec((1, IDX_HEADS * IDX_DIM, QUERY_BLOCK), lambda i, j: (i, 0, j)),
        pl.BlockSpec((1, SUBLANES, QUERY_BLOCK), lambda i, j: (i, 0, j)),
        pl.BlockSpec((1, s, IDX_DIM), lambda i, j: (i, 0, 0)),
        pl.BlockSpec((1, ATTN_KV_HEADS, s, HEAD_DIM), lambda i, j: (i, 0, 0, 0)),
        pl.BlockSpec((1, ATTN_KV_HEADS, VT_ROWS, s), lambda i, j: (i, 0, 0, 0)),
        pl.BlockSpec((min(TIE_ROWS, ck), min(TIE_ROWS, ck)), lambda i, j: (0, 0)),
    ]
    tri = jnp.asarray(np.tril(np.ones((min(TIE_ROWS, ck), min(TIE_ROWS, ck)), np.float32), -1), BF16)
    return pl.pallas_call(
        functools.partial(_attn_kernel, n_sel=n_sel, ck=ck),
        grid=grid, in_specs=in_specs,
        out_specs=pl.BlockSpec((1, QUERY_BLOCK, ATTN_WIDTH), lambda i, j: (i, j, 0)),
        out_shape=jax.ShapeDtypeStruct((b, s, ATTN_WIDTH), BF16),
        scratch_shapes=[pltpu.VMEM((s, QUERY_BLOCK), F32),
                        pltpu.VMEM((ATTN_KV_HEADS, ATT_CK, GROUPS * QUERY_BLOCK), F32),
                        pltpu.VMEM((ATTN_KV_HEADS, ATT_CK, GROUPS * QUERY_BLOCK), F32)],
        compiler_params=_cparams("parallel", "parallel"), name="attn",
    )(qt, qit, wit, ki, k, vt, tri)


def _merge_kernel(x_ref, attn_ref, pool_ref, gate_ref, wba_ref, mix_ref, pscale_ref, wbp_ref, wout_ref,
                  g2_ref, rw_ref, rb_ref, tri_ref,
                  o_ref, h_ref, meta_ref, rgate_ref, cnt_ref, ext_ref, carry_ref):
    tm = x_ref.shape[1]
    j = pl.program_id(1)

    @pl.when(j == 0)
    def _():
        ext_ref[0:POOL_HALO] = jnp.zeros((POOL_HALO, POOL_WIDTH), F32)

    @pl.when(j > 0)
    def _():
        ext_ref[0:POOL_HALO] = ext_ref[tm:tm + POOL_HALO]

    ext_ref[POOL_HALO:POOL_HALO + tm] = pool_ref[0]
    t = j * tm + lax.broadcasted_iota(I32, (tm, 1), 0)
    mixed = []
    for g, w in enumerate(POOL_WINDOWS):
        cols = slice(g * POOL_GROUP_DIM, (g + 1) * POOL_GROUP_DIM)
        cur = ext_ref[POOL_HALO:POOL_HALO + tm, cols]
        wsum = cur
        for i in range(1, w):
            wsum = wsum + ext_ref[POOL_HALO - i:POOL_HALO - i + tm, cols]
        cnt = jnp.minimum(t + 1, w).astype(F32)
        dev = (wsum / cnt - cur).astype(BF16)
        mixed.append(jnp.dot(dev, mix_ref[g], preferred_element_type=F32))
    pooled = (jnp.concatenate(mixed, axis=1) * pscale_ref[...]).astype(BF16)
    branch_a = jnp.dot(attn_ref[0], wba_ref[...], preferred_element_type=F32)
    branch_p = jnp.dot(pooled, wbp_ref[...], preferred_element_type=F32)
    gates = gate_ref[0]
    merged = (jax.nn.sigmoid(gates[:, 0:D_MODEL]) * branch_a
              + jax.nn.sigmoid(gates[:, D_MODEL:2 * D_MODEL]) * branch_p)
    x1 = x_ref[0] + jnp.dot(merged.astype(BF16), wout_ref[...], preferred_element_type=F32)
    o_ref[0] = x1
    _route_tile(x1, jnp.logical_and(pl.program_id(0) == 0, j == 0), g2_ref, rw_ref, rb_ref, tri_ref,
                h_ref, meta_ref, rgate_ref, cnt_ref, carry_ref)


def _merge_call(x, attn, pool, gates, wba, mix, pscale, wbp, wout, g2, rw, rb):
    b, s, d = x.shape
    tm = min(256, s)
    nt = s // tm
    t = b * s
    tile = lambda w: pl.BlockSpec((1, tm, w), lambda i, j: (i, j, 0))
    flat = lambda rows, w: pl.BlockSpec((rows, w), lambda i, j: (i * nt + j, 0))
    full2 = lambda i, j: (0, 0)
    tri = jnp.asarray(np.tril(np.ones((tm, tm), np.float32), -1), BF16)
    in_specs = [
        tile(d), tile(ATTN_WIDTH), tile(POOL_WIDTH), tile(2 * D_MODEL),
        pl.BlockSpec((ATTN_WIDTH, d), full2),
        pl.BlockSpec((len(POOL_WINDOWS), POOL_GROUP_DIM, POOL_GROUP_DIM), lambda i, j: (0, 0, 0)),
        pl.BlockSpec((1, POOL_WIDTH), full2),
        pl.BlockSpec((POOL_WIDTH, d), full2),
        pl.BlockSpec((d, d), full2),
        pl.BlockSpec((1, d), full2),
        pl.BlockSpec((d, LANES), full2),
        pl.BlockSpec((1, LANES), full2),
        pl.BlockSpec((tm, tm), full2),
    ]
    out_shape = (
        jax.ShapeDtypeStruct((b, s, d), F32),
        jax.ShapeDtypeStruct((t * ROW_TILES, LANES), F32),
        jax.ShapeDtypeStruct((2 * TOP_K, t), I32),
        jax.ShapeDtypeStruct((t, LANES), F32),
        jax.ShapeDtypeStruct((1, LANES), I32),
    )
    out_specs = (
        tile(d), flat(tm * ROW_TILES, LANES),
        pl.BlockSpec((2 * TOP_K, tm), lambda i, j: (0, i * nt + j)),
        flat(tm, LANES), pl.BlockSpec((1, LANES), full2),
    )
    return pl.pallas_call(
        _merge_kernel, grid=(b, nt), in_specs=in_specs, out_specs=out_specs, out_shape=out_shape,
        scratch_shapes=[pltpu.VMEM((POOL_HALO + tm, POOL_WIDTH), F32), pltpu.VMEM((1, LANES), F32)],
        compiler_params=_cparams("arbitrary", "arbitrary"), name="merge",
    )(x, attn, pool, gates, wba, mix, pscale, wbp, wout, g2, rw, rb, tri)


ROW_TILES = D_MODEL // LANES


def _load_rows(ref, n, *lead):
    return jnp.concatenate(
        [ref[(*lead, pl.ds(c, n, stride=ROW_TILES), slice(None))] for c in range(ROW_TILES)], axis=1)


def _store_rows(ref, val):
    for c in range(ROW_TILES):
        ref[pl.ds(c, val.shape[0], stride=ROW_TILES), :] = val[:, c * LANES:(c + 1) * LANES]


def _row_tile(ref, i):
    return ref.at[pl.ds(pl.multiple_of(i * ROW_TILES, ROW_TILES), ROW_TILES)]


def _route_tile(x, first_tile, g2_ref, rw_ref, rb_ref, tri_ref, h_ref, meta_ref, gate_ref, cnt_ref, carry_ref):
    tm = x.shape[0]

    @pl.when(first_tile)
    def _():
        carry_ref[...] = jnp.zeros((1, LANES), F32)

    ms = jnp.mean(x * x, axis=-1, keepdims=True)
    h = x * lax.rsqrt(ms + NORM_EPS) * g2_ref[...]
    _store_rows(h_ref, h)
    logits = jnp.dot(h.astype(BF16), rw_ref[...], preferred_element_type=F32) + rb_ref[...]
    lane = lax.broadcasted_iota(I32, (tm, LANES), 1).astype(F32)
    work = jnp.where(lane < N_EXPERTS, logits, -jnp.inf)
    vals, idxs = [], []
    for _ in range(TOP_K):
        m = jnp.max(work, axis=-1, keepdims=True)
        idx = jnp.min(jnp.where(work == m, lane, float(LANES)), axis=-1, keepdims=True)
        vals.append(m)
        idxs.append(idx)
        work = jnp.where(lane == idx, -jnp.inf, work)
    exps = [jnp.exp(v - vals[0]) for v in vals]
    denom = exps[0] + exps[1] + exps[2] + exps[3]
    member = jnp.zeros((tm, LANES), F32)
    for idx in idxs:
        member = member + jnp.where(lane == idx, 1.0, 0.0)
    before = jnp.dot(tri_ref[...], member.astype(BF16), preferred_element_type=F32) + carry_ref[...]
    meta = jnp.zeros((tm, LANES), F32)
    gate = jnp.zeros((tm, LANES), F32)
    for k in range(TOP_K):
        rank = jnp.sum(jnp.where(lane == idxs[k], before, 0.0), axis=-1, keepdims=True)
        meta = jnp.where(lane == float(k), idxs[k], meta)
        meta = jnp.where(lane == float(TOP_K + k), rank, meta)
        gate = jnp.where(lane == float(k), exps[k] / denom, gate)
    meta_ref[...] = meta.T[0:2 * TOP_K].astype(I32)
    gate_ref[...] = gate
    total = carry_ref[...] + jnp.sum(member, axis=0, keepdims=True)
    carry_ref[...] = total
    cnt_ref[...] = total.astype(I32)


ROW_UNROLL = SUBLANES
ROW_DMA_TOKENS = 256


def _dispatch_kernel(slot_ref, h_ref, xs_ref, sem, *, tm):
    def issue(i, carry):
        for u in range(ROW_UNROLL):
            j = i * ROW_UNROLL + u
            for k in range(TOP_K):
                pltpu.make_async_copy(_row_tile(h_ref, j), _row_tile(xs_ref, slot_ref[k * tm + j]),
                                      sem).start(priority=k % 2)
        return carry

    lax.fori_loop(0, tm // ROW_UNROLL, issue, 0)

    def drain(j, carry):
        for k in range(TOP_K):
            pltpu.make_async_copy(_row_tile(h_ref, 0), _row_tile(xs_ref, 0), sem).wait()
        return carry

    lax.fori_loop(0, tm, drain, 0)


def _dispatch_call(slots, h2):
    t = h2.shape[0] // ROW_TILES
    tm = ROW_DMA_TOKENS
    return pl.pallas_call(
        functools.partial(_dispatch_kernel, tm=tm),
        grid=(t // tm,),
        in_specs=[pl.BlockSpec((tm * TOP_K,), lambda i: (i,), memory_space=pltpu.SMEM),
                  pl.BlockSpec((tm * ROW_TILES, LANES), lambda i: (i, 0))],
        out_specs=pl.BlockSpec(memory_space=pl.ANY),
        scratch_shapes=[pltpu.SemaphoreType.DMA(())],
        out_shape=jax.ShapeDtypeStruct((t * TOP_K * ROW_TILES, LANES), F32),
        compiler_params=_cparams("arbitrary"), name="dispatch",
    )(slots, h2)


def _ffn_kernel(item_e_ref, item_blk_ref, nact_ref, gstart_ref, gend_ref,
                xs_ref, wgu_ref, bgu_ref, wd_ref, bd_ref, o_ref, wgu_bf, wd_bf):
    w = pl.program_id(0)
    e = item_e_ref[w]
    blk = item_blk_ref[w]
    prev = jnp.maximum(w - 1, 0)
    active = w < nact_ref[0]
    new_expert = jnp.logical_or(w == 0, e != item_e_ref[prev])
    first_visit = jnp.logical_or(w == 0, blk != item_blk_ref[prev])

    @pl.when(jnp.logical_and(active, new_expert))
    def _():
        wgu_bf[...] = wgu_ref[0].astype(BF16)
        wd_bf[...] = wd_ref[0].astype(BF16)

    @pl.when(active)
    def _():
        xb = _load_rows(xs_ref, ROW_BLOCK).astype(BF16)
        res = bd_ref[0]
        for c in range(FFN_SPLIT):
            g0, g1 = c * FFN_COLS, (c + 1) * FFN_COLS
            gate = jnp.dot(xb, wgu_bf[:, g0:g1], preferred_element_type=F32) + bgu_ref[0, :, g0:g1]
            up = (jnp.dot(xb, wgu_bf[:, EXPERT_DIM + g0:EXPERT_DIM + g1], preferred_element_type=F32)
                  + bgu_ref[0, :, EXPERT_DIM + g0:EXPERT_DIM + g1])
            gate = jnp.minimum(gate, SWIGLU_LIMIT)
            up = jnp.clip(up, -SWIGLU_LIMIT, SWIGLU_LIMIT)
            act = gate * jax.nn.sigmoid(SWIGLU_ALPHA * gate) * (up + 1.0)
            res = res + jnp.dot(act.astype(BF16), wd_bf[g0:g1, :], preferred_element_type=F32)
        row = blk * ROW_BLOCK + lax.broadcasted_iota(I32, (ROW_BLOCK, 1), 0)
        mine = jnp.logical_and(row >= gstart_ref[e], row < gend_ref[e])

        @pl.when(first_visit)
        def _():
            _store_rows(o_ref, jnp.where(mine, res, 0.0))

        @pl.when(jnp.logical_not(first_visit))
        def _():
            _store_rows(o_ref, jnp.where(mine, res, _load_rows(o_ref, ROW_BLOCK)))


def _ffn_call(item_e, item_blk, nact, gstart, gend, xs, wgu, bgu, wd, bd):
    d = D_MODEL
    rows = lambda w, ie, ib, *_: (ib[w], 0)
    exp3 = lambda w, ie, *_: (ie[w], 0, 0)
    return pl.pallas_call(
        _ffn_kernel,
        grid_spec=pltpu.PrefetchScalarGridSpec(
            num_scalar_prefetch=5, grid=(item_e.shape[0],),
            in_specs=[pl.BlockSpec((ROW_BLOCK * ROW_TILES, LANES), rows),
                      pl.BlockSpec((1, d, 2 * EXPERT_DIM), exp3),
                      pl.BlockSpec((1, 1, 2 * EXPERT_DIM), exp3),
                      pl.BlockSpec((1, EXPERT_DIM, d), exp3),
                      pl.BlockSpec((1, 1, d), exp3)],
            out_specs=pl.BlockSpec((ROW_BLOCK * ROW_TILES, LANES), rows),
            scratch_shapes=[pltpu.VMEM((d, 2 * EXPERT_DIM), BF16), pltpu.VMEM((EXPERT_DIM, d), BF16)]),
        out_shape=jax.ShapeDtypeStruct(xs.shape, F32),
        compiler_params=_cparams("arbitrary"), name="ffn",
    )(item_e, item_blk, nact, gstart, gend, xs, wgu, bgu, wd, bd)


def _combine_kernel(slot_ref, slot_next_ref, ys_ref, gate_ref, x_ref, o_ref, buf_ref, sem, *, tm, n_steps):
    i = pl.program_id(0)
    half = i % 2

    def gather(slots, dst_half):
        def issue(it, carry):
            for u in range(ROW_UNROLL):
                j = it * ROW_UNROLL + u
                for k in range(TOP_K):
                    pltpu.make_async_copy(_row_tile(ys_ref, slots[k * tm + j]),
                                          _row_tile(buf_ref.at[dst_half, k], j),
                                          sem.at[dst_half]).start(priority=k % 2)
            return carry
        lax.fori_loop(0, tm // ROW_UNROLL, issue, 0)

    @pl.when(i == 0)
    def _():
        gather(slot_ref, 0)

    @pl.when(i + 1 < n_steps)
    def _():
        gather(slot_next_ref, 1 - half)

    def drain(j, carry):
        for k in range(TOP_K):
            pltpu.make_async_copy(_row_tile(ys_ref, 0), _row_tile(buf_ref.at[half, k], 0),
                                  sem.at[half]).wait()
        return carry

    lax.fori_loop(0, tm, drain, 0)
    gate = gate_ref[...]
    y = x_ref[...]
    for k in range(TOP_K):
        y = y + gate[:, k:k + 1] * _load_rows(buf_ref, tm, half, k)
    o_ref[...] = y


def _combine_call(slots, ys, gates, x1):
    t, d = x1.shape
    tm = ROW_DMA_TOKENS
    n_steps = t // tm
    tile = lambda w: pl.BlockSpec((tm, w), lambda i: (i, 0))
    return pl.pallas_call(
        functools.partial(_combine_kernel, tm=tm, n_steps=n_steps),
        grid=(n_steps,),
        in_specs=[pl.BlockSpec((tm * TOP_K,), lambda i: (i,), memory_space=pltpu.SMEM),
                  pl.BlockSpec((tm * TOP_K,), lambda i: (jnp.minimum(i + 1, n_steps - 1),),
                               memory_space=pltpu.SMEM),
                  pl.BlockSpec(memory_space=pl.ANY), tile(LANES), tile(d)],
        out_specs=tile(d),
        scratch_shapes=[pltpu.VMEM((2, TOP_K, tm * ROW_TILES, LANES), F32), pltpu.SemaphoreType.DMA((2,))],
        out_shape=jax.ShapeDtypeStruct((t, d), F32),
        compiler_params=_cparams("arbitrary"), name="combine",
    )(slots, slots, ys, gates, x1)


def _pack_w_in(w):
    d = w.shape[0]
    o = np.cumsum([0, ATTN_WIDTH, KV_WIDTH, KV_WIDTH, IDX_HEADS * IDX_DIM, IDX_DIM, IDX_HEADS,
                   POOL_WIDTH, D_MODEL, D_MODEL])
    pad = jnp.zeros((d, LANES - IDX_DIM - IDX_HEADS), w.dtype)
    parts = [w[:, o[0]:o[4]], w[:, o[4]:o[6]], pad, w[:, o[6]:o[9]]]
    return jnp.concatenate(parts, axis=1).astype(BF16)


def _const_rows(q_g, k_g, i_g, i_b):
    zeros = jnp.zeros((LANES - IDX_DIM,), F32)
    rows = [
        jnp.tile(q_g.astype(F32), LANES // HEAD_DIM),
        jnp.tile(k_g.astype(F32), LANES // HEAD_DIM),
        jnp.concatenate([i_g.astype(F32), zeros]),
        jnp.concatenate([i_b.astype(F32), zeros]),
    ]
    return jnp.concatenate([jnp.stack(rows, axis=0), jnp.zeros((SUBLANES - len(rows), LANES), F32)], axis=0)


def _rope_freqs():
    inv_m = ROPE_THETA ** (-jnp.arange(0, HEAD_DIM, 2, dtype=F32) / HEAD_DIM)
    inv_i = ROPE_THETA ** (-jnp.arange(0, IDX_ROPE_DIM, 2, dtype=F32) / IDX_ROPE_DIM)
    pad = jnp.zeros((ROPE_ROWS - inv_m.shape[0] - inv_i.shape[0],), F32)
    return jnp.concatenate([inv_m, inv_i, pad]).reshape(ROPE_ROWS, 1)


def _ffn_schedule(counts, n_blocks):
    gend = jnp.cumsum(counts).astype(I32)
    gstart = gend - counts
    first_blk = gstart // ROW_BLOCK
    last_blk = (jnp.maximum(gend, 1) - 1) // ROW_BLOCK
    n_items = jnp.where(counts > 0, last_blk - first_blk + 1, 0)
    item_end = jnp.cumsum(n_items).astype(I32)
    item_start = item_end - n_items
    total = item_end[-1]
    w = jnp.minimum(jnp.arange(n_blocks + N_EXPERTS - 1, dtype=I32), total - 1)
    item_e = jnp.sum((item_end[None, :] <= w[:, None]).astype(I32), axis=1)
    mine = item_e[:, None] == jnp.arange(N_EXPERTS, dtype=I32)[None, :]
    item_blk = w + jnp.sum(jnp.where(mine, (first_blk - item_start)[None, :], 0), axis=1)
    return gstart, gend, item_e, item_blk, total.reshape(1)


def _layer(x, positions, norm1_g, w_in, q_norm_g, k_norm_g, idx_k_norm_g, idx_k_norm_b, w_branch_attn,
           pool_mix_w, pool_scale, w_branch_pool, w_out, norm2_g, router_w, router_b, w_gate_up,
           b_gate_up, w_down, b_down):
    b, s, d = x.shape
    t = b * s
    assert (t * TOP_K) % ROW_BLOCK == 0 and s % QUERY_BLOCK == 0
    lane = np.arange(LANES)
    gsum = jnp.asarray(lane[:, None] // HEAD_DIM == lane[None, :] // HEAD_DIM, BF16)

    qt, k, vt, qit, ki, wit, pool, gates = _proj_call(
        x, positions.reshape(b, 1, s), norm1_g.reshape(1, d), _pack_w_in(w_in),
        _const_rows(q_norm_g, k_norm_g, idx_k_norm_g, idx_k_norm_b), _rope_freqs(), gsum)
    attn = _attn_call(qt, k, vt, qit, ki, wit)
    rw = jnp.pad(router_w, ((0, 0), (0, LANES - N_EXPERTS))).astype(BF16)
    rb = jnp.pad(router_b, (0, LANES - N_EXPERTS)).reshape(1, LANES)
    x1, h2, meta, rgate, counts = _merge_call(
        x, attn, pool, gates, w_branch_attn.astype(BF16), pool_mix_w.astype(BF16),
        pool_scale.reshape(1, POOL_WIDTH), w_branch_pool.astype(BF16), w_out.astype(BF16),
        norm2_g.reshape(1, d), rw, rb)
    x1 = x1.reshape(t, d)

    gstart, gend, item_e, item_blk, nact = _ffn_schedule(counts[0, :N_EXPERTS], t * TOP_K // ROW_BLOCK)
    seg = jnp.sum(jnp.where(meta[None, 0:TOP_K] == jnp.arange(N_EXPERTS, dtype=I32)[:, None, None],
                            gstart[:, None, None], 0), axis=0)
    slots = (seg + meta[TOP_K:2 * TOP_K]).reshape(TOP_K, t // ROW_DMA_TOKENS, ROW_DMA_TOKENS)
    slots = slots.transpose(1, 0, 2).reshape(t * TOP_K)
    xs = _dispatch_call(slots, h2)
    ys = _ffn_call(item_e, item_blk, nact, gstart, gend, xs, w_gate_up,
                   b_gate_up.reshape(N_EXPERTS, 1, 2 * EXPERT_DIM), w_down, b_down.reshape(N_EXPERTS, 1, d))
    out = _combine_call(slots, ys, rgate, x1)
    return out.reshape(b, s, d)


def kernel(x, positions, norm1_g, w_in, q_norm_g, k_norm_g, idx_k_norm_g, idx_k_norm_b, w_branch_attn,
           pool_mix_w, pool_scale, w_branch_pool, w_out, norm2_g, router_w, router_b, w_gate_up,
           b_gate_up, w_down, b_down):
    for l in range(norm1_g.shape[0]):
        x = _layer(x, positions, norm1_g[l], w_in[l], q_norm_g[l], k_norm_g[l], idx_k_norm_g[l],
                   idx_k_norm_b[l], w_branch_attn[l], pool_mix_w[l], pool_scale[l], w_branch_pool[l],
                   w_out[l], norm2_g[l], router_w[l], router_b[l], w_gate_up[l], b_gate_up[l],
                   w_down[l], b_down[l])
    return x
```

```python
import functools

import numpy as np
import jax
import jax.numpy as jnp
from jax import lax
from jax.experimental import pallas as pl
from jax.experimental.pallas import tpu as pltpu

F32 = jnp.float32
BF16 = jnp.bfloat16
I32 = jnp.int32

D_MODEL = 1024
CHUNK = 64
CHUNK_SHIFT = CHUNK.bit_length() - 1
ATTN_HEADS = 8
ATTN_KV_HEADS = 2
HEAD_DIM = 64
GROUPS = ATTN_HEADS // ATTN_KV_HEADS
ATTN_WIDTH = ATTN_HEADS * HEAD_DIM
KV_WIDTH = ATTN_KV_HEADS * HEAD_DIM
ATTN_SCALE = HEAD_DIM ** -0.5
IDX_HEADS = 4
IDX_DIM = 64
IDX_ROPE_DIM = 32
IDX_SCALE = (IDX_HEADS ** -0.5) * (IDX_DIM ** -0.5)
IDX_TOPK_MAX = 256
QUERY_BLOCK = 256
POOL_WINDOWS = (2, 4, 8, 16)
POOL_WIDTH = 512
POOL_GROUP_DIM = 128
POOL_HALO = 16
N_EXPERTS = 32
TOP_K = 4
EXPERT_DIM = 1024
SWIGLU_ALPHA = 1.702
SWIGLU_LIMIT = 7.0
ROPE_THETA = 10000.0
NORM_EPS = 1e-6

LANES = 128
SUBLANES = 8
VMEM_LIMIT = 56 * 1024 * 1024
FLT_MAX = float(np.finfo(np.float32).max)
MASKED = -1e30

C_Q = 0
C_K = 512
C_V = 640
C_QI = 768
C_KIW = 1024
C_POOL = 1152
C_GATE = 1664
W_PACKED = 3712
C_SMALL_END = C_POOL
PACK_COLS = 512

ROW_BLOCK = 512
FFN_SPLIT = 1
FFN_COLS = EXPERT_DIM // FFN_SPLIT
FOLD_ROWS = 4 * SUBLANES
VALUE_BISECT_ITERS = 16
BISECT_CAP = 64
BISECT_HEAD = 16
BISECT_UNROLL = 2
ATT_CK = 256
TIE_ROWS = 512
VT_ROWS = HEAD_DIM + 2 * SUBLANES
LOG2E = float(np.log2(np.e))
ROPE_ROWS = 64


def _cparams(*sem):
    return pltpu.CompilerParams(dimension_semantics=sem, vmem_limit_bytes=VMEM_LIMIT)


def _proj_kernel(x_ref, pos_ref, g1_ref, wraw_ref, rows_ref, freq_ref, gsum_ref,
                 qt_ref, k_ref, vt_ref, qit_ref, ki_ref, wit_ref, pool_ref, gate_ref, w_ref):
    @pl.when(jnp.logical_and(pl.program_id(0) == 0, pl.program_id(1) == 0))
    def _():
        narrow = IDX_DIM + IDX_HEADS
        w_ref[:, 0:C_KIW] = wraw_ref[:, 0:C_KIW].astype(BF16)
        w_ref[:, C_KIW:C_POOL] = jnp.concatenate(
            [wraw_ref[:, C_KIW:C_KIW + narrow], jnp.zeros((D_MODEL, LANES - narrow), F32)], axis=1).astype(BF16)
        for c0 in range(C_POOL, W_PACKED, PACK_COLS):
            src = c0 - (LANES - narrow)
            w_ref[:, c0:c0 + PACK_COLS] = wraw_ref[:, src:src + PACK_COLS].astype(BF16)

    x = x_ref[0]
    tm = x.shape[0]
    ms = jnp.mean(x * x, axis=-1, keepdims=True)
    h = (x * lax.rsqrt(ms + NORM_EPS) * g1_ref[...]).astype(BF16)
    d1 = jnp.dot(h, w_ref[:, 0:C_SMALL_END], preferred_element_type=F32)
    pool_ref[0] = jnp.dot(h, w_ref[:, C_POOL:C_GATE], preferred_element_type=F32)
    gate_ref[0] = jnp.dot(h, w_ref[:, C_GATE:W_PACKED], preferred_element_type=F32)

    half_m, half_i = HEAD_DIM // 2, IDX_ROPE_DIM // 2
    ang = freq_ref[...] * pos_ref[0].astype(F32)
    cos_t, sin_t = jnp.cos(ang), jnp.sin(ang)
    cm, sm = cos_t[0:half_m], sin_t[0:half_m]
    ci, si = cos_t[half_m:half_m + half_i], sin_t[half_m:half_m + half_i]
    rest_one = jnp.ones((IDX_DIM - IDX_ROPE_DIM, tm), F32)
    rest_zero = jnp.zeros((IDX_DIM - IDX_ROPE_DIM, tm), F32)
    heads = LANES // HEAD_DIM
    cos_m = jnp.concatenate([cm, cm] * heads, axis=0)
    sin_m = jnp.concatenate([-sm, sm] * heads, axis=0)
    cos_i = jnp.concatenate([ci, ci, rest_one] * heads, axis=0)
    sin_i = jnp.concatenate([-si, si, rest_zero] * heads, axis=0)
    rows = rows_ref[...]
    gsum = gsum_ref[...]

    def head_rms(xc, grow):
        sq = xc * xc
        hi = sq.astype(BF16)
        lo = (sq - hi.astype(F32)).astype(BF16)
        ssum = (jnp.dot(hi, gsum, preferred_element_type=F32)
                + jnp.dot(lo, gsum, preferred_element_type=F32))
        return xc * lax.rsqrt(ssum * (1.0 / HEAD_DIM) + NORM_EPS) * grow

    def rope_t(xt, cos_tab, sin_tab, half, width):
        parts = []
        for h0 in range(0, LANES, width):
            parts += [xt[h0 + half:h0 + 2 * half], xt[h0:h0 + half]]
            if 2 * half < width:
                parts.append(xt[h0 + 2 * half:h0 + width])
        return xt * cos_tab + jnp.concatenate(parts, axis=0) * sin_tab

    for c in range(ATTN_WIDTH // LANES):
        qc = d1[:, C_Q + c * LANES:C_Q + (c + 1) * LANES]
        qr = rope_t(head_rms(qc, rows[0:1]).T, cos_m, sin_m, half_m, HEAD_DIM) * (ATTN_SCALE * LOG2E)
        qt_ref[0, c * LANES:(c + 1) * LANES, :] = qr.astype(BF16)

    kr = rope_t(head_rms(d1[:, C_K:C_K + KV_WIDTH], rows[1:2]).T, cos_m, sin_m, half_m, HEAD_DIM).T
    vt = d1[:, C_V:C_V + KV_WIDTH].T
    for r in range(ATTN_KV_HEADS):
        k_ref[0, r] = kr[:, r * HEAD_DIM:(r + 1) * HEAD_DIM].astype(BF16)
        vt_ref[0, r, 0:HEAD_DIM, :] = vt[r * HEAD_DIM:(r + 1) * HEAD_DIM].astype(BF16)
        vt_ref[0, r, HEAD_DIM:VT_ROWS, :] = jnp.ones((VT_ROWS - HEAD_DIM, tm), BF16)

    for c in range(IDX_HEADS * IDX_DIM // LANES):
        qc = d1[:, C_QI + c * LANES:C_QI + (c + 1) * LANES]
        qit_ref[0, c * LANES:(c + 1) * LANES, :] = rope_t(qc.T, cos_i, sin_i, half_i, IDX_DIM).astype(BF16)

    kiw = d1[:, C_KIW:C_KIW + LANES]
    in_ki = lax.broadcasted_iota(I32, (tm, LANES), 1) < IDX_DIM
    mu = jnp.sum(jnp.where(in_ki, kiw, 0.0), axis=-1, keepdims=True) * (1.0 / IDX_DIM)
    dv = jnp.where(in_ki, kiw - mu, 0.0)
    var = jnp.sum(dv * dv, axis=-1, keepdims=True) * (1.0 / IDX_DIM)
    kin = dv * lax.rsqrt(var + NORM_EPS) * rows[2:3] + rows[3:4]
    ki_ref[0] = rope_t(kin.T, cos_i, sin_i, half_i, IDX_DIM).T[:, 0:IDX_DIM].astype(BF16)
    wit_ref[0] = (kiw * IDX_SCALE).T[IDX_DIM:IDX_DIM + SUBLANES]


def _proj_call(x, pos_rows, g1, w_in, rows, freq, gsum):
    b, s, d = x.shape
    tm = min(512, s)
    grid = (b, s // tm)
    full2 = lambda i, j: (0, 0)
    out_shape = (
        jax.ShapeDtypeStruct((b, ATTN_WIDTH, s), BF16),
        jax.ShapeDtypeStruct((b, ATTN_KV_HEADS, s, HEAD_DIM), BF16),
        jax.ShapeDtypeStruct((b, ATTN_KV_HEADS, VT_ROWS, s), BF16),
        jax.ShapeDtypeStruct((b, IDX_HEADS * IDX_DIM, s), BF16),
        jax.ShapeDtypeStruct((b, s, IDX_DIM), BF16),
        jax.ShapeDtypeStruct((b, SUBLANES, s), F32),
        jax.ShapeDtypeStruct((b, s, POOL_WIDTH), F32),
        jax.ShapeDtypeStruct((b, s, 2 * D_MODEL), F32),
    )
    in_specs = [
        pl.BlockSpec((1, tm, d), lambda i, j: (i, j, 0)),
        pl.BlockSpec((1, 1, tm), lambda i, j: (i, 0, j)),
        pl.BlockSpec((1, d), full2),
        pl.BlockSpec(w_in.shape, full2, pipeline_mode=pl.Buffered(1)),
        pl.BlockSpec((SUBLANES, LANES), full2),
        pl.BlockSpec((ROPE_ROWS, 1), full2),
        pl.BlockSpec((LANES, LANES), full2),
    ]
    out_specs = (
        pl.BlockSpec((1, ATTN_WIDTH, tm), lambda i, j: (i, 0, j)),
        pl.BlockSpec((1, ATTN_KV_HEADS, tm, HEAD_DIM), lambda i, j: (i, 0, j, 0)),
        pl.BlockSpec((1, ATTN_KV_HEADS, VT_ROWS, tm), lambda i, j: (i, 0, 0, j)),
        pl.BlockSpec((1, IDX_HEADS * IDX_DIM, tm), lambda i, j: (i, 0, j)),
        pl.BlockSpec((1, tm, IDX_DIM), lambda i, j: (i, j, 0)),
        pl.BlockSpec((1, SUBLANES, tm), lambda i, j: (i, 0, j)),
        pl.BlockSpec((1, tm, POOL_WIDTH), lambda i, j: (i, j, 0)),
        pl.BlockSpec((1, tm, 2 * D_MODEL), lambda i, j: (i, j, 0)),
    )
    return pl.pallas_call(
        _proj_kernel, grid=grid, in_specs=in_specs, out_specs=out_specs, out_shape=out_shape,
        scratch_shapes=[pltpu.VMEM((d, W_PACKED), BF16)],
        compiler_params=_cparams("arbitrary", "arbitrary"), name="proj",
    )(x, pos_rows, g1, w_in, rows, freq, gsum)


def _sort_key(x):
    b = lax.bitcast_convert_type(x, I32)
    return b ^ ((b >> 31) & 0x7FFFFFFF)


def _unsort_key(k):
    return lax.bitcast_convert_type(k ^ ((k >> 31) & 0x7FFFFFFF), F32)


def _attn_kernel(qt_ref, qit_ref, wit_ref, ki_ref, k_ref, vt_ref, tri_ref, o_ref, sc_ref, sa_ref, sb_ref,
                 *, n_sel, ck):
    qb = QUERY_BLOCK
    t0 = pl.program_id(1) * qb
    nck = (t0 + qb + ck - 1) // ck
    qpos = t0 + lax.broadcasted_iota(I32, (1, qb), 1)
    cend = ((qpos >> CHUNK_SHIFT) + 1) * CHUNK
    key_ck = lax.broadcasted_iota(I32, (ck, qb), 0)
    k_sel = float(n_sel)
    cnt_rows = min(ck, 64 * SUBLANES * LANES // qb)

    def fold(x, op):
        return op(x.reshape(ck // FOLD_ROWS, FOLD_ROWS, qb), axis=0)

    qit = qit_ref[0]
    rhs_i = jnp.concatenate([qit[h * IDX_DIM:(h + 1) * IDX_DIM] for h in range(IDX_HEADS)], axis=1)
    wit = wit_ref[0]

    def score_body(c, carry):
        mxp, mnp, c0p, c1p = carry
        off = pl.multiple_of(c * ck, ck)
        lg = jnp.dot(ki_ref[0, pl.ds(off, ck), :], rhs_i, preferred_element_type=F32)
        sc = jnp.maximum(lg[:, 0:qb], 0.0) * wit[0:1]
        for h in range(1, IDX_HEADS):
            sc = sc + jnp.maximum(lg[:, h * qb:(h + 1) * qb], 0.0) * wit[h:h + 1]
        adm = (off + key_ck) < cend
        sc = jnp.where(adm, sc, -jnp.inf)
        sc_ref[pl.ds(off, ck), :] = sc
        mxp = jnp.maximum(mxp, fold(sc, jnp.max))
        mnp = jnp.minimum(mnp, fold(jnp.where(adm, sc, jnp.inf), jnp.min))
        c0p = c0p + fold(jnp.where(sc >= 0.0, 1.0, 0.0), jnp.sum)
        c1p = c1p + fold(jnp.where(sc > 0.0, 1.0, 0.0), jnp.sum)
        return mxp, mnp, c0p, c1p

    init = (jnp.full((FOLD_ROWS, qb), -jnp.inf, F32), jnp.full((FOLD_ROWS, qb), jnp.inf, F32),
            jnp.zeros((FOLD_ROWS, qb), F32), jnp.zeros((FOLD_ROWS, qb), F32))
    mxp, mnp, c0p, c1p = lax.fori_loop(0, nck, score_body, init)
    mx = jnp.max(mxp, axis=0, keepdims=True)
    mn = jnp.min(mnp, axis=0, keepdims=True)
    c0 = jnp.sum(c0p, axis=0, keepdims=True)
    c1 = jnp.sum(c1p, axis=0, keepdims=True)

    def count(pred):
        def body(c, acc):
            for sb in range(ck // cnt_rows):
                off = pl.multiple_of(c * ck + sb * cnt_rows, cnt_rows)
                hit = pred(sc_ref[pl.ds(off, cnt_rows), :], off + key_ck[0:cnt_rows])
                acc = acc + jnp.sum(jnp.where(hit, 1.0, 0.0).reshape(cnt_rows // FOLD_ROWS, FOLD_ROWS, qb),
                                    axis=0)
            return acc
        acc = lax.fori_loop(0, nck, body, jnp.zeros((FOLD_ROWS, qb), F32))
        return jnp.sum(acc, axis=0, keepdims=True)

    small = cend.astype(F32) <= k_sel
    at_zero = jnp.logical_and(jnp.logical_not(small), jnp.logical_and(c1 < k_sel, c0 >= k_sel))
    positive = jnp.logical_and(jnp.logical_not(small), c1 >= k_sel)
    lo0 = jnp.where(positive, 0.0, mn)
    hi0 = jnp.where(positive, jnp.minimum(2.0 * mx, FLT_MAX), 0.0)
    thr0 = jnp.where(small, -FLT_MAX, 0.0)
    done0 = jnp.where(jnp.logical_or(small, at_zero), 1.0, 0.0)
    tie0 = jnp.where(jnp.logical_and(at_zero, c0 > k_sel), 1.0, 0.0)
    chi0 = jnp.where(positive, 0.0, c0)
    left0 = jnp.sum(1.0 - done0)

    def bisect_cond(st):
        it, left = st[0], st[1]
        return jnp.logical_and(it < BISECT_CAP, left > 0.0)

    def bisect_step(it, lo, hi, thr, done, tie, chi):
        mid_v = 0.5 * lo + 0.5 * hi
        klo = _sort_key(lo)
        khi = _sort_key(hi)
        mid_k = _unsort_key((klo & khi) + ((klo ^ khi) >> 1))
        mid = jnp.where(it < VALUE_BISECT_ITERS, mid_v, mid_k)
        stuck = jnp.logical_or(mid <= lo, mid >= hi)
        cnt = count(lambda s, _: s >= mid)
        active = done == 0.0
        moving = jnp.logical_and(active, jnp.logical_not(stuck))
        hit = jnp.logical_and(moving, cnt == k_sel)
        new_tie = jnp.logical_and(active, stuck)
        thr = jnp.where(hit, mid, jnp.where(new_tie, lo, thr))
        tie = jnp.where(new_tie, 1.0, tie)
        done = jnp.where(jnp.logical_or(hit, new_tie), 1.0, done)
        upd = jnp.logical_and(moving, jnp.logical_not(hit))
        lo = jnp.where(jnp.logical_and(upd, cnt >= k_sel), mid, lo)
        lower = jnp.logical_and(upd, cnt < k_sel)
        hi = jnp.where(lower, mid, hi)
        chi = jnp.where(lower, cnt, chi)
        return lo, hi, thr, done, tie, chi

    def bisect_body(steps, st):
        it, _, lo, hi, thr, done, tie, chi = st
        for u in range(steps):
            lo, hi, thr, done, tie, chi = bisect_step(it + u, lo, hi, thr, done, tie, chi)
        return it + steps, jnp.sum(1.0 - done), lo, hi, thr, done, tie, chi

    st = (jnp.int32(0), left0, lo0, hi0, thr0, done0, tie0, chi0)
    st = lax.while_loop(lambda s: jnp.logical_and(s[0] < BISECT_HEAD, s[1] > 0.0),
                        functools.partial(bisect_body, BISECT_HEAD), st)
    st = lax.while_loop(bisect_cond, functools.partial(bisect_body, BISECT_UNROLL), st)
    thr, tie = st[4], st[6]
    n_above = jnp.where(at_zero, c1, st[7])

    @pl.when(jnp.sum(tie) > 0.0)
    def _():
        need = jnp.where(tie > 0.0, k_sel - n_above, FLT_MAX)

        def drop_body(c, before):
            tr = tri_ref.shape[0]
            for sb in range(ck // tr):
                off = pl.multiple_of(c * ck + sb * tr, tr)
                blk = sc_ref[pl.ds(off, tr), :]
                eq = blk == thr
                eqf = jnp.where(eq, 1.0, 0.0)
                rank = before + jnp.dot(tri_ref[...], eqf.astype(BF16), preferred_element_type=F32)
                sc_ref[pl.ds(off, tr), :] = jnp.where(jnp.logical_and(eq, rank >= need), -jnp.inf, blk)
                before = before + jnp.sum(jnp.sum(eqf.reshape(tr // FOLD_ROWS, FOLD_ROWS, qb), axis=0),
                                          axis=0, keepdims=True)
            return before

        lax.fori_loop(0, nck, drop_body, jnp.zeros((1, qb), F32))

    qt = qt_ref[0]
    rhs = [jnp.concatenate(
        [qt[(r * GROUPS + g) * HEAD_DIM:(r * GROUPS + g + 1) * HEAD_DIM] for g in range(GROUPS)],
        axis=1) for r in range(ATTN_KV_HEADS)]

    n_att = (t0 + qb + ATT_CK - 1) // ATT_CK

    def step_offset(step):
        return pl.multiple_of(jnp.minimum(step, n_att - 1) * ATT_CK, ATT_CK)

    def qk_logits(step, buf):
        off = step_offset(step)
        thr_step = jnp.where(step < n_att, thr, jnp.inf)
        bias = jnp.where(sc_ref[pl.ds(off, ATT_CK), :] >= thr_step, 0.0, MASKED)
        for r in range(ATTN_KV_HEADS):
            lg = jnp.dot(k_ref[0, r, pl.ds(off, ATT_CK), :], rhs[r], preferred_element_type=F32)
            for g in range(GROUPS):
                buf[r, :, g * qb:(g + 1) * qb] = lg[:, g * qb:(g + 1) * qb] + bias

    def softmax_pv(step, buf, carry):
        off = step_offset(step)
        out = []
        for r in range(ATTN_KV_HEADS):
            m, acc = carry[r]
            m_new = jnp.maximum(m, jnp.max(buf[r], axis=0, keepdims=True))
            p = jnp.exp2(buf[r] - m_new).astype(BF16)
            pv = jnp.dot(vt_ref[0, r, :, pl.ds(off, ATT_CK)], p, preferred_element_type=F32)
            out.append((m_new, jnp.exp2(m - m_new) * acc + pv))
        return tuple(out)

    def att_body(i, carry):
        qk_logits(2 * i + 1, sb_ref)
        carry = softmax_pv(2 * i, sa_ref, carry)
        qk_logits(2 * i + 2, sa_ref)
        return softmax_pv(2 * i + 1, sb_ref, carry)

    init_a = tuple((jnp.full((1, GROUPS * qb), MASKED, F32), jnp.zeros((VT_ROWS, GROUPS * qb), F32))
                   for _ in range(ATTN_KV_HEADS))
    qk_logits(0, sa_ref)
    fin = lax.fori_loop(0, (n_att + 1) // 2, att_body, init_a)
    outs = []
    for r in range(ATTN_KV_HEADS):
        acc = fin[r][1]
        o = acc[0:HEAD_DIM] / acc[HEAD_DIM:HEAD_DIM + 1]
        outs.extend(o[:, g * qb:(g + 1) * qb] for g in range(GROUPS))
    o_ref[0] = jnp.concatenate(outs, axis=0).T.astype(BF16)


def _attn_call(qt, k, vt, qit, ki, wit):
    b, _, s = qt.shape
    n_sel = min(IDX_TOPK_MAX, s // 4)
    ck = min(512, s)
    grid = (b, s // QUERY_BLOCK)
    in_specs = [
        pl.BlockSpec((1, ATTN_WIDTH, QUERY_BLOCK), lambda i, j: (i, 0, j)),
        pl.BlockSpec((1, IDX_HEADS * IDX_DIM, QUERY_BLOCK), lambda i, j: (i, 0, j)),
        pl.BlockSpec((1, SUBLANES, QUERY_BLOCK), lambda i, j: (i, 0, j)),
        pl.BlockSpec((1, s, IDX_DIM), lambda i, j: (i, 0, 0)),
        pl.BlockSpec((1, ATTN_KV_HEADS, s, HEAD_DIM), lambda i, j: (i, 0, 0, 0)),
        pl.BlockSpec((1, ATTN_KV_HEADS, VT_ROWS, s), lambda i, j: (i, 0, 0, 0)),
        pl.BlockSpec((min(TIE_ROWS, ck), min(TIE_ROWS, ck)), lambda i, j: (0, 0)),
    ]
    tri = jnp.asarray(np.tril(np.ones((min(TIE_ROWS, ck), min(TIE_ROWS, ck)), np.float32), -1), BF16)
    return pl.pallas_call(
        functools.partial(_attn_kernel, n_sel=n_sel, ck=ck),
        grid=grid, in_specs=in_specs,
        out_specs=pl.BlockSpec((1, QUERY_BLOCK, ATTN_WIDTH), lambda i, j: (i, j, 0)),
        out_shape=jax.ShapeDtypeStruct((b, s, ATTN_WIDTH), BF16),
        scratch_shapes=[pltpu.VMEM((s, QUERY_BLOCK), F32),
                        pltpu.VMEM((ATTN_KV_HEADS, ATT_CK, GROUPS * QUERY_BLOCK), F32),
                        pltpu.VMEM((ATTN_KV_HEADS, ATT_CK, GROUPS * QUERY_BLOCK), F32)],
        compiler_params=_cparams("parallel", "parallel"), name="attn",
    )(qt, qit, wit, ki, k, vt, tri)


def _merge_kernel(x_ref, attn_ref, pool_ref, gate_ref, wba_ref, mix_ref, pscale_ref, wbp_ref, wout_ref,
                  g2_ref, rw_ref, rb_ref, tri_ref,
                  o_ref, h_ref, meta_ref, rgate_ref, cnt_ref, ext_ref, carry_ref):
    tm = x_ref.shape[1]
    j = pl.program_id(1)

    @pl.when(j == 0)
    def _():
        ext_ref[0:POOL_HALO] = jnp.zeros((POOL_HALO, POOL_WIDTH), F32)

    @pl.when(j > 0)
    def _():
        ext_ref[0:POOL_HALO] = ext_ref[tm:tm + POOL_HALO]

    ext_ref[POOL_HALO:POOL_HALO + tm] = pool_ref[0]
    t = j * tm + lax.broadcasted_iota(I32, (tm, 1), 0)
    mixed = []
    for g, w in enumerate(POOL_WINDOWS):
        cols = slice(g * POOL_GROUP_DIM, (g + 1) * POOL_GROUP_DIM)
        cur = ext_ref[POOL_HALO:POOL_HALO + tm, cols]
        wsum = cur
        for i in range(1, w):
            wsum = wsum + ext_ref[POOL_HALO - i:POOL_HALO - i + tm, cols]
        cnt = jnp.minimum(t + 1, w).astype(F32)
        dev = (wsum / cnt - cur).astype(BF16)
        mixed.append(jnp.dot(dev, mix_ref[g], preferred_element_type=F32))
    pooled = (jnp.concatenate(mixed, axis=1) * pscale_ref[...]).astype(BF16)
    branch_a = jnp.dot(attn_ref[0], wba_ref[...], preferred_element_type=F32)
    branch_p = jnp.dot(pooled, wbp_ref[...], preferred_element_type=F32)
    gates = gate_ref[0]
    merged = (jax.nn.sigmoid(gates[:, 0:D_MODEL]) * branch_a
              + jax.nn.sigmoid(gates[:, D_MODEL:2 * D_MODEL]) * branch_p)
    x1 = x_ref[0] + jnp.dot(merged.astype(BF16), wout_ref[...], preferred_element_type=F32)
    o_ref[0] = x1
    _route_tile(x1, jnp.logical_and(pl.program_id(0) == 0, j == 0), g2_ref, rw_ref, rb_ref, tri_ref,
                h_ref, meta_ref, rgate_ref, cnt_ref, carry_ref)


def _merge_call(x, attn, pool, gates, wba, mix, pscale, wbp, wout, g2, rw, rb):
    b, s, d = x.shape
    tm = min(512, s)
    nt = s // tm
    t = b * s
    tile = lambda w: pl.BlockSpec((1, tm, w), lambda i, j: (i, j, 0))
    flat = lambda rows, w: pl.BlockSpec((rows, w), lambda i, j: (i * nt + j, 0))
    full2 = lambda i, j: (0, 0)
    tri = jnp.asarray(np.tril(np.ones((tm, tm), np.float32), -1), BF16)
    in_specs = [
        tile(d), tile(ATTN_WIDTH), tile(POOL_WIDTH), tile(2 * D_MODEL),
        pl.BlockSpec((ATTN_WIDTH, d), full2),
        pl.BlockSpec((len(POOL_WINDOWS), POOL_GROUP_DIM, POOL_GROUP_DIM), lambda i, j: (0, 0, 0)),
        pl.BlockSpec((1, POOL_WIDTH), full2),
        pl.BlockSpec((POOL_WIDTH, d), full2),
        pl.BlockSpec((d, d), full2),
        pl.BlockSpec((1, d), full2),
        pl.BlockSpec((d, LANES), full2),
        pl.BlockSpec((1, LANES), full2),
        pl.BlockSpec((tm, tm), full2),
    ]
    out_shape = (
        jax.ShapeDtypeStruct((b, s, d), F32),
        jax.ShapeDtypeStruct((t * ROW_TILES, LANES), F32),
        jax.ShapeDtypeStruct((2 * TOP_K, t), I32),
        jax.ShapeDtypeStruct((t, LANES), F32),
        jax.ShapeDtypeStruct((1, LANES), I32),
    )
    out_specs = (
        tile(d), flat(tm * ROW_TILES, LANES),
        pl.BlockSpec((2 * TOP_K, tm), lambda i, j: (0, i * nt + j)),
        flat(tm, LANES), pl.BlockSpec((1, LANES), full2),
    )
    return pl.pallas_call(
        _merge_kernel, grid=(b, nt), in_specs=in_specs, out_specs=out_specs, out_shape=out_shape,
        scratch_shapes=[pltpu.VMEM((POOL_HALO + tm, POOL_WIDTH), F32), pltpu.VMEM((1, LANES), F32)],
        compiler_params=_cparams("arbitrary", "arbitrary"), name="merge",
    )(x, attn, pool, gates, wba, mix, pscale, wbp, wout, g2, rw, rb, tri)


ROW_TILES = D_MODEL // LANES


def _load_rows(ref, n, *lead):
    return jnp.concatenate(
        [ref[(*lead, pl.ds(c, n, stride=ROW_TILES), slice(None))] for c in range(ROW_TILES)], axis=1)


def _store_rows(ref, val):
    for c in range(ROW_TILES):
        ref[pl.ds(c, val.shape[0], stride=ROW_TILES), :] = val[:, c * LANES:(c + 1) * LANES]


def _row_tile(ref, i):
    return ref.at[pl.ds(pl.multiple_of(i * ROW_TILES, ROW_TILES), ROW_TILES)]


def _route_tile(x, first_tile, g2_ref, rw_ref, rb_ref, tri_ref, h_ref, meta_ref, gate_ref, cnt_ref, carry_ref):
    tm = x.shape[0]

    @pl.when(first_tile)
    def _():
        carry_ref[...] = jnp.zeros((1, LANES), F32)

    ms = jnp.mean(x * x, axis=-1, keepdims=True)
    h = x * lax.rsqrt(ms + NORM_EPS) * g2_ref[...]
    _store_rows(h_ref, h)
    logits = jnp.dot(h.astype(BF16), rw_ref[...], preferred_element_type=F32) + rb_ref[...]
    lane = lax.broadcasted_iota(I32, (tm, LANES), 1).astype(F32)
    work = jnp.where(lane < N_EXPERTS, logits, -jnp.inf)
    vals, idxs = [], []
    for _ in range(TOP_K):
        m = jnp.max(work, axis=-1, keepdims=True)
        idx = jnp.min(jnp.where(work == m, lane, float(LANES)), axis=-1, keepdims=True)
        vals.append(m)
        idxs.append(idx)
        work = jnp.where(lane == idx, -jnp.inf, work)
    exps = [jnp.exp(v - vals[0]) for v in vals]
    denom = exps[0] + exps[1] + exps[2] + exps[3]
    member = jnp.zeros((tm, LANES), F32)
    for idx in idxs:
        member = member + jnp.where(lane == idx, 1.0, 0.0)
    before = jnp.dot(tri_ref[...], member.astype(BF16), preferred_element_type=F32) + carry_ref[...]
    meta = jnp.zeros((tm, LANES), F32)
    gate = jnp.zeros((tm, LANES), F32)
    for k in range(TOP_K):
        rank = jnp.sum(jnp.where(lane == idxs[k], before, 0.0), axis=-1, keepdims=True)
        meta = jnp.where(lane == float(k), idxs[k], meta)
        meta = jnp.where(lane == float(TOP_K + k), rank, meta)
        gate = jnp.where(lane == float(k), exps[k] / denom, gate)
    meta_ref[...] = meta.T[0:2 * TOP_K].astype(I32)
    gate_ref[...] = gate
    total = carry_ref[...] + jnp.sum(member, axis=0, keepdims=True)
    carry_ref[...] = total
    cnt_ref[...] = total.astype(I32)


ROW_UNROLL = SUBLANES
ROW_DMA_TOKENS = 256


def _dispatch_kernel(slot_ref, h_ref, xs_ref, sem, *, tm):
    def issue(i, carry):
        for u in range(ROW_UNROLL):
            j = i * ROW_UNROLL + u
            for k in range(TOP_K):
                pltpu.make_async_copy(_row_tile(h_ref, j), _row_tile(xs_ref, slot_ref[k * tm + j]),
                                      sem).start(priority=k % 2)
        return carry

    lax.fori_loop(0, tm // ROW_UNROLL, issue, 0)

    def drain(j, carry):
        for k in range(TOP_K):
            pltpu.make_async_copy(_row_tile(h_ref, 0), _row_tile(xs_ref, 0), sem).wait()
        return carry

    lax.fori_loop(0, tm, drain, 0)


def _dispatch_call(slots, h2):
    t = h2.shape[0] // ROW_TILES
    tm = ROW_DMA_TOKENS
    return pl.pallas_call(
        functools.partial(_dispatch_kernel, tm=tm),
        grid=(t // tm,),
        in_specs=[pl.BlockSpec((tm * TOP_K,), lambda i: (i,), memory_space=pltpu.SMEM),
                  pl.BlockSpec((tm * ROW_TILES, LANES), lambda i: (i, 0))],
        out_specs=pl.BlockSpec(memory_space=pl.ANY),
        scratch_shapes=[pltpu.SemaphoreType.DMA(())],
        out_shape=jax.ShapeDtypeStruct((t * TOP_K * ROW_TILES, LANES), F32),
        compiler_params=_cparams("arbitrary"), name="dispatch",
    )(slots, h2)


def _ffn_kernel(item_e_ref, item_blk_ref, nact_ref, gstart_ref, gend_ref,
                xs_ref, wgu_ref, bgu_ref, wd_ref, bd_ref, o_ref, wgu_bf, wd_bf):
    w = pl.program_id(0)
    e = item_e_ref[w]
    blk = item_blk_ref[w]
    prev = jnp.maximum(w - 1, 0)
    active = w < nact_ref[0]
    new_expert = jnp.logical_or(w == 0, e != item_e_ref[prev])
    first_visit = jnp.logical_or(w == 0, blk != item_blk_ref[prev])

    @pl.when(jnp.logical_and(active, new_expert))
    def _():
        wgu_bf[...] = wgu_ref[0].astype(BF16)
        wd_bf[...] = wd_ref[0].astype(BF16)

    @pl.when(active)
    def _():
        xb = _load_rows(xs_ref, ROW_BLOCK).astype(BF16)
        res = bd_ref[0]
        for c in range(FFN_SPLIT):
            g0, g1 = c * FFN_COLS, (c + 1) * FFN_COLS
            gate = jnp.dot(xb, wgu_bf[:, g0:g1], preferred_element_type=F32) + bgu_ref[0, :, g0:g1]
            up = (jnp.dot(xb, wgu_bf[:, EXPERT_DIM + g0:EXPERT_DIM + g1], preferred_element_type=F32)
                  + bgu_ref[0, :, EXPERT_DIM + g0:EXPERT_DIM + g1])
            gate = jnp.minimum(gate, SWIGLU_LIMIT)
            up = jnp.clip(up, -SWIGLU_LIMIT, SWIGLU_LIMIT)
            act = gate * jax.nn.sigmoid(SWIGLU_ALPHA * gate) * (up + 1.0)
            res = res + jnp.dot(act.astype(BF16), wd_bf[g0:g1, :], preferred_element_type=F32)
        row = blk * ROW_BLOCK + lax.broadcasted_iota(I32, (ROW_BLOCK, 1), 0)
        mine = jnp.logical_and(row >= gstart_ref[e], row < gend_ref[e])

        @pl.when(first_visit)
        def _():
            _store_rows(o_ref, jnp.where(mine, res, 0.0))

        @pl.when(jnp.logical_not(first_visit))
        def _():
            _store_rows(o_ref, jnp.where(mine, res, _load_rows(o_ref, ROW_BLOCK)))


def _ffn_call(item_e, item_blk, nact, gstart, gend, xs, wgu, bgu, wd, bd):
    d = D_MODEL
    rows = lambda w, ie, ib, *_: (ib[w], 0)
    exp3 = lambda w, ie, *_: (ie[w], 0, 0)
    return pl.pallas_call(
        _ffn_kernel,
        grid_spec=pltpu.PrefetchScalarGridSpec(
            num_scalar_prefetch=5, grid=(item_e.shape[0],),
            in_specs=[pl.BlockSpec((ROW_BLOCK * ROW_TILES, LANES), rows),
                      pl.BlockSpec((1, d, 2 * EXPERT_DIM), exp3),
                      pl.BlockSpec((1, 1, 2 * EXPERT_DIM), exp3),
                      pl.BlockSpec((1, EXPERT_DIM, d), exp3),
                      pl.BlockSpec((1, 1, d), exp3)],
            out_specs=pl.BlockSpec((ROW_BLOCK * ROW_TILES, LANES), rows),
            scratch_shapes=[pltpu.VMEM((d, 2 * EXPERT_DIM), BF16), pltpu.VMEM((EXPERT_DIM, d), BF16)]),
        out_shape=jax.ShapeDtypeStruct(xs.shape, F32),
        compiler_params=_cparams("arbitrary"), name="ffn",
    )(item_e, item_blk, nact, gstart, gend, xs, wgu, bgu, wd, bd)


def _combine_kernel(slot_ref, slot_next_ref, ys_ref, gate_ref, x_ref, o_ref, buf_ref, sem, *, tm, n_steps):
    i = pl.program_id(0)
    half = i % 2

    def gather(slots, dst_half):
        def issue(it, carry):
            for u in range(ROW_UNROLL):
                j = it * ROW_UNROLL + u
                for k in range(TOP_K):
                    pltpu.make_async_copy(_row_tile(ys_ref, slots[k * tm + j]),
                                          _row_tile(buf_ref.at[dst_half, k], j),
                                          sem.at[dst_half]).start(priority=k % 2)
            return carry
        lax.fori_loop(0, tm // ROW_UNROLL, issue, 0)

    @pl.when(i == 0)
    def _():
        gather(slot_ref, 0)

    @pl.when(i + 1 < n_steps)
    def _():
        gather(slot_next_ref, 1 - half)

    def drain(j, carry):
        for k in range(TOP_K):
            pltpu.make_async_copy(_row_tile(ys_ref, 0), _row_tile(buf_ref.at[half, k], 0),
                                  sem.at[half]).wait()
        return carry

    lax.fori_loop(0, tm, drain, 0)
    gate = gate_ref[...]
    y = x_ref[...]
    for k in range(TOP_K):
        y = y + gate[:, k:k + 1] * _load_rows(buf_ref, tm, half, k)
    o_ref[...] = y


def _combine_call(slots, ys, gates, x1):
    t, d = x1.shape
    tm = ROW_DMA_TOKENS
    n_steps = t // tm
    tile = lambda w: pl.BlockSpec((tm, w), lambda i: (i, 0))
    return pl.pallas_call(
        functools.partial(_combine_kernel, tm=tm, n_steps=n_steps),
        grid=(n_steps,),
        in_specs=[pl.BlockSpec((tm * TOP_K,), lambda i: (i,), memory_space=pltpu.SMEM),
                  pl.BlockSpec((tm * TOP_K,), lambda i: (jnp.minimum(i + 1, n_steps - 1),),
                               memory_space=pltpu.SMEM),
                  pl.BlockSpec(memory_space=pl.ANY), tile(LANES), tile(d)],
        out_specs=tile(d),
        scratch_shapes=[pltpu.VMEM((2, TOP_K, tm * ROW_TILES, LANES), F32), pltpu.SemaphoreType.DMA((2,))],
        out_shape=jax.ShapeDtypeStruct((t, d), F32),
        compiler_params=_cparams("arbitrary"), name="combine",
    )(slots, slots, ys, gates, x1)


def _const_rows(q_g, k_g, i_g, i_b):
    zeros = jnp.zeros((LANES - IDX_DIM,), F32)
    rows = [
        jnp.tile(q_g.astype(F32), LANES // HEAD_DIM),
        jnp.tile(k_g.astype(F32), LANES // HEAD_DIM),
        jnp.concatenate([i_g.astype(F32), zeros]),
        jnp.concatenate([i_b.astype(F32), zeros]),
    ]
    return jnp.concatenate([jnp.stack(rows, axis=0), jnp.zeros((SUBLANES - len(rows), LANES), F32)], axis=0)


def _rope_freqs():
    inv_m = ROPE_THETA ** (-jnp.arange(0, HEAD_DIM, 2, dtype=F32) / HEAD_DIM)
    inv_i = ROPE_THETA ** (-jnp.arange(0, IDX_ROPE_DIM, 2, dtype=F32) / IDX_ROPE_DIM)
    pad = jnp.zeros((ROPE_ROWS - inv_m.shape[0] - inv_i.shape[0],), F32)
    return jnp.concatenate([inv_m, inv_i, pad]).reshape(ROPE_ROWS, 1)


def _ffn_schedule(counts, n_blocks):
    gend = jnp.cumsum(counts).astype(I32)
    gstart = gend - counts
    first_blk = gstart // ROW_BLOCK
    last_blk = (jnp.maximum(gend, 1) - 1) // ROW_BLOCK
    n_items = jnp.where(counts > 0, last_blk - first_blk + 1, 0)
    item_end = jnp.cumsum(n_items).astype(I32)
    item_start = item_end - n_items
    total = item_end[-1]
    w = jnp.minimum(jnp.arange(n_blocks + N_EXPERTS - 1, dtype=I32), total - 1)
    item_e = jnp.sum((item_end[None, :] <= w[:, None]).astype(I32), axis=1)
    mine = item_e[:, None] == jnp.arange(N_EXPERTS, dtype=I32)[None, :]
    item_blk = w + jnp.sum(jnp.where(mine, (first_blk - item_start)[None, :], 0), axis=1)
    return gstart, gend, item_e, item_blk, total.reshape(1)


def _layer(x, positions, norm1_g, w_in, q_norm_g, k_norm_g, idx_k_norm_g, idx_k_norm_b, w_branch_attn,
           pool_mix_w, pool_scale, w_branch_pool, w_out, norm2_g, router_w, router_b, w_gate_up,
           b_gate_up, w_down, b_down):
    b, s, d = x.shape
    t = b * s
    assert (t * TOP_K) % ROW_BLOCK == 0 and s % QUERY_BLOCK == 0
    lane = np.arange(LANES)
    gsum = jnp.asarray(lane[:, None] // HEAD_DIM == lane[None, :] // HEAD_DIM, BF16)

    qt, k, vt, qit, ki, wit, pool, gates = _proj_call(
        x, positions.reshape(b, 1, s), norm1_g.reshape(1, d), w_in,
        _const_rows(q_norm_g, k_norm_g, idx_k_norm_g, idx_k_norm_b), _rope_freqs(), gsum)
    attn = _attn_call(qt, k, vt, qit, ki, wit)
    rw = jnp.pad(router_w, ((0, 0), (0, LANES - N_EXPERTS))).astype(BF16)
    rb = jnp.pad(router_b, (0, LANES - N_EXPERTS)).reshape(1, LANES)
    x1, h2, meta, rgate, counts = _merge_call(
        x, attn, pool, gates, w_branch_attn.astype(BF16), pool_mix_w.astype(BF16),
        pool_scale.reshape(1, POOL_WIDTH), w_branch_pool.astype(BF16), w_out.astype(BF16),
        norm2_g.reshape(1, d), rw, rb)
    x1 = x1.reshape(t, d)

    gstart, gend, item_e, item_blk, nact = _ffn_schedule(counts[0, :N_EXPERTS], t * TOP_K // ROW_BLOCK)
    seg = jnp.sum(jnp.where(meta[None, 0:TOP_K] == jnp.arange(N_EXPERTS, dtype=I32)[:, None, None],
                            gstart[:, None, None], 0), axis=0)
    slots = (seg + meta[TOP_K:2 * TOP_K]).reshape(TOP_K, t // ROW_DMA_TOKENS, ROW_DMA_TOKENS)
    slots = slots.transpose(1, 0, 2).reshape(t * TOP_K)
    xs = _dispatch_call(slots, h2)
    ys = _ffn_call(item_e, item_blk, nact, gstart, gend, xs, w_gate_up,
                   b_gate_up.reshape(N_EXPERTS, 1, 2 * EXPERT_DIM), w_down, b_down.reshape(N_EXPERTS, 1, d))
    out = _combine_call(slots, ys, rgate, x1)
    return out.reshape(b, s, d)


def kernel(x, positions, norm1_g, w_in, q_norm_g, k_norm_g, idx_k_norm_g, idx_k_norm_b, w_branch_attn,
           pool_mix_w, pool_scale, w_branch_pool, w_out, norm2_g, router_w, router_b, w_gate_up,
           b_gate_up, w_down, b_down):
    for l in range(norm1_g.shape[0]):
        x = _layer(x, positions, norm1_g[l], w_in[l], q_norm_g[l], k_norm_g[l], idx_k_norm_g[l],
                   idx_k_norm_b[l], w_branch_attn[l], pool_mix_w[l], pool_scale[l], w_branch_pool[l],
                   w_out[l], norm2_g[l], router_w[l], router_b[l], w_gate_up[l], b_gate_up[l],
                   w_down[l], b_down[l])
    return x
```

```python
import functools

import numpy as np
import jax
import jax.numpy as jnp
from jax import lax
from jax.experimental import pallas as pl
from jax.experimental.pallas import tpu as pltpu

F32 = jnp.float32
BF16 = jnp.bfloat16
I32 = jnp.int32

D_MODEL = 1024
CHUNK = 64
CHUNK_SHIFT = CHUNK.bit_length() - 1
ATTN_HEADS = 8
ATTN_KV_HEADS = 2
HEAD_DIM = 64
GROUPS = ATTN_HEADS // ATTN_KV_HEADS
ATTN_WIDTH = ATTN_HEADS * HEAD_DIM
KV_WIDTH = ATTN_KV_HEADS * HEAD_DIM
ATTN_SCALE = HEAD_DIM ** -0.5
IDX_HEADS = 4
IDX_DIM = 64
IDX_ROPE_DIM = 32
IDX_SCALE = (IDX_HEADS ** -0.5) * (IDX_DIM ** -0.5)
IDX_TOPK_MAX = 256
QUERY_BLOCK = 256
POOL_WINDOWS = (2, 4, 8, 16)
POOL_WIDTH = 512
POOL_GROUP_DIM = 128
POOL_HALO = 16
N_EXPERTS = 32
TOP_K = 4
EXPERT_DIM = 1024
SWIGLU_ALPHA = 1.702
SWIGLU_LIMIT = 7.0
ROPE_THETA = 10000.0
NORM_EPS = 1e-6

LANES = 128
SUBLANES = 8
VREGS = 64
VMEM_LIMIT = 56 * 1024 * 1024
FLT_MAX = float(np.finfo(np.float32).max)
MASKED = -1e30

C_Q = 0
C_K = C_Q + ATTN_WIDTH
C_V = C_K + KV_WIDTH
C_QI = C_V + KV_WIDTH
C_KIW = C_QI + IDX_HEADS * IDX_DIM
C_POOL = C_KIW + LANES
C_GATE = C_POOL + POOL_WIDTH
W_PACKED = C_GATE + 2 * D_MODEL
C_SMALL_END = C_POOL
PACK_COLS = 512

ROW_BLOCK = 512
FOLD_ROWS = 4 * SUBLANES
VALUE_BISECT_ITERS = 16
BISECT_CAP = 64
BISECT_HEAD = 16
BISECT_UNROLL = 2
ATT_CK = 256
TIE_ROWS = 512
VT_ROWS = HEAD_DIM + 2 * SUBLANES
LOG2E = float(np.log2(np.e))
ROPE_ROWS = 64


def _cparams(*sem):
    return pltpu.CompilerParams(dimension_semantics=sem, vmem_limit_bytes=VMEM_LIMIT)


def _proj_kernel(x_ref, pos_ref, g1_ref, wraw_ref, rows_ref, freq_ref, gsum_ref,
                 qt_ref, k_ref, vt_ref, qit_ref, ki_ref, wit_ref, pool_ref, gate_ref, w_ref):
    @pl.when(jnp.logical_and(pl.program_id(0) == 0, pl.program_id(1) == 0))
    def _():
        narrow = IDX_DIM + IDX_HEADS
        w_ref[:, 0:C_KIW] = wraw_ref[:, 0:C_KIW].astype(BF16)
        w_ref[:, C_KIW:C_POOL] = jnp.concatenate(
            [wraw_ref[:, C_KIW:C_KIW + narrow], jnp.zeros((D_MODEL, LANES - narrow), F32)], axis=1).astype(BF16)
        for c0 in range(C_POOL, W_PACKED, PACK_COLS):
            src = c0 - (LANES - narrow)
            w_ref[:, c0:c0 + PACK_COLS] = wraw_ref[:, src:src + PACK_COLS].astype(BF16)

    x = x_ref[0]
    tm = x.shape[0]
    ms = jnp.mean(x * x, axis=-1, keepdims=True)
    h = (x * lax.rsqrt(ms + NORM_EPS) * g1_ref[...]).astype(BF16)
    d1 = jnp.dot(h, w_ref[:, 0:C_SMALL_END], preferred_element_type=F32)
    pool_ref[0] = jnp.dot(h, w_ref[:, C_POOL:C_GATE], preferred_element_type=F32)
    gate_ref[0] = jnp.dot(h, w_ref[:, C_GATE:W_PACKED], preferred_element_type=F32)

    half_m, half_i = HEAD_DIM // 2, IDX_ROPE_DIM // 2
    ang = freq_ref[...] * pos_ref[0].astype(F32)
    cos_t, sin_t = jnp.cos(ang), jnp.sin(ang)
    cm, sm = cos_t[0:half_m], sin_t[0:half_m]
    ci, si = cos_t[half_m:half_m + half_i], sin_t[half_m:half_m + half_i]
    rest_one = jnp.ones((IDX_DIM - IDX_ROPE_DIM, tm), F32)
    rest_zero = jnp.zeros((IDX_DIM - IDX_ROPE_DIM, tm), F32)
    heads = LANES // HEAD_DIM
    cos_m = jnp.concatenate([cm, cm] * heads, axis=0)
    sin_m = jnp.concatenate([-sm, sm] * heads, axis=0)
    cos_i = jnp.concatenate([ci, ci, rest_one] * heads, axis=0)
    sin_i = jnp.concatenate([-si, si, rest_zero] * heads, axis=0)
    rows = rows_ref[...]
    gsum = gsum_ref[...]

    def head_rms(xc, grow):
        sq = xc * xc
        hi = sq.astype(BF16)
        lo = (sq - hi.astype(F32)).astype(BF16)
        ssum = (jnp.dot(hi, gsum, preferred_element_type=F32)
                + jnp.dot(lo, gsum, preferred_element_type=F32))
        return xc * lax.rsqrt(ssum * (1.0 / HEAD_DIM) + NORM_EPS) * grow

    def rope_t(xt, cos_tab, sin_tab, half, width):
        parts = []
        for h0 in range(0, LANES, width):
            parts += [xt[h0 + half:h0 + 2 * half], xt[h0:h0 + half]]
            if 2 * half < width:
                parts.append(xt[h0 + 2 * half:h0 + width])
        return xt * cos_tab + jnp.concatenate(parts, axis=0) * sin_tab

    for c in range(ATTN_WIDTH // LANES):
        qc = d1[:, C_Q + c * LANES:C_Q + (c + 1) * LANES]
        qr = rope_t(head_rms(qc, rows[0:1]).T, cos_m, sin_m, half_m, HEAD_DIM) * (ATTN_SCALE * LOG2E)
        qt_ref[0, c * LANES:(c + 1) * LANES, :] = qr.astype(BF16)

    kr = rope_t(head_rms(d1[:, C_K:C_K + KV_WIDTH], rows[1:2]).T, cos_m, sin_m, half_m, HEAD_DIM).T
    vt = d1[:, C_V:C_V + KV_WIDTH].T
    for r in range(ATTN_KV_HEADS):
        k_ref[0, r] = kr[:, r * HEAD_DIM:(r + 1) * HEAD_DIM].astype(BF16)
        vt_ref[0, r, 0:HEAD_DIM, :] = vt[r * HEAD_DIM:(r + 1) * HEAD_DIM].astype(BF16)
        vt_ref[0, r, HEAD_DIM:VT_ROWS, :] = jnp.ones((VT_ROWS - HEAD_DIM, tm), BF16)

    for c in range(IDX_HEADS * IDX_DIM // LANES):
        qc = d1[:, C_QI + c * LANES:C_QI + (c + 1) * LANES]
        qit_ref[0, c * LANES:(c + 1) * LANES, :] = rope_t(qc.T, cos_i, sin_i, half_i, IDX_DIM).astype(BF16)

    kiw = d1[:, C_KIW:C_KIW + LANES]
    in_ki = lax.broadcasted_iota(I32, (tm, LANES), 1) < IDX_DIM
    mu = jnp.sum(jnp.where(in_ki, kiw, 0.0), axis=-1, keepdims=True) * (1.0 / IDX_DIM)
    dv = jnp.where(in_ki, kiw - mu, 0.0)
    var = jnp.sum(dv * dv, axis=-1, keepdims=True) * (1.0 / IDX_DIM)
    kin = dv * lax.rsqrt(var + NORM_EPS) * rows[2:3] + rows[3:4]
    ki_ref[0] = rope_t(kin.T, cos_i, sin_i, half_i, IDX_DIM).T[:, 0:IDX_DIM].astype(BF16)
    wit_ref[0] = (kiw * IDX_SCALE).T[IDX_DIM:IDX_DIM + SUBLANES]


def _proj_call(x, pos_rows, g1, w_in, rows, freq, gsum):
    b, s, d = x.shape
    tm = min(512, s)
    grid = (b, s // tm)
    full2 = lambda i, j: (0, 0)
    out_shape = (
        jax.ShapeDtypeStruct((b, ATTN_WIDTH, s), BF16),
        jax.ShapeDtypeStruct((b, ATTN_KV_HEADS, s, HEAD_DIM), BF16),
        jax.ShapeDtypeStruct((b, ATTN_KV_HEADS, VT_ROWS, s), BF16),
        jax.ShapeDtypeStruct((b, IDX_HEADS * IDX_DIM, s), BF16),
        jax.ShapeDtypeStruct((b, s, IDX_DIM), BF16),
        jax.ShapeDtypeStruct((b, SUBLANES, s), F32),
        jax.ShapeDtypeStruct((b, s, POOL_WIDTH), F32),
        jax.ShapeDtypeStruct((b, s, 2 * D_MODEL), F32),
    )
    in_specs = [
        pl.BlockSpec((1, tm, d), lambda i, j: (i, j, 0)),
        pl.BlockSpec((1, 1, tm), lambda i, j: (i, 0, j)),
        pl.BlockSpec((1, d), full2),
        pl.BlockSpec(w_in.shape, full2, pipeline_mode=pl.Buffered(1)),
        pl.BlockSpec((SUBLANES, LANES), full2),
        pl.BlockSpec((ROPE_ROWS, 1), full2),
        pl.BlockSpec((LANES, LANES), full2),
    ]
    out_specs = (
        pl.BlockSpec((1, ATTN_WIDTH, tm), lambda i, j: (i, 0, j)),
        pl.BlockSpec((1, ATTN_KV_HEADS, tm, HEAD_DIM), lambda i, j: (i, 0, j, 0)),
        pl.BlockSpec((1, ATTN_KV_HEADS, VT_ROWS, tm), lambda i, j: (i, 0, 0, j)),
        pl.BlockSpec((1, IDX_HEADS * IDX_DIM, tm), lambda i, j: (i, 0, j)),
        pl.BlockSpec((1, tm, IDX_DIM), lambda i, j: (i, j, 0)),
        pl.BlockSpec((1, SUBLANES, tm), lambda i, j: (i, 0, j)),
        pl.BlockSpec((1, tm, POOL_WIDTH), lambda i, j: (i, j, 0)),
        pl.BlockSpec((1, tm, 2 * D_MODEL), lambda i, j: (i, j, 0)),
    )
    return pl.pallas_call(
        _proj_kernel, grid=grid, in_specs=in_specs, out_specs=out_specs, out_shape=out_shape,
        scratch_shapes=[pltpu.VMEM((d, W_PACKED), BF16)],
        compiler_params=_cparams("arbitrary", "arbitrary"), name="proj",
    )(x, pos_rows, g1, w_in, rows, freq, gsum)


def _sort_key(x):
    b = lax.bitcast_convert_type(x, I32)
    return b ^ ((b >> 31) & 0x7FFFFFFF)


def _unsort_key(k):
    return lax.bitcast_convert_type(k ^ ((k >> 31) & 0x7FFFFFFF), F32)


def _attn_kernel(qt_ref, qit_ref, wit_ref, ki_ref, k_ref, vt_ref, tri_ref, o_ref, sc_ref, sa_ref, sb_ref,
                 *, n_sel, ck):
    qb = QUERY_BLOCK
    t0 = pl.program_id(1) * qb
    nck = (t0 + qb + ck - 1) // ck
    qpos = t0 + lax.broadcasted_iota(I32, (1, qb), 1)
    cend = ((qpos >> CHUNK_SHIFT) + 1) * CHUNK
    key_ck = lax.broadcasted_iota(I32, (ck, qb), 0)
    k_sel = float(n_sel)
    cnt_rows = min(ck, VREGS * SUBLANES * LANES // qb)

    def fold(x, op):
        return op(x.reshape(ck // FOLD_ROWS, FOLD_ROWS, qb), axis=0)

    qit = qit_ref[0]
    rhs_i = jnp.concatenate([qit[h * IDX_DIM:(h + 1) * IDX_DIM] for h in range(IDX_HEADS)], axis=1)
    wit = wit_ref[0]

    def score_body(c, carry):
        mxp, mnp, c0p, c1p = carry
        off = pl.multiple_of(c * ck, ck)
        lg = jnp.dot(ki_ref[0, pl.ds(off, ck), :], rhs_i, preferred_element_type=F32)
        sc = jnp.maximum(lg[:, 0:qb], 0.0) * wit[0:1]
        for h in range(1, IDX_HEADS):
            sc = sc + jnp.maximum(lg[:, h * qb:(h + 1) * qb], 0.0) * wit[h:h + 1]
        adm = (off + key_ck) < cend
        sc = jnp.where(adm, sc, -jnp.inf)
        sc_ref[pl.ds(off, ck), :] = sc
        mxp = jnp.maximum(mxp, fold(sc, jnp.max))
        mnp = jnp.minimum(mnp, fold(jnp.where(adm, sc, jnp.inf), jnp.min))
        c0p = c0p + fold(jnp.where(sc >= 0.0, 1.0, 0.0), jnp.sum)
        c1p = c1p + fold(jnp.where(sc > 0.0, 1.0, 0.0), jnp.sum)
        return mxp, mnp, c0p, c1p

    init = (jnp.full((FOLD_ROWS, qb), -jnp.inf, F32), jnp.full((FOLD_ROWS, qb), jnp.inf, F32),
            jnp.zeros((FOLD_ROWS, qb), F32), jnp.zeros((FOLD_ROWS, qb), F32))
    mxp, mnp, c0p, c1p = lax.fori_loop(0, nck, score_body, init)
    mx = jnp.max(mxp, axis=0, keepdims=True)
    mn = jnp.min(mnp, axis=0, keepdims=True)
    c0 = jnp.sum(c0p, axis=0, keepdims=True)
    c1 = jnp.sum(c1p, axis=0, keepdims=True)

    def count(pred):
        def body(c, acc):
            for sb in range(ck // cnt_rows):
                off = pl.multiple_of(c * ck + sb * cnt_rows, cnt_rows)
                hit = pred(sc_ref[pl.ds(off, cnt_rows), :], off + key_ck[0:cnt_rows])
                acc = acc + jnp.sum(jnp.where(hit, 1.0, 0.0).reshape(cnt_rows // FOLD_ROWS, FOLD_ROWS, qb),
                                    axis=0)
            return acc
        acc = lax.fori_loop(0, nck, body, jnp.zeros((FOLD_ROWS, qb), F32))
        return jnp.sum(acc, axis=0, keepdims=True)

    small = cend.astype(F32) <= k_sel
    at_zero = jnp.logical_and(jnp.logical_not(small), jnp.logical_and(c1 < k_sel, c0 >= k_sel))
    positive = jnp.logical_and(jnp.logical_not(small), c1 >= k_sel)
    lo0 = jnp.where(positive, 0.0, mn)
    hi0 = jnp.where(positive, jnp.minimum(2.0 * mx, FLT_MAX), 0.0)
    thr0 = jnp.where(small, -FLT_MAX, 0.0)
    done0 = jnp.where(jnp.logical_or(small, at_zero), 1.0, 0.0)
    tie0 = jnp.where(jnp.logical_and(at_zero, c0 > k_sel), 1.0, 0.0)
    chi0 = jnp.where(positive, 0.0, c0)
    left0 = jnp.sum(1.0 - done0)

    def bisect_cond(st):
        it, left = st[0], st[1]
        return jnp.logical_and(it < BISECT_CAP, left > 0.0)

    def bisect_step(it, lo, hi, thr, done, tie, chi):
        mid_v = 0.5 * lo + 0.5 * hi
        klo = _sort_key(lo)
        khi = _sort_key(hi)
        mid_k = _unsort_key((klo & khi) + ((klo ^ khi) >> 1))
        mid = jnp.where(it < VALUE_BISECT_ITERS, mid_v, mid_k)
        stuck = jnp.logical_or(mid <= lo, mid >= hi)
        cnt = count(lambda s, _: s >= mid)
        active = done == 0.0
        moving = jnp.logical_and(active, jnp.logical_not(stuck))
        hit = jnp.logical_and(moving, cnt == k_sel)
        new_tie = jnp.logical_and(active, stuck)
        thr = jnp.where(hit, mid, jnp.where(new_tie, lo, thr))
        tie = jnp.where(new_tie, 1.0, tie)
        done = jnp.where(jnp.logical_or(hit, new_tie), 1.0, done)
        upd = jnp.logical_and(moving, jnp.logical_not(hit))
        lo = jnp.where(jnp.logical_and(upd, cnt >= k_sel), mid, lo)
        lower = jnp.logical_and(upd, cnt < k_sel)
        hi = jnp.where(lower, mid, hi)
        chi = jnp.where(lower, cnt, chi)
        return lo, hi, thr, done, tie, chi

    def bisect_body(steps, st):
        it, _, lo, hi, thr, done, tie, chi = st
        for u in range(steps):
            lo, hi, thr, done, tie, chi = bisect_step(it + u, lo, hi, thr, done, tie, chi)
        return it + steps, jnp.sum(1.0 - done), lo, hi, thr, done, tie, chi

    st = (jnp.int32(0), left0, lo0, hi0, thr0, done0, tie0, chi0)
    st = lax.while_loop(lambda s: jnp.logical_and(s[0] < BISECT_HEAD, s[1] > 0.0),
                        functools.partial(bisect_body, BISECT_HEAD), st)
    st = lax.while_loop(bisect_cond, functools.partial(bisect_body, BISECT_UNROLL), st)
    thr, tie = st[4], st[6]
    n_above = jnp.where(at_zero, c1, st[7])

    @pl.when(jnp.sum(tie) > 0.0)
    def _():
        need = jnp.where(tie > 0.0, k_sel - n_above, FLT_MAX)

        def drop_body(c, before):
            tr = tri_ref.shape[0]
            for sb in range(ck // tr):
                off = pl.multiple_of(c * ck + sb * tr, tr)
                blk = sc_ref[pl.ds(off, tr), :]
                eq = blk == thr
                eqf = jnp.where(eq, 1.0, 0.0)
                rank = before + jnp.dot(tri_ref[...], eqf.astype(BF16), preferred_element_type=F32)
                sc_ref[pl.ds(off, tr), :] = jnp.where(jnp.logical_and(eq, rank >= need), -jnp.inf, blk)
                before = before + jnp.sum(jnp.sum(eqf.reshape(tr // FOLD_ROWS, FOLD_ROWS, qb), axis=0),
                                          axis=0, keepdims=True)
            return before

        lax.fori_loop(0, nck, drop_body, jnp.zeros((1, qb), F32))

    qt = qt_ref[0]
    rhs = [jnp.concatenate(
        [qt[(r * GROUPS + g) * HEAD_DIM:(r * GROUPS + g + 1) * HEAD_DIM] for g in range(GROUPS)],
        axis=1) for r in range(ATTN_KV_HEADS)]

    n_att = (t0 + qb + ATT_CK - 1) // ATT_CK

    def step_offset(step):
        return pl.multiple_of(jnp.minimum(step, n_att - 1) * ATT_CK, ATT_CK)

    def qk_logits(step, buf):
        off = step_offset(step)
        thr_step = jnp.where(step < n_att, thr, jnp.inf)
        bias = jnp.where(sc_ref[pl.ds(off, ATT_CK), :] >= thr_step, 0.0, MASKED)
        for r in range(ATTN_KV_HEADS):
            lg = jnp.dot(k_ref[0, r, pl.ds(off, ATT_CK), :], rhs[r], preferred_element_type=F32)
            for g in range(GROUPS):
                buf[r, :, g * qb:(g + 1) * qb] = lg[:, g * qb:(g + 1) * qb] + bias

    def softmax_pv(step, buf, carry):
        off = step_offset(step)
        out = []
        for r in range(ATTN_KV_HEADS):
            m, acc = carry[r]
            m_new = jnp.maximum(m, jnp.max(buf[r], axis=0, keepdims=True))
            p = jnp.exp2(buf[r] - m_new).astype(BF16)
            pv = jnp.dot(vt_ref[0, r, :, pl.ds(off, ATT_CK)], p, preferred_element_type=F32)
            out.append((m_new, jnp.exp2(m - m_new) * acc + pv))
        return tuple(out)

    def att_body(i, carry):
        qk_logits(2 * i + 1, sb_ref)
        carry = softmax_pv(2 * i, sa_ref, carry)
        qk_logits(2 * i + 2, sa_ref)
        return softmax_pv(2 * i + 1, sb_ref, carry)

    init_a = tuple((jnp.full((1, GROUPS * qb), MASKED, F32), jnp.zeros((VT_ROWS, GROUPS * qb), F32))
                   for _ in range(ATTN_KV_HEADS))
    qk_logits(0, sa_ref)
    fin = lax.fori_loop(0, (n_att + 1) // 2, att_body, init_a)
    outs = []
    for r in range(ATTN_KV_HEADS):
        acc = fin[r][1]
        o = acc[0:HEAD_DIM] / acc[HEAD_DIM:HEAD_DIM + 1]
        outs.extend(o[:, g * qb:(g + 1) * qb] for g in range(GROUPS))
    o_ref[0] = jnp.concatenate(outs, axis=0).T.astype(BF16)


def _attn_call(qt, k, vt, qit, ki, wit):
    b, _, s = qt.shape
    n_sel = min(IDX_TOPK_MAX, s // 4)
    ck = min(512, s)
    grid = (b, s // QUERY_BLOCK)
    in_specs = [
        pl.BlockSpec((1, ATTN_WIDTH, QUERY_BLOCK), lambda i, j: (i, 0, j)),
        pl.BlockSpec((1, IDX_HEADS * IDX_DIM, QUERY_BLOCK), lambda i, j: (i, 0, j)),
        pl.BlockSpec((1, SUBLANES, QUERY_BLOCK), lambda i, j: (i, 0, j)),
        pl.BlockSpec((1, s, IDX_DIM), lambda i, j: (i, 0, 0)),
        pl.BlockSpec((1, ATTN_KV_HEADS, s, HEAD_DIM), lambda i, j: (i, 0, 0, 0)),
        pl.BlockSpec((1, ATTN_KV_HEADS, VT_ROWS, s), lambda i, j: (i, 0, 0, 0)),
        pl.BlockSpec((min(TIE_ROWS, ck), min(TIE_ROWS, ck)), lambda i, j: (0, 0)),
    ]
    tri = jnp.asarray(np.tril(np.ones((min(TIE_ROWS, ck), min(TIE_ROWS, ck)), np.float32), -1), BF16)
    return pl.pallas_call(
        functools.partial(_attn_kernel, n_sel=n_sel, ck=ck),
        grid=grid, in_specs=in_specs,
        out_specs=pl.BlockSpec((1, QUERY_BLOCK, ATTN_WIDTH), lambda i, j: (i, j, 0)),
        out_shape=jax.ShapeDtypeStruct((b, s, ATTN_WIDTH), BF16),
        scratch_shapes=[pltpu.VMEM((s, QUERY_BLOCK), F32),
                        pltpu.VMEM((ATTN_KV_HEADS, ATT_CK, GROUPS * QUERY_BLOCK), F32),
                        pltpu.VMEM((ATTN_KV_HEADS, ATT_CK, GROUPS * QUERY_BLOCK), F32)],
        compiler_params=_cparams("parallel", "parallel"), name="attn",
    )(qt, qit, wit, ki, k, vt, tri)


def _merge_kernel(x_ref, attn_ref, pool_ref, gate_ref, wba_ref, mix_ref, pscale_ref, wbp_ref, wout_ref,
                  g2_ref, rw_ref, rb_ref, tri_ref,
                  o_ref, h_ref, meta_ref, rgate_ref, cnt_ref, ext_ref, carry_ref):
    tm = x_ref.shape[1]
    j = pl.program_id(1)

    @pl.when(j == 0)
    def _():
        ext_ref[0:POOL_HALO] = jnp.zeros((POOL_HALO, POOL_WIDTH), F32)

    @pl.when(j > 0)
    def _():
        ext_ref[0:POOL_HALO] = ext_ref[tm:tm + POOL_HALO]

    ext_ref[POOL_HALO:POOL_HALO + tm] = pool_ref[0]
    t = j * tm + lax.broadcasted_iota(I32, (tm, 1), 0)
    mixed = []
    for g, w in enumerate(POOL_WINDOWS):
        cols = slice(g * POOL_GROUP_DIM, (g + 1) * POOL_GROUP_DIM)
        cur = ext_ref[POOL_HALO:POOL_HALO + tm, cols]
        wsum = cur
        for i in range(1, w):
            wsum = wsum + ext_ref[POOL_HALO - i:POOL_HALO - i + tm, cols]
        cnt = jnp.minimum(t + 1, w).astype(F32)
        dev = (wsum / cnt - cur).astype(BF16)
        mixed.append(jnp.dot(dev, mix_ref[g], preferred_element_type=F32))
    pooled = (jnp.concatenate(mixed, axis=1) * pscale_ref[...]).astype(BF16)
    branch_a = jnp.dot(attn_ref[0], wba_ref[...], preferred_element_type=F32)
    branch_p = jnp.dot(pooled, wbp_ref[...], preferred_element_type=F32)
    gates = gate_ref[0]
    merged = (jax.nn.sigmoid(gates[:, 0:D_MODEL]) * branch_a
              + jax.nn.sigmoid(gates[:, D_MODEL:2 * D_MODEL]) * branch_p)
    x1 = x_ref[0] + jnp.dot(merged.astype(BF16), wout_ref[...], preferred_element_type=F32)
    o_ref[0] = x1
    _route_tile(x1, jnp.logical_and(pl.program_id(0) == 0, j == 0), g2_ref, rw_ref, rb_ref, tri_ref,
                h_ref, meta_ref, rgate_ref, cnt_ref, carry_ref)


def _merge_call(x, attn, pool, gates, wba, mix, pscale, wbp, wout, g2, rw, rb):
    b, s, d = x.shape
    tm = min(512, s)
    nt = s // tm
    t = b * s
    tile = lambda w: pl.BlockSpec((1, tm, w), lambda i, j: (i, j, 0))
    flat = lambda rows, w: pl.BlockSpec((rows, w), lambda i, j: (i * nt + j, 0))
    full2 = lambda i, j: (0, 0)
    tri = jnp.asarray(np.tril(np.ones((tm, tm), np.float32), -1), BF16)
    in_specs = [
        tile(d), tile(ATTN_WIDTH), tile(POOL_WIDTH), tile(2 * D_MODEL),
        pl.BlockSpec((ATTN_WIDTH, d), full2),
        pl.BlockSpec((len(POOL_WINDOWS), POOL_GROUP_DIM, POOL_GROUP_DIM), lambda i, j: (0, 0, 0)),
        pl.BlockSpec((1, POOL_WIDTH), full2),
        pl.BlockSpec((POOL_WIDTH, d), full2),
        pl.BlockSpec((d, d), full2),
        pl.BlockSpec((1, d), full2),
        pl.BlockSpec((d, LANES), full2),
        pl.BlockSpec((1, LANES), full2),
        pl.BlockSpec((tm, tm), full2),
    ]
    out_shape = (
        jax.ShapeDtypeStruct((b, s, d), F32),
        jax.ShapeDtypeStruct((t * ROW_TILES, LANES), F32),
        jax.ShapeDtypeStruct((2 * TOP_K, t), I32),
        jax.ShapeDtypeStruct((t, LANES), F32),
        jax.ShapeDtypeStruct((1, LANES), I32),
    )
    out_specs = (
        tile(d), flat(tm * ROW_TILES, LANES),
        pl.BlockSpec((2 * TOP_K, tm), lambda i, j: (0, i * nt + j)),
        flat(tm, LANES), pl.BlockSpec((1, LANES), full2),
    )
    return pl.pallas_call(
        _merge_kernel, grid=(b, nt), in_specs=in_specs, out_specs=out_specs, out_shape=out_shape,
        scratch_shapes=[pltpu.VMEM((POOL_HALO + tm, POOL_WIDTH), F32), pltpu.VMEM((1, LANES), F32)],
        compiler_params=_cparams("arbitrary", "arbitrary"), name="merge",
    )(x, attn, pool, gates, wba, mix, pscale, wbp, wout, g2, rw, rb, tri)


ROW_TILES = D_MODEL // LANES


def _load_rows(ref, n, *lead):
    return jnp.concatenate(
        [ref[(*lead, pl.ds(c, n, stride=ROW_TILES), slice(None))] for c in range(ROW_TILES)], axis=1)


def _store_rows(ref, val):
    for c in range(ROW_TILES):
        ref[pl.ds(c, val.shape[0], stride=ROW_TILES), :] = val[:, c * LANES:(c + 1) * LANES]


def _row_tile(ref, i):
    return ref.at[pl.ds(pl.multiple_of(i * ROW_TILES, ROW_TILES), ROW_TILES)]


def _route_tile(x, first_tile, g2_ref, rw_ref, rb_ref, tri_ref, h_ref, meta_ref, gate_ref, cnt_ref, carry_ref):
    tm = x.shape[0]

    @pl.when(first_tile)
    def _():
        carry_ref[...] = jnp.zeros((1, LANES), F32)

    ms = jnp.mean(x * x, axis=-1, keepdims=True)
    h = x * lax.rsqrt(ms + NORM_EPS) * g2_ref[...]
    _store_rows(h_ref, h)
    logits = jnp.dot(h.astype(BF16), rw_ref[...], preferred_element_type=F32) + rb_ref[...]
    lane = lax.broadcasted_iota(I32, (tm, LANES), 1).astype(F32)
    work = jnp.where(lane < N_EXPERTS, logits, -jnp.inf)
    vals, idxs = [], []
    for _ in range(TOP_K):
        m = jnp.max(work, axis=-1, keepdims=True)
        idx = jnp.min(jnp.where(work == m, lane, float(LANES)), axis=-1, keepdims=True)
        vals.append(m)
        idxs.append(idx)
        work = jnp.where(lane == idx, -jnp.inf, work)
    exps = [jnp.exp(v - vals[0]) for v in vals]
    denom = exps[0] + exps[1] + exps[2] + exps[3]
    member = jnp.zeros((tm, LANES), F32)
    for idx in idxs:
        member = member + jnp.where(lane == idx, 1.0, 0.0)
    before = jnp.dot(tri_ref[...], member.astype(BF16), preferred_element_type=F32) + carry_ref[...]
    meta = jnp.zeros((tm, LANES), F32)
    gate = jnp.zeros((tm, LANES), F32)
    for k in range(TOP_K):
        rank = jnp.sum(jnp.where(lane == idxs[k], before, 0.0), axis=-1, keepdims=True)
        meta = jnp.where(lane == float(k), idxs[k], meta)
        meta = jnp.where(lane == float(TOP_K + k), rank, meta)
        gate = jnp.where(lane == float(k), exps[k] / denom, gate)
    meta_ref[...] = meta.T[0:2 * TOP_K].astype(I32)
    gate_ref[...] = gate
    total = carry_ref[...] + jnp.sum(member, axis=0, keepdims=True)
    carry_ref[...] = total
    cnt_ref[...] = total.astype(I32)


ROW_UNROLL = SUBLANES
ROW_DMA_TOKENS = 256


def _dispatch_kernel(slot_ref, h_ref, xs_ref, sem, *, tm):
    def issue(i, carry):
        for u in range(ROW_UNROLL):
            j = i * ROW_UNROLL + u
            for k in range(TOP_K):
                pltpu.make_async_copy(_row_tile(h_ref, j), _row_tile(xs_ref, slot_ref[k * tm + j]),
                                      sem).start(priority=k % 2)
        return carry

    lax.fori_loop(0, tm // ROW_UNROLL, issue, 0)

    def drain(j, carry):
        for k in range(TOP_K):
            pltpu.make_async_copy(_row_tile(h_ref, 0), _row_tile(xs_ref, 0), sem).wait()
        return carry

    lax.fori_loop(0, tm, drain, 0)


def _dispatch_call(slots, h2):
    t = h2.shape[0] // ROW_TILES
    tm = ROW_DMA_TOKENS
    return pl.pallas_call(
        functools.partial(_dispatch_kernel, tm=tm),
        grid=(t // tm,),
        in_specs=[pl.BlockSpec((tm * TOP_K,), lambda i: (i,), memory_space=pltpu.SMEM),
                  pl.BlockSpec((tm * ROW_TILES, LANES), lambda i: (i, 0))],
        out_specs=pl.BlockSpec(memory_space=pl.ANY),
        scratch_shapes=[pltpu.SemaphoreType.DMA(())],
        out_shape=jax.ShapeDtypeStruct((t * TOP_K * ROW_TILES, LANES), F32),
        compiler_params=_cparams("arbitrary"), name="dispatch",
    )(slots, h2)


def _ffn_kernel(item_e_ref, item_blk_ref, nact_ref, gstart_ref, gend_ref,
                xs_ref, wgu_ref, bgu_ref, wd_ref, bd_ref, o_ref, wgu_bf, wd_bf):
    w = pl.program_id(0)
    e = item_e_ref[w]
    blk = item_blk_ref[w]
    prev = jnp.maximum(w - 1, 0)
    active = w < nact_ref[0]
    new_expert = jnp.logical_or(w == 0, e != item_e_ref[prev])
    first_visit = jnp.logical_or(w == 0, blk != item_blk_ref[prev])

    @pl.when(jnp.logical_and(active, new_expert))
    def _():
        wgu_bf[...] = wgu_ref[0].astype(BF16)
        wd_bf[...] = wd_ref[0].astype(BF16)

    @pl.when(active)
    def _():
        xb = _load_rows(xs_ref, ROW_BLOCK).astype(BF16)
        gate = jnp.dot(xb, wgu_bf[:, 0:EXPERT_DIM], preferred_element_type=F32) + bgu_ref[0, :, 0:EXPERT_DIM]
        up = (jnp.dot(xb, wgu_bf[:, EXPERT_DIM:2 * EXPERT_DIM], preferred_element_type=F32)
              + bgu_ref[0, :, EXPERT_DIM:2 * EXPERT_DIM])
        gate = jnp.minimum(gate, SWIGLU_LIMIT)
        up = jnp.clip(up, -SWIGLU_LIMIT, SWIGLU_LIMIT)
        act = gate * jax.nn.sigmoid(SWIGLU_ALPHA * gate) * (up + 1.0)
        res = jnp.dot(act.astype(BF16), wd_bf[...], preferred_element_type=F32) + bd_ref[0]
        row = blk * ROW_BLOCK + lax.broadcasted_iota(I32, (ROW_BLOCK, 1), 0)
        mine = jnp.logical_and(row >= gstart_ref[e], row < gend_ref[e])

        @pl.when(first_visit)
        def _():
            _store_rows(o_ref, jnp.where(mine, res, 0.0))

        @pl.when(jnp.logical_not(first_visit))
        def _():
            _store_rows(o_ref, jnp.where(mine, res, _load_rows(o_ref, ROW_BLOCK)))


def _ffn_call(item_e, item_blk, nact, gstart, gend, xs, wgu, bgu, wd, bd):
    d = D_MODEL
    rows = lambda w, ie, ib, *_: (ib[w], 0)
    exp3 = lambda w, ie, *_: (ie[w], 0, 0)
    return pl.pallas_call(
        _ffn_kernel,
        grid_spec=pltpu.PrefetchScalarGridSpec(
            num_scalar_prefetch=5, grid=(item_e.shape[0],),
            in_specs=[pl.BlockSpec((ROW_BLOCK * ROW_TILES, LANES), rows),
                      pl.BlockSpec((1, d, 2 * EXPERT_DIM), exp3),
                      pl.BlockSpec((1, 1, 2 * EXPERT_DIM), exp3),
                      pl.BlockSpec((1, EXPERT_DIM, d), exp3),
                      pl.BlockSpec((1, 1, d), exp3)],
            out_specs=pl.BlockSpec((ROW_BLOCK * ROW_TILES, LANES), rows),
            scratch_shapes=[pltpu.VMEM((d, 2 * EXPERT_DIM), BF16), pltpu.VMEM((EXPERT_DIM, d), BF16)]),
        out_shape=jax.ShapeDtypeStruct(xs.shape, F32),
        compiler_params=_cparams("arbitrary"), name="ffn",
    )(item_e, item_blk, nact, gstart, gend, xs, wgu, bgu, wd, bd)


def _combine_kernel(slot_ref, slot_next_ref, ys_ref, gate_ref, x_ref, o_ref, buf_ref, sem, *, tm, n_steps):
    i = pl.program_id(0)
    half = i % 2

    def gather(slots, dst_half):
        def issue(it, carry):
            for u in range(ROW_UNROLL):
                j = it * ROW_UNROLL + u
                for k in range(TOP_K):
                    pltpu.make_async_copy(_row_tile(ys_ref, slots[k * tm + j]),
                                          _row_tile(buf_ref.at[dst_half, k], j),
                                          sem.at[dst_half]).start(priority=k % 2)
            return carry
        lax.fori_loop(0, tm // ROW_UNROLL, issue, 0)

    @pl.when(i == 0)
    def _():
        gather(slot_ref, 0)

    @pl.when(i + 1 < n_steps)
    def _():
        gather(slot_next_ref, 1 - half)

    def drain(j, carry):
        for k in range(TOP_K):
            pltpu.make_async_copy(_row_tile(ys_ref, 0), _row_tile(buf_ref.at[half, k], 0),
                                  sem.at[half]).wait()
        return carry

    lax.fori_loop(0, tm, drain, 0)
    gate = gate_ref[...]
    y = x_ref[...]
    for k in range(TOP_K):
        y = y + gate[:, k:k + 1] * _load_rows(buf_ref, tm, half, k)
    o_ref[...] = y


def _combine_call(slots, ys, gates, x1):
    t, d = x1.shape
    tm = ROW_DMA_TOKENS
    n_steps = t // tm
    tile = lambda w: pl.BlockSpec((tm, w), lambda i: (i, 0))
    return pl.pallas_call(
        functools.partial(_combine_kernel, tm=tm, n_steps=n_steps),
        grid=(n_steps,),
        in_specs=[pl.BlockSpec((tm * TOP_K,), lambda i: (i,), memory_space=pltpu.SMEM),
                  pl.BlockSpec((tm * TOP_K,), lambda i: (jnp.minimum(i + 1, n_steps - 1),),
                               memory_space=pltpu.SMEM),
                  pl.BlockSpec(memory_space=pl.ANY), tile(LANES), tile(d)],
        out_specs=tile(d),
        scratch_shapes=[pltpu.VMEM((2, TOP_K, tm * ROW_TILES, LANES), F32), pltpu.SemaphoreType.DMA((2,))],
        out_shape=jax.ShapeDtypeStruct((t, d), F32),
        compiler_params=_cparams("arbitrary"), name="combine",
    )(slots, slots, ys, gates, x1)


def _const_rows(q_g, k_g, i_g, i_b):
    zeros = jnp.zeros((LANES - IDX_DIM,), F32)
    rows = [
        jnp.tile(q_g.astype(F32), LANES // HEAD_DIM),
        jnp.tile(k_g.astype(F32), LANES // HEAD_DIM),
        jnp.concatenate([i_g.astype(F32), zeros]),
        jnp.concatenate([i_b.astype(F32), zeros]),
    ]
    return jnp.concatenate([jnp.stack(rows, axis=0), jnp.zeros((SUBLANES - len(rows), LANES), F32)], axis=0)


def _rope_freqs():
    inv_m = ROPE_THETA ** (-jnp.arange(0, HEAD_DIM, 2, dtype=F32) / HEAD_DIM)
    inv_i = ROPE_THETA ** (-jnp.arange(0, IDX_ROPE_DIM, 2, dtype=F32) / IDX_ROPE_DIM)
    pad = jnp.zeros((ROPE_ROWS - inv_m.shape[0] - inv_i.shape[0],), F32)
    return jnp.concatenate([inv_m, inv_i, pad]).reshape(ROPE_ROWS, 1)


def _ffn_schedule(counts, n_blocks):
    gend = jnp.cumsum(counts).astype(I32)
    gstart = gend - counts
    first_blk = gstart // ROW_BLOCK
    last_blk = (jnp.maximum(gend, 1) - 1) // ROW_BLOCK
    n_items = jnp.where(counts > 0, last_blk - first_blk + 1, 0)
    item_end = jnp.cumsum(n_items).astype(I32)
    item_start = item_end - n_items
    total = item_end[-1]
    w = jnp.minimum(jnp.arange(n_blocks + N_EXPERTS - 1, dtype=I32), total - 1)
    item_e = jnp.sum((item_end[None, :] <= w[:, None]).astype(I32), axis=1)
    mine = item_e[:, None] == jnp.arange(N_EXPERTS, dtype=I32)[None, :]
    item_blk = w + jnp.sum(jnp.where(mine, (first_blk - item_start)[None, :], 0), axis=1)
    return gstart, gend, item_e, item_blk, total.reshape(1)


def _layer(x, positions, norm1_g, w_in, q_norm_g, k_norm_g, idx_k_norm_g, idx_k_norm_b, w_branch_attn,
           pool_mix_w, pool_scale, w_branch_pool, w_out, norm2_g, router_w, router_b, w_gate_up,
           b_gate_up, w_down, b_down):
    b, s, d = x.shape
    t = b * s
    assert (t * TOP_K) % ROW_BLOCK == 0 and s % QUERY_BLOCK == 0
    lane = np.arange(LANES)
    gsum = jnp.asarray(lane[:, None] // HEAD_DIM == lane[None, :] // HEAD_DIM, BF16)

    qt, k, vt, qit, ki, wit, pool, gates = _proj_call(
        x, positions.reshape(b, 1, s), norm1_g.reshape(1, d), w_in,
        _const_rows(q_norm_g, k_norm_g, idx_k_norm_g, idx_k_norm_b), _rope_freqs(), gsum)
    attn = _attn_call(qt, k, vt, qit, ki, wit)
    rw = jnp.pad(router_w, ((0, 0), (0, LANES - N_EXPERTS))).astype(BF16)
    rb = jnp.pad(router_b, (0, LANES - N_EXPERTS)).reshape(1, LANES)
    x1, h2, meta, rgate, counts = _merge_call(
        x, attn, pool, gates, w_branch_attn.astype(BF16), pool_mix_w.astype(BF16),
        pool_scale.reshape(1, POOL_WIDTH), w_branch_pool.astype(BF16), w_out.astype(BF16),
        norm2_g.reshape(1, d), rw, rb)
    x1 = x1.reshape(t, d)

    gstart, gend, item_e, item_blk, nact = _ffn_schedule(counts[0, :N_EXPERTS], t * TOP_K // ROW_BLOCK)
    seg = jnp.sum(jnp.where(meta[None, 0:TOP_K] == jnp.arange(N_EXPERTS, dtype=I32)[:, None, None],
                            gstart[:, None, None], 0), axis=0)
    slots = (seg + meta[TOP_K:2 * TOP_K]).reshape(TOP_K, t // ROW_DMA_TOKENS, ROW_DMA_TOKENS)
    slots = slots.transpose(1, 0, 2).reshape(t * TOP_K)
    xs = _dispatch_call(slots, h2)
    ys = _ffn_call(item_e, item_blk, nact, gstart, gend, xs, w_gate_up,
                   b_gate_up.reshape(N_EXPERTS, 1, 2 * EXPERT_DIM), w_down, b_down.reshape(N_EXPERTS, 1, d))
    out = _combine_call(slots, ys, rgate, x1)
    return out.reshape(b, s, d)


def kernel(x, positions, norm1_g, w_in, q_norm_g, k_norm_g, idx_k_norm_g, idx_k_norm_b, w_branch_attn,
           pool_mix_w, pool_scale, w_branch_pool, w_out, norm2_g, router_w, router_b, w_gate_up,
           b_gate_up, w_down, b_down):
    for l in range(norm1_g.shape[0]):
        x = _layer(x, positions, norm1_g[l], w_in[l], q_norm_g[l], k_norm_g[l], idx_k_norm_g[l],
                   idx_k_norm_b[l], w_branch_attn[l], pool_mix_w[l], pool_scale[l], w_branch_pool[l],
                   w_out[l], norm2_g[l], router_w[l], router_b[l], w_gate_up[l], b_gate_up[l],
                   w_down[l], b_down[l])
    return x
```

```python
import functools

import numpy as np
import jax
import jax.numpy as jnp
from jax import lax
from jax.experimental import pallas as pl
from jax.experimental.pallas import tpu as pltpu

F32 = jnp.float32
BF16 = jnp.bfloat16
I32 = jnp.int32

D_MODEL = 1024
CHUNK = 64
CHUNK_SHIFT = CHUNK.bit_length() - 1
ATTN_HEADS = 8
ATTN_KV_HEADS = 2
HEAD_DIM = 64
GROUPS = ATTN_HEADS // ATTN_KV_HEADS
ATTN_WIDTH = ATTN_HEADS * HEAD_DIM
KV_WIDTH = ATTN_KV_HEADS * HEAD_DIM
ATTN_SCALE = HEAD_DIM ** -0.5
IDX_HEADS = 4
IDX_DIM = 64
IDX_ROPE_DIM = 32
IDX_SCALE = (IDX_HEADS ** -0.5) * (IDX_DIM ** -0.5)
IDX_TOPK_MAX = 256
QUERY_BLOCK = 256
POOL_WINDOWS = (2, 4, 8, 16)
POOL_WIDTH = 512
POOL_GROUP_DIM = 128
POOL_HALO = 16
N_EXPERTS = 32
TOP_K = 4
EXPERT_DIM = 1024
SWIGLU_ALPHA = 1.702
SWIGLU_LIMIT = 7.0
ROPE_THETA = 10000.0
NORM_EPS = 1e-6

LANES = 128
SUBLANES = 8
VREGS = 64
VMEM_LIMIT = 56 * 1024 * 1024
FLT_MAX = float(np.finfo(np.float32).max)
MASKED = -1e30

C_Q = 0
C_K = C_Q + ATTN_WIDTH
C_V = C_K + KV_WIDTH
C_QI = C_V + KV_WIDTH
C_KIW = C_QI + IDX_HEADS * IDX_DIM
C_POOL = C_KIW + LANES
C_GATE = C_POOL + POOL_WIDTH
W_PACKED = C_GATE + 2 * D_MODEL
C_SMALL_END = C_POOL
PACK_COLS = 512

ROW_BLOCK = 512
FOLD_ROWS = 4 * SUBLANES
VALUE_BISECT_ITERS = 16
BISECT_CAP = 64
BISECT_HEAD = 16
BISECT_UNROLL = 2
ATT_CK = 256
TIE_ROWS = 512
VT_ROWS = HEAD_DIM + 2 * SUBLANES
LOG2E = float(np.log2(np.e))
ROPE_ROWS = 64


def _cparams(*sem):
    return pltpu.CompilerParams(dimension_semantics=sem, vmem_limit_bytes=VMEM_LIMIT)


def _proj_kernel(x_ref, pos_ref, g1_ref, wraw_ref, rows_ref, freq_ref, gsum_ref,
                 qt_ref, k_ref, vt_ref, qit_ref, ki_ref, wit_ref, pool_ref, gate_ref, w_ref):
    @pl.when(jnp.logical_and(pl.program_id(0) == 0, pl.program_id(1) == 0))
    def _():
        narrow = IDX_DIM + IDX_HEADS
        w_ref[:, 0:C_KIW] = wraw_ref[:, 0:C_KIW].astype(BF16)
        w_ref[:, C_KIW:C_POOL] = jnp.concatenate(
            [wraw_ref[:, C_KIW:C_KIW + narrow], jnp.zeros((D_MODEL, LANES - narrow), F32)], axis=1).astype(BF16)
        for c0 in range(C_POOL, W_PACKED, PACK_COLS):
            src = c0 - (LANES - narrow)
            w_ref[:, c0:c0 + PACK_COLS] = wraw_ref[:, src:src + PACK_COLS].astype(BF16)

    x = x_ref[0]
    tm = x.shape[0]
    ms = jnp.mean(x * x, axis=-1, keepdims=True)
    h = (x * lax.rsqrt(ms + NORM_EPS) * g1_ref[...]).astype(BF16)
    d1 = jnp.dot(h, w_ref[:, 0:C_SMALL_END], preferred_element_type=F32)
    pool_ref[0] = jnp.dot(h, w_ref[:, C_POOL:C_GATE], preferred_element_type=F32)
    gate_ref[0] = jnp.dot(h, w_ref[:, C_GATE:W_PACKED], preferred_element_type=F32)

    half_m, half_i = HEAD_DIM // 2, IDX_ROPE_DIM // 2
    ang = freq_ref[...] * pos_ref[0].astype(F32)
    cos_t, sin_t = jnp.cos(ang), jnp.sin(ang)
    cm, sm = cos_t[0:half_m], sin_t[0:half_m]
    ci, si = cos_t[half_m:half_m + half_i], sin_t[half_m:half_m + half_i]
    rest_one = jnp.ones((IDX_DIM - IDX_ROPE_DIM, tm), F32)
    rest_zero = jnp.zeros((IDX_DIM - IDX_ROPE_DIM, tm), F32)
    heads = LANES // HEAD_DIM
    cos_m = jnp.concatenate([cm, cm] * heads, axis=0)
    sin_m = jnp.concatenate([-sm, sm] * heads, axis=0)
    cos_i = jnp.concatenate([ci, ci, rest_one] * heads, axis=0)
    sin_i = jnp.concatenate([-si, si, rest_zero] * heads, axis=0)
    rows = rows_ref[...]
    gsum = gsum_ref[...]

    def head_rms(xc, grow):
        sq = xc * xc
        hi = sq.astype(BF16)
        lo = (sq - hi.astype(F32)).astype(BF16)
        ssum = (jnp.dot(hi, gsum, preferred_element_type=F32)
                + jnp.dot(lo, gsum, preferred_element_type=F32))
        return xc * lax.rsqrt(ssum * (1.0 / HEAD_DIM) + NORM_EPS) * grow

    def rope_t(xt, cos_tab, sin_tab, half, width):
        parts = []
        for h0 in range(0, LANES, width):
            parts += [xt[h0 + half:h0 + 2 * half], xt[h0:h0 + half]]
            if 2 * half < width:
                parts.append(xt[h0 + 2 * half:h0 + width])
        return xt * cos_tab + jnp.concatenate(parts, axis=0) * sin_tab

    for c in range(ATTN_WIDTH // LANES):
        qc = d1[:, C_Q + c * LANES:C_Q + (c + 1) * LANES]
        qr = rope_t(head_rms(qc, rows[0:1]).T, cos_m, sin_m, half_m, HEAD_DIM) * (ATTN_SCALE * LOG2E)
        qt_ref[0, c * LANES:(c + 1) * LANES, :] = qr.astype(BF16)

    kr = rope_t(head_rms(d1[:, C_K:C_K + KV_WIDTH], rows[1:2]).T, cos_m, sin_m, half_m, HEAD_DIM).T
    vt = d1[:, C_V:C_V + KV_WIDTH].T
    for r in range(ATTN_KV_HEADS):
        k_ref[0, r] = kr[:, r * HEAD_DIM:(r + 1) * HEAD_DIM].astype(BF16)
        vt_ref[0, r, 0:HEAD_DIM, :] = vt[r * HEAD_DIM:(r + 1) * HEAD_DIM].astype(BF16)
        vt_ref[0, r, HEAD_DIM:VT_ROWS, :] = jnp.ones((VT_ROWS - HEAD_DIM, tm), BF16)

    for c in range(IDX_HEADS * IDX_DIM // LANES):
        qc = d1[:, C_QI + c * LANES:C_QI + (c + 1) * LANES]
        qit_ref[0, c * LANES:(c + 1) * LANES, :] = rope_t(qc.T, cos_i, sin_i, half_i, IDX_DIM).astype(BF16)

    kiw = d1[:, C_KIW:C_KIW + LANES]
    in_ki = lax.broadcasted_iota(I32, (tm, LANES), 1) < IDX_DIM
    mu = jnp.sum(jnp.where(in_ki, kiw, 0.0), axis=-1, keepdims=True) * (1.0 / IDX_DIM)
    dv = jnp.where(in_ki, kiw - mu, 0.0)
    var = jnp.sum(dv * dv, axis=-1, keepdims=True) * (1.0 / IDX_DIM)
    kin = dv * lax.rsqrt(var + NORM_EPS) * rows[2:3] + rows[3:4]
    ki_ref[0] = rope_t(kin.T, cos_i, sin_i, half_i, IDX_DIM).T[:, 0:IDX_DIM].astype(BF16)
    wit_ref[0] = (kiw * IDX_SCALE).T[IDX_DIM:IDX_DIM + SUBLANES]


def _proj_call(x, pos_rows, g1, w_in, rows, freq, gsum):
    b, s, d = x.shape
    tm = min(512, s)
    grid = (b, s // tm)
    full2 = lambda i, j: (0, 0)
    out_shape = (
        jax.ShapeDtypeStruct((b, ATTN_WIDTH, s), BF16),
        jax.ShapeDtypeStruct((b, ATTN_KV_HEADS, s, HEAD_DIM), BF16),
        jax.ShapeDtypeStruct((b, ATTN_KV_HEADS, VT_ROWS, s), BF16),
        jax.ShapeDtypeStruct((b, IDX_HEADS * IDX_DIM, s), BF16),
        jax.ShapeDtypeStruct((b, s, IDX_DIM), BF16),
        jax.ShapeDtypeStruct((b, SUBLANES, s), F32),
        jax.ShapeDtypeStruct((b, s, POOL_WIDTH), F32),
        jax.ShapeDtypeStruct((b, s, 2 * D_MODEL), F32),
    )
    in_specs = [
        pl.BlockSpec((1, tm, d), lambda i, j: (i, j, 0)),
        pl.BlockSpec((1, 1, tm), lambda i, j: (i, 0, j)),
        pl.BlockSpec((1, d), full2),
        pl.BlockSpec(w_in.shape, full2, pipeline_mode=pl.Buffered(1)),
        pl.BlockSpec((SUBLANES, LANES), full2),
        pl.BlockSpec((ROPE_ROWS, 1), full2),
        pl.BlockSpec((LANES, LANES), full2),
    ]
    out_specs = (
        pl.BlockSpec((1, ATTN_WIDTH, tm), lambda i, j: (i, 0, j)),
        pl.BlockSpec((1, ATTN_KV_HEADS, tm, HEAD_DIM), lambda i, j: (i, 0, j, 0)),
        pl.BlockSpec((1, ATTN_KV_HEADS, VT_ROWS, tm), lambda i, j: (i, 0, 0, j)),
        pl.BlockSpec((1, IDX_HEADS * IDX_DIM, tm), lambda i, j: (i, 0, j)),
        pl.BlockSpec((1, tm, IDX_DIM), lambda i, j: (i, j, 0)),
        pl.BlockSpec((1, SUBLANES, tm), lambda i, j: (i, 0, j)),
        pl.BlockSpec((1, tm, POOL_WIDTH), lambda i, j: (i, j, 0)),
        pl.BlockSpec((1, tm, 2 * D_MODEL), lambda i, j: (i, j, 0)),
    )
    return pl.pallas_call(
        _proj_kernel, grid=grid, in_specs=in_specs, out_specs=out_specs, out_shape=out_shape,
        scratch_shapes=[pltpu.VMEM((d, W_PACKED), BF16)],
        compiler_params=_cparams("arbitrary", "arbitrary"), name="proj",
    )(x, pos_rows, g1, w_in, rows, freq, gsum)


def _sort_key(x):
    b = lax.bitcast_convert_type(x, I32)
    return b ^ ((b >> 31) & 0x7FFFFFFF)


def _unsort_key(k):
    return lax.bitcast_convert_type(k ^ ((k >> 31) & 0x7FFFFFFF), F32)


def _attn_kernel(qt_ref, qit_ref, wit_ref, ki_ref, k_ref, vt_ref, tri_ref, o_ref, sc_ref, sa_ref, sb_ref,
                 *, n_sel, ck):
    qb = QUERY_BLOCK
    t0 = pl.program_id(1) * qb
    nck = (t0 + qb + ck - 1) // ck
    qpos = t0 + lax.broadcasted_iota(I32, (1, qb), 1)
    cend = ((qpos >> CHUNK_SHIFT) + 1) * CHUNK
    key_ck = lax.broadcasted_iota(I32, (ck, qb), 0)
    k_sel = float(n_sel)
    cnt_rows = min(ck, VREGS * SUBLANES * LANES // qb)

    def fold(x, op):
        return op(x.reshape(ck // FOLD_ROWS, FOLD_ROWS, qb), axis=0)

    qit = qit_ref[0]
    rhs_i = jnp.concatenate([qit[h * IDX_DIM:(h + 1) * IDX_DIM] for h in range(IDX_HEADS)], axis=1)
    wit = wit_ref[0]

    def score_body(c, carry):
        mxp, mnp, c0p, c1p = carry
        off = pl.multiple_of(c * ck, ck)
        lg = jnp.dot(ki_ref[0, pl.ds(off, ck), :], rhs_i, preferred_element_type=F32)
        sc = jnp.maximum(lg[:, 0:qb], 0.0) * wit[0:1]
        for h in range(1, IDX_HEADS):
            sc = sc + jnp.maximum(lg[:, h * qb:(h + 1) * qb], 0.0) * wit[h:h + 1]
        adm = (off + key_ck) < cend
        sc = jnp.where(adm, sc, -jnp.inf)
        sc_ref[pl.ds(off, ck), :] = sc
        mxp = jnp.maximum(mxp, fold(sc, jnp.max))
        mnp = jnp.minimum(mnp, fold(jnp.where(adm, sc, jnp.inf), jnp.min))
        c0p = c0p + fold(jnp.where(sc >= 0.0, 1.0, 0.0), jnp.sum)
        c1p = c1p + fold(jnp.where(sc > 0.0, 1.0, 0.0), jnp.sum)
        return mxp, mnp, c0p, c1p

    init = (jnp.full((FOLD_ROWS, qb), -jnp.inf, F32), jnp.full((FOLD_ROWS, qb), jnp.inf, F32),
            jnp.zeros((FOLD_ROWS, qb), F32), jnp.zeros((FOLD_ROWS, qb), F32))
    mxp, mnp, c0p, c1p = lax.fori_loop(0, nck, score_body, init)
    mx = jnp.max(mxp, axis=0, keepdims=True)
    mn = jnp.min(mnp, axis=0, keepdims=True)
    c0 = jnp.sum(c0p, axis=0, keepdims=True)
    c1 = jnp.sum(c1p, axis=0, keepdims=True)

    def count(pred):
        def body(c, acc):
            for sb in range(ck // cnt_rows):
                off = pl.multiple_of(c * ck + sb * cnt_rows, cnt_rows)
                hit = pred(sc_ref[pl.ds(off, cnt_rows), :], off + key_ck[0:cnt_rows])
                acc = acc + jnp.sum(jnp.where(hit, 1.0, 0.0).reshape(cnt_rows // FOLD_ROWS, FOLD_ROWS, qb),
                                    axis=0)
            return acc
        acc = lax.fori_loop(0, nck, body, jnp.zeros((FOLD_ROWS, qb), F32))
        return jnp.sum(acc, axis=0, keepdims=True)

    small = cend.astype(F32) <= k_sel
    at_zero = jnp.logical_and(jnp.logical_not(small), jnp.logical_and(c1 < k_sel, c0 >= k_sel))
    positive = jnp.logical_and(jnp.logical_not(small), c1 >= k_sel)
    lo0 = jnp.where(positive, 0.0, mn)
    hi0 = jnp.where(positive, jnp.minimum(2.0 * mx, FLT_MAX), 0.0)
    thr0 = jnp.where(small, -FLT_MAX, 0.0)
    done0 = jnp.where(jnp.logical_or(small, at_zero), 1.0, 0.0)
    tie0 = jnp.where(jnp.logical_and(at_zero, c0 > k_sel), 1.0, 0.0)
    chi0 = jnp.where(positive, 0.0, c0)
    left0 = jnp.sum(1.0 - done0)

    def bisect_cond(st):
        it, left = st[0], st[1]
        return jnp.logical_and(it < BISECT_CAP, left > 0.0)

    def bisect_step(it, lo, hi, thr, done, tie, chi):
        mid_v = 0.5 * lo + 0.5 * hi
        klo = _sort_key(lo)
        khi = _sort_key(hi)
        mid_k = _unsort_key((klo & khi) + ((klo ^ khi) >> 1))
        mid = jnp.where(it < VALUE_BISECT_ITERS, mid_v, mid_k)
        stuck = jnp.logical_or(mid <= lo, mid >= hi)
        cnt = count(lambda s, _: s >= mid)
        active = done == 0.0
        moving = jnp.logical_and(active, jnp.logical_not(stuck))
        hit = jnp.logical_and(moving, cnt == k_sel)
        new_tie = jnp.logical_and(active, stuck)
        thr = jnp.where(hit, mid, jnp.where(new_tie, lo, thr))
        tie = jnp.where(new_tie, 1.0, tie)
        done = jnp.where(jnp.logical_or(hit, new_tie), 1.0, done)
        upd = jnp.logical_and(moving, jnp.logical_not(hit))
        lo = jnp.where(jnp.logical_and(upd, cnt >= k_sel), mid, lo)
        lower = jnp.logical_and(upd, cnt < k_sel)
        hi = jnp.where(lower, mid, hi)
        chi = jnp.where(lower, cnt, chi)
        return lo, hi, thr, done, tie, chi

    def bisect_body(steps, st):
        it, _, lo, hi, thr, done, tie, chi = st
        for u in range(steps):
            lo, hi, thr, done, tie, chi = bisect_step(it + u, lo, hi, thr, done, tie, chi)
        return it + steps, jnp.sum(1.0 - done), lo, hi, thr, done, tie, chi

    st = (jnp.int32(0), left0, lo0, hi0, thr0, done0, tie0, chi0)
    st = lax.while_loop(lambda s: jnp.logical_and(s[0] < BISECT_HEAD, s[1] > 0.0),
                        functools.partial(bisect_body, BISECT_HEAD), st)
    st = lax.while_loop(bisect_cond, functools.partial(bisect_body, BISECT_UNROLL), st)
    thr, tie = st[4], st[6]
    n_above = jnp.where(at_zero, c1, st[7])

    @pl.when(jnp.sum(tie) > 0.0)
    def _():
        need = jnp.where(tie > 0.0, k_sel - n_above, FLT_MAX)

        def drop_body(c, before):
            tr = tri_ref.shape[0]
            for sb in range(ck // tr):
                off = pl.multiple_of(c * ck + sb * tr, tr)
                blk = sc_ref[pl.ds(off, tr), :]
                eq = blk == thr
                eqf = jnp.where(eq, 1.0, 0.0)
                rank = before + jnp.dot(tri_ref[...], eqf.astype(BF16), preferred_element_type=F32)
                sc_ref[pl.ds(off, tr), :] = jnp.where(jnp.logical_and(eq, rank >= need), -jnp.inf, blk)
                before = before + jnp.sum(jnp.sum(eqf.reshape(tr // FOLD_ROWS, FOLD_ROWS, qb), axis=0),
                                          axis=0, keepdims=True)
            return before

        lax.fori_loop(0, nck, drop_body, jnp.zeros((1, qb), F32))

    qt = qt_ref[0]
    rhs = [jnp.concatenate(
        [qt[(r * GROUPS + g) * HEAD_DIM:(r * GROUPS + g + 1) * HEAD_DIM] for g in range(GROUPS)],
        axis=1) for r in range(ATTN_KV_HEADS)]

    n_att = (t0 + qb + ATT_CK - 1) // ATT_CK

    def step_offset(step):
        return pl.multiple_of(jnp.minimum(step, n_att - 1) * ATT_CK, ATT_CK)

    def qk_logits(step, buf):
        off = step_offset(step)
        thr_step = jnp.where(step < n_att, thr, jnp.inf)
        bias = jnp.where(sc_ref[pl.ds(off, ATT_CK), :] >= thr_step, 0.0, MASKED)
        for r in range(ATTN_KV_HEADS):
            lg = jnp.dot(k_ref[0, r, pl.ds(off, ATT_CK), :], rhs[r], preferred_element_type=F32)
            for g in range(GROUPS):
                buf[r, :, g * qb:(g + 1) * qb] = lg[:, g * qb:(g + 1) * qb] + bias

    def softmax_pv(step, buf, carry):
        off = step_offset(step)
        out = []
        for r in range(ATTN_KV_HEADS):
            m, acc = carry[r]
            m_new = jnp.maximum(m, jnp.max(buf[r], axis=0, keepdims=True))
            p = jnp.exp2(buf[r] - m_new).astype(BF16)
            pv = jnp.dot(vt_ref[0, r, :, pl.ds(off, ATT_CK)], p, preferred_element_type=F32)
            out.append((m_new, jnp.exp2(m - m_new) * acc + pv))
        return tuple(out)

    def att_body(i, carry):
        qk_logits(2 * i + 1, sb_ref)
        carry = softmax_pv(2 * i, sa_ref, carry)
        qk_logits(2 * i + 2, sa_ref)
        return softmax_pv(2 * i + 1, sb_ref, carry)

    init_a = tuple((jnp.full((1, GROUPS * qb), MASKED, F32), jnp.zeros((VT_ROWS, GROUPS * qb), F32))
                   for _ in range(ATTN_KV_HEADS))
    qk_logits(0, sa_ref)
    fin = lax.fori_loop(0, (n_att + 1) // 2, att_body, init_a)
    outs = []
    for r in range(ATTN_KV_HEADS):
        acc = fin[r][1]
        o = acc[0:HEAD_DIM] / acc[HEAD_DIM:HEAD_DIM + 1]
        outs.extend(o[:, g * qb:(g + 1) * qb] for g in range(GROUPS))
    o_ref[0] = jnp.concatenate(outs, axis=0).T.astype(BF16)


def _attn_call(qt, k, vt, qit, ki, wit):
    b, _, s = qt.shape
    n_sel = min(IDX_TOPK_MAX, s // 4)
    ck = min(512, s)
    grid = (b, s // QUERY_BLOCK)
    in_specs = [
        pl.BlockSpec((1, ATTN_WIDTH, QUERY_BLOCK), lambda i, j: (i, 0, j)),
        pl.BlockSpec((1, IDX_HEADS * IDX_DIM, QUERY_BLOCK), lambda i, j: (i, 0, j)),
        pl.BlockSpec((1, SUBLANES, QUERY_BLOCK), lambda i, j: (i, 0, j)),
        pl.BlockSpec((1, s, IDX_DIM), lambda i, j: (i, 0, 0)),
        pl.BlockSpec((1, ATTN_KV_HEADS, s, HEAD_DIM), lambda i, j: (i, 0, 0, 0)),
        pl.BlockSpec((1, ATTN_KV_HEADS, VT_ROWS, s), lambda i, j: (i, 0, 0, 0)),
        pl.BlockSpec((min(TIE_ROWS, ck), min(TIE_ROWS, ck)), lambda i, j: (0, 0)),
    ]
    tri = jnp.asarray(np.tril(np.ones((min(TIE_ROWS, ck), min(TIE_ROWS, ck)), np.float32), -1), BF16)
    return pl.pallas_call(
        functools.partial(_attn_kernel, n_sel=n_sel, ck=ck),
        grid=grid, in_specs=in_specs,
        out_specs=pl.BlockSpec((1, QUERY_BLOCK, ATTN_WIDTH), lambda i, j: (i, j, 0)),
        out_shape=jax.ShapeDtypeStruct((b, s, ATTN_WIDTH), BF16),
        scratch_shapes=[pltpu.VMEM((s, QUERY_BLOCK), F32),
                        pltpu.VMEM((ATTN_KV_HEADS, ATT_CK, GROUPS * QUERY_BLOCK), F32),
                        pltpu.VMEM((ATTN_KV_HEADS, ATT_CK, GROUPS * QUERY_BLOCK), F32)],
        compiler_params=_cparams("parallel", "parallel"), name="attn",
    )(qt, qit, wit, ki, k, vt, tri)


def _merge_kernel(x_ref, attn_ref, pool_ref, gate_ref, wba_ref, mix_ref, pscale_ref, wbp_ref, wout_ref,
                  g2_ref, rw_ref, rb_ref, tri_ref,
                  o_ref, h_ref, meta_ref, rgate_ref, cnt_ref, ext_ref, carry_ref):
    tm = x_ref.shape[1]
    j = pl.program_id(1)

    @pl.when(j == 0)
    def _():
        ext_ref[0:POOL_HALO] = jnp.zeros((POOL_HALO, POOL_WIDTH), F32)

    @pl.when(j > 0)
    def _():
        ext_ref[0:POOL_HALO] = ext_ref[tm:tm + POOL_HALO]

    ext_ref[POOL_HALO:POOL_HALO + tm] = pool_ref[0]
    t = j * tm + lax.broadcasted_iota(I32, (tm, 1), 0)
    mixed = []
    for g, w in enumerate(POOL_WINDOWS):
        cols = slice(g * POOL_GROUP_DIM, (g + 1) * POOL_GROUP_DIM)
        cur = ext_ref[POOL_HALO:POOL_HALO + tm, cols]
        wsum = cur
        for i in range(1, w):
            wsum = wsum + ext_ref[POOL_HALO - i:POOL_HALO - i + tm, cols]
        cnt = jnp.minimum(t + 1, w).astype(F32)
        dev = (wsum / cnt - cur).astype(BF16)
        mixed.append(jnp.dot(dev, mix_ref[g], preferred_element_type=F32))
    pooled = (jnp.concatenate(mixed, axis=1) * pscale_ref[...]).astype(BF16)
    branch_a = jnp.dot(attn_ref[0], wba_ref[...], preferred_element_type=F32)
    branch_p = jnp.dot(pooled, wbp_ref[...], preferred_element_type=F32)
    gates = gate_ref[0]
    merged = (jax.nn.sigmoid(gates[:, 0:D_MODEL]) * branch_a
              + jax.nn.sigmoid(gates[:, D_MODEL:2 * D_MODEL]) * branch_p)
    x1 = x_ref[0] + jnp.dot(merged.astype(BF16), wout_ref[...], preferred_element_type=F32)
    o_ref[0] = x1
    _route_tile(x1, jnp.logical_and(pl.program_id(0) == 0, j == 0), g2_ref, rw_ref, rb_ref, tri_ref,
                h_ref, meta_ref, rgate_ref, cnt_ref, carry_ref)


def _merge_call(x, attn, pool, gates, wba, mix, pscale, wbp, wout, g2, rw, rb):
    b, s, d = x.shape
    tm = min(512, s)
    nt = s // tm
    t = b * s
    tile = lambda w: pl.BlockSpec((1, tm, w), lambda i, j: (i, j, 0))
    flat = lambda rows, w: pl.BlockSpec((rows, w), lambda i, j: (i * nt + j, 0))
    full2 = lambda i, j: (0, 0)
    tri = jnp.asarray(np.tril(np.ones((tm, tm), np.float32), -1), BF16)
    in_specs = [
        tile(d), tile(ATTN_WIDTH), tile(POOL_WIDTH), tile(2 * D_MODEL),
        pl.BlockSpec((ATTN_WIDTH, d), full2),
        pl.BlockSpec((len(POOL_WINDOWS), POOL_GROUP_DIM, POOL_GROUP_DIM), lambda i, j: (0, 0, 0)),
        pl.BlockSpec((1, POOL_WIDTH), full2),
        pl.BlockSpec((POOL_WIDTH, d), full2),
        pl.BlockSpec((d, d), full2),
        pl.BlockSpec((1, d), full2),
        pl.BlockSpec((d, LANES), full2),
        pl.BlockSpec((1, LANES), full2),
        pl.BlockSpec((tm, tm), full2),
    ]
    out_shape = (
        jax.ShapeDtypeStruct((b, s, d), F32),
        jax.ShapeDtypeStruct((t * ROW_TILES, LANES), F32),
        jax.ShapeDtypeStruct((2 * TOP_K, t), I32),
        jax.ShapeDtypeStruct((t, LANES), F32),
        jax.ShapeDtypeStruct((1, LANES), I32),
    )
    out_specs = (
        tile(d), flat(tm * ROW_TILES, LANES),
        pl.BlockSpec((2 * TOP_K, tm), lambda i, j: (0, i * nt + j)),
        flat(tm, LANES), pl.BlockSpec((1, LANES), full2),
    )
    return pl.pallas_call(
        _merge_kernel, grid=(b, nt), in_specs=in_specs, out_specs=out_specs, out_shape=out_shape,
        scratch_shapes=[pltpu.VMEM((POOL_HALO + tm, POOL_WIDTH), F32), pltpu.VMEM((1, LANES), F32)],
        compiler_params=_cparams("arbitrary", "arbitrary"), name="merge",
    )(x, attn, pool, gates, wba, mix, pscale, wbp, wout, g2, rw, rb, tri)


ROW_TILES = D_MODEL // LANES


def _load_rows(ref, n, *lead):
    return jnp.concatenate(
        [ref[(*lead, pl.ds(c, n, stride=ROW_TILES), slice(None))] for c in range(ROW_TILES)], axis=1)


def _store_rows(ref, val):
    for c in range(ROW_TILES):
        ref[pl.ds(c, val.shape[0], stride=ROW_TILES), :] = val[:, c * LANES:(c + 1) * LANES]


def _row_tile(ref, i):
    return ref.at[pl.ds(pl.multiple_of(i * ROW_TILES, ROW_TILES), ROW_TILES)]


def _route_tile(x, first_tile, g2_ref, rw_ref, rb_ref, tri_ref, h_ref, meta_ref, gate_ref, cnt_ref, carry_ref):
    tm = x.shape[0]

    @pl.when(first_tile)
    def _():
        carry_ref[...] = jnp.zeros((1, LANES), F32)

    ms = jnp.mean(x * x, axis=-1, keepdims=True)
    h = x * lax.rsqrt(ms + NORM_EPS) * g2_ref[...]
    _store_rows(h_ref, h)
    logits = jnp.dot(h.astype(BF16), rw_ref[...], preferred_element_type=F32) + rb_ref[...]
    lane = lax.broadcasted_iota(I32, (tm, LANES), 1).astype(F32)
    work = jnp.where(lane < N_EXPERTS, logits, -jnp.inf)
    vals, idxs = [], []
    for _ in range(TOP_K):
        m = jnp.max(work, axis=-1, keepdims=True)
        idx = jnp.min(jnp.where(work == m, lane, float(LANES)), axis=-1, keepdims=True)
        vals.append(m)
        idxs.append(idx)
        work = jnp.where(lane == idx, -jnp.inf, work)
    exps = [jnp.exp(v - vals[0]) for v in vals]
    denom = exps[0] + exps[1] + exps[2] + exps[3]
    member = jnp.zeros((tm, LANES), F32)
    for idx in idxs:
        member = member + jnp.where(lane == idx, 1.0, 0.0)
    before = jnp.dot(tri_ref[...], member.astype(BF16), preferred_element_type=F32) + carry_ref[...]
    meta = jnp.zeros((tm, LANES), F32)
    gate = jnp.zeros((tm, LANES), F32)
    for k in range(TOP_K):
        rank = jnp.sum(jnp.where(lane == idxs[k], before, 0.0), axis=-1, keepdims=True)
        meta = jnp.where(lane == float(k), idxs[k], meta)
        meta = jnp.where(lane == float(TOP_K + k), rank, meta)
        gate = jnp.where(lane == float(k), exps[k] / denom, gate)
    meta_ref[...] = meta.T[0:2 * TOP_K].astype(I32)
    gate_ref[...] = gate
    total = carry_ref[...] + jnp.sum(member, axis=0, keepdims=True)
    carry_ref[...] = total
    cnt_ref[...] = total.astype(I32)


ROW_UNROLL = SUBLANES
ROW_DMA_TOKENS = 256


def _dispatch_kernel(slot_ref, h_ref, xs_ref, sem, *, tm):
    def issue(i, carry):
        for u in range(ROW_UNROLL):
            j = i * ROW_UNROLL + u
            for k in range(TOP_K):
                pltpu.make_async_copy(_row_tile(h_ref, j), _row_tile(xs_ref, slot_ref[k * tm + j]),
                                      sem).start(priority=k % 2)
        return carry

    lax.fori_loop(0, tm // ROW_UNROLL, issue, 0)

    for k in range(TOP_K):
        pltpu.make_async_copy(h_ref, xs_ref.at[pl.ds(0, tm * ROW_TILES)], sem).wait()


def _dispatch_call(slots, h2):
    t = h2.shape[0] // ROW_TILES
    tm = ROW_DMA_TOKENS
    return pl.pallas_call(
        functools.partial(_dispatch_kernel, tm=tm),
        grid=(t // tm,),
        in_specs=[pl.BlockSpec((tm * TOP_K,), lambda i: (i,), memory_space=pltpu.SMEM),
                  pl.BlockSpec((tm * ROW_TILES, LANES), lambda i: (i, 0))],
        out_specs=pl.BlockSpec(memory_space=pl.ANY),
        scratch_shapes=[pltpu.SemaphoreType.DMA(())],
        out_shape=jax.ShapeDtypeStruct((t * TOP_K * ROW_TILES, LANES), F32),
        compiler_params=_cparams("arbitrary"), name="dispatch",
    )(slots, h2)


def _ffn_kernel(item_e_ref, item_blk_ref, nact_ref, gstart_ref, gend_ref,
                xs_ref, wgu_ref, bgu_ref, wd_ref, bd_ref, o_ref, wgu_bf, wd_bf):
    w = pl.program_id(0)
    e = item_e_ref[w]
    blk = item_blk_ref[w]
    prev = jnp.maximum(w - 1, 0)
    active = w < nact_ref[0]
    new_expert = jnp.logical_or(w == 0, e != item_e_ref[prev])
    first_visit = jnp.logical_or(w == 0, blk != item_blk_ref[prev])

    @pl.when(jnp.logical_and(active, new_expert))
    def _():
        wgu_bf[...] = wgu_ref[0].astype(BF16)
        wd_bf[...] = wd_ref[0].astype(BF16)

    @pl.when(active)
    def _():
        xb = _load_rows(xs_ref, ROW_BLOCK).astype(BF16)
        gate = jnp.dot(xb, wgu_bf[:, 0:EXPERT_DIM], preferred_element_type=F32) + bgu_ref[0, :, 0:EXPERT_DIM]
        up = (jnp.dot(xb, wgu_bf[:, EXPERT_DIM:2 * EXPERT_DIM], preferred_element_type=F32)
              + bgu_ref[0, :, EXPERT_DIM:2 * EXPERT_DIM])
        gate = jnp.minimum(gate, SWIGLU_LIMIT)
        up = jnp.clip(up, -SWIGLU_LIMIT, SWIGLU_LIMIT)
        act = gate * jax.nn.sigmoid(SWIGLU_ALPHA * gate) * (up + 1.0)
        res = jnp.dot(act.astype(BF16), wd_bf[...], preferred_element_type=F32) + bd_ref[0]
        row = blk * ROW_BLOCK + lax.broadcasted_iota(I32, (ROW_BLOCK, 1), 0)
        mine = jnp.logical_and(row >= gstart_ref[e], row < gend_ref[e])

        @pl.when(first_visit)
        def _():
            _store_rows(o_ref, jnp.where(mine, res, 0.0))

        @pl.when(jnp.logical_not(first_visit))
        def _():
            _store_rows(o_ref, jnp.where(mine, res, _load_rows(o_ref, ROW_BLOCK)))


def _ffn_call(item_e, item_blk, nact, gstart, gend, xs, wgu, bgu, wd, bd):
    d = D_MODEL
    rows = lambda w, ie, ib, *_: (ib[w], 0)
    exp3 = lambda w, ie, *_: (ie[w], 0, 0)
    return pl.pallas_call(
        _ffn_kernel,
        grid_spec=pltpu.PrefetchScalarGridSpec(
            num_scalar_prefetch=5, grid=(item_e.shape[0],),
            in_specs=[pl.BlockSpec((ROW_BLOCK * ROW_TILES, LANES), rows),
                      pl.BlockSpec((1, d, 2 * EXPERT_DIM), exp3),
                      pl.BlockSpec((1, 1, 2 * EXPERT_DIM), exp3),
                      pl.BlockSpec((1, EXPERT_DIM, d), exp3),
                      pl.BlockSpec((1, 1, d), exp3)],
            out_specs=pl.BlockSpec((ROW_BLOCK * ROW_TILES, LANES), rows),
            scratch_shapes=[pltpu.VMEM((d, 2 * EXPERT_DIM), BF16), pltpu.VMEM((EXPERT_DIM, d), BF16)]),
        out_shape=jax.ShapeDtypeStruct(xs.shape, F32),
        compiler_params=_cparams("arbitrary"), name="ffn",
    )(item_e, item_blk, nact, gstart, gend, xs, wgu, bgu, wd, bd)


def _combine_kernel(slot_ref, slot_next_ref, ys_ref, gate_ref, x_ref, o_ref, buf_ref, sem, *, tm, n_steps):
    i = pl.program_id(0)
    half = i % 2

    def gather(slots, dst_half):
        def issue(it, carry):
            for u in range(ROW_UNROLL):
                j = it * ROW_UNROLL + u
                for k in range(TOP_K):
                    pltpu.make_async_copy(_row_tile(ys_ref, slots[k * tm + j]),
                                          _row_tile(buf_ref.at[dst_half, k], j),
                                          sem.at[dst_half]).start(priority=k % 2)
            return carry
        lax.fori_loop(0, tm // ROW_UNROLL, issue, 0)

    @pl.when(i == 0)
    def _():
        gather(slot_ref, 0)

    @pl.when(i + 1 < n_steps)
    def _():
        gather(slot_next_ref, 1 - half)

    for k in range(TOP_K):
        pltpu.make_async_copy(ys_ref.at[pl.ds(0, tm * ROW_TILES)], buf_ref.at[half, k], sem.at[half]).wait()
    gate = gate_ref[...]
    y = x_ref[...]
    for k in range(TOP_K):
        y = y + gate[:, k:k + 1] * _load_rows(buf_ref, tm, half, k)
    o_ref[...] = y


def _combine_call(slots, ys, gates, x1):
    t, d = x1.shape
    tm = ROW_DMA_TOKENS
    n_steps = t // tm
    tile = lambda w: pl.BlockSpec((tm, w), lambda i: (i, 0))
    return pl.pallas_call(
        functools.partial(_combine_kernel, tm=tm, n_steps=n_steps),
        grid=(n_steps,),
        in_specs=[pl.BlockSpec((tm * TOP_K,), lambda i: (i,), memory_space=pltpu.SMEM),
                  pl.BlockSpec((tm * TOP_K,), lambda i: (jnp.minimum(i + 1, n_steps - 1),),
                               memory_space=pltpu.SMEM),
                  pl.BlockSpec(memory_space=pl.ANY), tile(LANES), tile(d)],
        out_specs=tile(d),
        scratch_shapes=[pltpu.VMEM((2, TOP_K, tm * ROW_TILES, LANES), F32), pltpu.SemaphoreType.DMA((2,))],
        out_shape=jax.ShapeDtypeStruct((t, d), F32),
        compiler_params=_cparams("arbitrary"), name="combine",
    )(slots, slots, ys, gates, x1)


def _const_rows(q_g, k_g, i_g, i_b):
    zeros = jnp.zeros((LANES - IDX_DIM,), F32)
    rows = [
        jnp.tile(q_g.astype(F32), LANES // HEAD_DIM),
        jnp.tile(k_g.astype(F32), LANES // HEAD_DIM),
        jnp.concatenate([i_g.astype(F32), zeros]),
        jnp.concatenate([i_b.astype(F32), zeros]),
    ]
    return jnp.concatenate([jnp.stack(rows, axis=0), jnp.zeros((SUBLANES - len(rows), LANES), F32)], axis=0)


def _rope_freqs():
    inv_m = ROPE_THETA ** (-jnp.arange(0, HEAD_DIM, 2, dtype=F32) / HEAD_DIM)
    inv_i = ROPE_THETA ** (-jnp.arange(0, IDX_ROPE_DIM, 2, dtype=F32) / IDX_ROPE_DIM)
    pad = jnp.zeros((ROPE_ROWS - inv_m.shape[0] - inv_i.shape[0],), F32)
    return jnp.concatenate([inv_m, inv_i, pad]).reshape(ROPE_ROWS, 1)


def _ffn_schedule(counts, n_blocks):
    gend = jnp.cumsum(counts).astype(I32)
    gstart = gend - counts
    first_blk = gstart // ROW_BLOCK
    last_blk = (jnp.maximum(gend, 1) - 1) // ROW_BLOCK
    n_items = jnp.where(counts > 0, last_blk - first_blk + 1, 0)
    item_end = jnp.cumsum(n_items).astype(I32)
    item_start = item_end - n_items
    total = item_end[-1]
    w = jnp.minimum(jnp.arange(n_blocks + N_EXPERTS - 1, dtype=I32), total - 1)
    item_e = jnp.sum((item_end[None, :] <= w[:, None]).astype(I32), axis=1)
    mine = item_e[:, None] == jnp.arange(N_EXPERTS, dtype=I32)[None, :]
    item_blk = w + jnp.sum(jnp.where(mine, (first_blk - item_start)[None, :], 0), axis=1)
    return gstart, gend, item_e, item_blk, total.reshape(1)


def _layer(x, positions, norm1_g, w_in, q_norm_g, k_norm_g, idx_k_norm_g, idx_k_norm_b, w_branch_attn,
           pool_mix_w, pool_scale, w_branch_pool, w_out, norm2_g, router_w, router_b, w_gate_up,
           b_gate_up, w_down, b_down):
    b, s, d = x.shape
    t = b * s
    assert (t * TOP_K) % ROW_BLOCK == 0 and s % QUERY_BLOCK == 0
    lane = np.arange(LANES)
    gsum = jnp.asarray(lane[:, None] // HEAD_DIM == lane[None, :] // HEAD_DIM, BF16)

    qt, k, vt, qit, ki, wit, pool, gates = _proj_call(
        x, positions.reshape(b, 1, s), norm1_g.reshape(1, d), w_in,
        _const_rows(q_norm_g, k_norm_g, idx_k_norm_g, idx_k_norm_b), _rope_freqs(), gsum)
    attn = _attn_call(qt, k, vt, qit, ki, wit)
    rw = jnp.pad(router_w, ((0, 0), (0, LANES - N_EXPERTS))).astype(BF16)
    rb = jnp.pad(router_b, (0, LANES - N_EXPERTS)).reshape(1, LANES)
    x1, h2, meta, rgate, counts = _merge_call(
        x, attn, pool, gates, w_branch_attn.astype(BF16), pool_mix_w.astype(BF16),
        pool_scale.reshape(1, POOL_WIDTH), w_branch_pool.astype(BF16), w_out.astype(BF16),
        norm2_g.reshape(1, d), rw, rb)
    x1 = x1.reshape(t, d)

    gstart, gend, item_e, item_blk, nact = _ffn_schedule(counts[0, :N_EXPERTS], t * TOP_K // ROW_BLOCK)
    seg = jnp.sum(jnp.where(meta[None, 0:TOP_K] == jnp.arange(N_EXPERTS, dtype=I32)[:, None, None],
                            gstart[:, None, None], 0), axis=0)
    slots = (seg + meta[TOP_K:2 * TOP_K]).reshape(TOP_K, t // ROW_DMA_TOKENS, ROW_DMA_TOKENS)
    slots = slots.transpose(1, 0, 2).reshape(t * TOP_K)
    xs = _dispatch_call(slots, h2)
    ys = _ffn_call(item_e, item_blk, nact, gstart, gend, xs, w_gate_up,
                   b_gate_up.reshape(N_EXPERTS, 1, 2 * EXPERT_DIM), w_down, b_down.reshape(N_EXPERTS, 1, d))
    out = _combine_call(slots, ys, rgate, x1)
    return out.reshape(b, s, d)


def kernel(x, positions, norm1_g, w_in, q_norm_g, k_norm_g, idx_k_norm_g, idx_k_norm_b, w_branch_attn,
           pool_mix_w, pool_scale, w_branch_pool, w_out, norm2_g, router_w, router_b, w_gate_up,
           b_gate_up, w_down, b_down):
    for l in range(norm1_g.shape[0]):
        x = _layer(x, positions, norm1_g[l], w_in[l], q_norm_g[l], k_norm_g[l], idx_k_norm_g[l],
                   idx_k_norm_b[l], w_branch_attn[l], pool_mix_w[l], pool_scale[l], w_branch_pool[l],
                   w_out[l], norm2_g[l], router_w[l], router_b[l], w_gate_up[l], b_gate_up[l],
                   w_down[l], b_down[l])
    return x
```

```python
import functools

import numpy as np
import jax
import jax.numpy as jnp
from jax import lax
from jax.experimental import pallas as pl
from jax.experimental.pallas import tpu as pltpu

F32 = jnp.float32
BF16 = jnp.bfloat16
I32 = jnp.int32

D_MODEL = 1024
CHUNK = 64
CHUNK_SHIFT = CHUNK.bit_length() - 1
ATTN_HEADS = 8
ATTN_KV_HEADS = 2
HEAD_DIM = 64
GROUPS = ATTN_HEADS // ATTN_KV_HEADS
ATTN_WIDTH = ATTN_HEADS * HEAD_DIM
KV_WIDTH = ATTN_KV_HEADS * HEAD_DIM
ATTN_SCALE = HEAD_DIM ** -0.5
IDX_HEADS = 4
IDX_DIM = 64
IDX_ROPE_DIM = 32
IDX_SCALE = (IDX_HEADS ** -0.5) * (IDX_DIM ** -0.5)
IDX_TOPK_MAX = 256
QUERY_BLOCK = 256
POOL_WINDOWS = (2, 4, 8, 16)
POOL_WIDTH = 512
POOL_GROUP_DIM = 128
POOL_HALO = 16
N_EXPERTS = 32
TOP_K = 4
EXPERT_DIM = 1024
SWIGLU_ALPHA = 1.702
SWIGLU_LIMIT = 7.0
ROPE_THETA = 10000.0
NORM_EPS = 1e-6

LANES = 128
SUBLANES = 8
VREGS = 64
VMEM_LIMIT = 56 * 1024 * 1024
FLT_MAX = float(np.finfo(np.float32).max)
MASKED = -1e30

C_Q = 0
C_K = C_Q + ATTN_WIDTH
C_V = C_K + KV_WIDTH
C_QI = C_V + KV_WIDTH
C_KIW = C_QI + IDX_HEADS * IDX_DIM
C_POOL = C_KIW + LANES
C_GATE = C_POOL + POOL_WIDTH
W_PACKED = C_GATE + 2 * D_MODEL
C_SMALL_END = C_POOL
PACK_COLS = 512

ROW_BLOCK = 512
FOLD_ROWS = 4 * SUBLANES
VALUE_BISECT_ITERS = 16
BISECT_CAP = 64
BISECT_HEAD = 16
BISECT_UNROLL = 2
ATT_CK = 256
TIE_ROWS = 512
VT_ROWS = HEAD_DIM + 2 * SUBLANES
LOG2E = float(np.log2(np.e))
ROPE_ROWS = 64


def _cparams(*sem):
    return pltpu.CompilerParams(dimension_semantics=sem, vmem_limit_bytes=VMEM_LIMIT)


def _proj_kernel(x_ref, pos_ref, g1_ref, wraw_ref, rows_ref, freq_ref, gsum_ref,
                 qt_ref, k_ref, vt_ref, qit_ref, ki_ref, wit_ref, pool_ref, gate_ref, w_ref):
    @pl.when(jnp.logical_and(pl.program_id(0) == 0, pl.program_id(1) == 0))
    def _():
        narrow = IDX_DIM + IDX_HEADS
        w_ref[:, 0:C_KIW] = wraw_ref[:, 0:C_KIW].astype(BF16)
        w_ref[:, C_KIW:C_POOL] = jnp.concatenate(
            [wraw_ref[:, C_KIW:C_KIW + narrow], jnp.zeros((D_MODEL, LANES - narrow), F32)], axis=1).astype(BF16)
        for c0 in range(C_POOL, W_PACKED, PACK_COLS):
            src = c0 - (LANES - narrow)
            w_ref[:, c0:c0 + PACK_COLS] = wraw_ref[:, src:src + PACK_COLS].astype(BF16)

    x = x_ref[0]
    tm = x.shape[0]
    ms = jnp.mean(x * x, axis=-1, keepdims=True)
    h = (x * lax.rsqrt(ms + NORM_EPS) * g1_ref[...]).astype(BF16)
    d1 = jnp.dot(h, w_ref[:, 0:C_SMALL_END], preferred_element_type=F32)
    pool_ref[0] = jnp.dot(h, w_ref[:, C_POOL:C_GATE], preferred_element_type=F32)
    gate_ref[0] = jnp.dot(h, w_ref[:, C_GATE:W_PACKED], preferred_element_type=F32)

    half_m, half_i = HEAD_DIM // 2, IDX_ROPE_DIM // 2
    ang = freq_ref[...] * pos_ref[0].astype(F32)
    cos_t, sin_t = jnp.cos(ang), jnp.sin(ang)
    cm, sm = cos_t[0:half_m], sin_t[0:half_m]
    ci, si = cos_t[half_m:half_m + half_i], sin_t[half_m:half_m + half_i]
    rest_one = jnp.ones((IDX_DIM - IDX_ROPE_DIM, tm), F32)
    rest_zero = jnp.zeros((IDX_DIM - IDX_ROPE_DIM, tm), F32)
    heads = LANES // HEAD_DIM
    cos_m = jnp.concatenate([cm, cm] * heads, axis=0)
    sin_m = jnp.concatenate([-sm, sm] * heads, axis=0)
    cos_i = jnp.concatenate([ci, ci, rest_one] * heads, axis=0)
    sin_i = jnp.concatenate([-si, si, rest_zero] * heads, axis=0)
    rows = rows_ref[...]
    gsum = gsum_ref[...]

    def head_rms(xc, grow):
        sq = xc * xc
        hi = sq.astype(BF16)
        lo = (sq - hi.astype(F32)).astype(BF16)
        ssum = (jnp.dot(hi, gsum, preferred_element_type=F32)
                + jnp.dot(lo, gsum, preferred_element_type=F32))
        return xc * lax.rsqrt(ssum * (1.0 / HEAD_DIM) + NORM_EPS) * grow

    def rope_t(xt, cos_tab, sin_tab, half, width):
        parts = []
        for h0 in range(0, LANES, width):
            parts += [xt[h0 + half:h0 + 2 * half], xt[h0:h0 + half]]
            if 2 * half < width:
                parts.append(xt[h0 + 2 * half:h0 + width])
        return xt * cos_tab + jnp.concatenate(parts, axis=0) * sin_tab

    for c in range(ATTN_WIDTH // LANES):
        qc = d1[:, C_Q + c * LANES:C_Q + (c + 1) * LANES]
        qr = rope_t(head_rms(qc, rows[0:1]).T, cos_m, sin_m, half_m, HEAD_DIM) * (ATTN_SCALE * LOG2E)
        qt_ref[0, c * LANES:(c + 1) * LANES, :] = qr.astype(BF16)

    kr = rope_t(head_rms(d1[:, C_K:C_K + KV_WIDTH], rows[1:2]).T, cos_m, sin_m, half_m, HEAD_DIM).T
    vt = d1[:, C_V:C_V + KV_WIDTH].T
    for r in range(ATTN_KV_HEADS):
        k_ref[0, r] = kr[:, r * HEAD_DIM:(r + 1) * HEAD_DIM].astype(BF16)
        vt_ref[0, r, 0:HEAD_DIM, :] = vt[r * HEAD_DIM:(r + 1) * HEAD_DIM].astype(BF16)
        vt_ref[0, r, HEAD_DIM:VT_ROWS, :] = jnp.ones((VT_ROWS - HEAD_DIM, tm), BF16)

    for c in range(IDX_HEADS * IDX_DIM // LANES):
        qc = d1[:, C_QI + c * LANES:C_QI + (c + 1) * LANES]
        qit_ref[0, c * LANES:(c + 1) * LANES, :] = rope_t(qc.T, cos_i, sin_i, half_i, IDX_DIM).astype(BF16)

    kiw = d1[:, C_KIW:C_KIW + LANES]
    in_ki = lax.broadcasted_iota(I32, (tm, LANES), 1) < IDX_DIM
    mu = jnp.sum(jnp.where(in_ki, kiw, 0.0), axis=-1, keepdims=True) * (1.0 / IDX_DIM)
    dv = jnp.where(in_ki, kiw - mu, 0.0)
    var = jnp.sum(dv * dv, axis=-1, keepdims=True) * (1.0 / IDX_DIM)
    kin = dv * lax.rsqrt(var + NORM_EPS) * rows[2:3] + rows[3:4]
    ki_ref[0] = rope_t(kin.T, cos_i, sin_i, half_i, IDX_DIM).T[:, 0:IDX_DIM].astype(BF16)
    wit_ref[0] = (kiw * IDX_SCALE).T[IDX_DIM:IDX_DIM + SUBLANES]


def _proj_call(x, pos_rows, g1, w_in, rows, freq, gsum):
    b, s, d = x.shape
    tm = min(512, s)
    grid = (b, s // tm)
    full2 = lambda i, j: (0, 0)
    out_shape = (
        jax.ShapeDtypeStruct((b, ATTN_WIDTH, s), BF16),
        jax.ShapeDtypeStruct((b, ATTN_KV_HEADS, s, HEAD_DIM), BF16),
        jax.ShapeDtypeStruct((b, ATTN_KV_HEADS, VT_ROWS, s), BF16),
        jax.ShapeDtypeStruct((b, IDX_HEADS * IDX_DIM, s), BF16),
        jax.ShapeDtypeStruct((b, s, IDX_DIM), BF16),
        jax.ShapeDtypeStruct((b, SUBLANES, s), F32),
        jax.ShapeDtypeStruct((b, s, POOL_WIDTH), F32),
        jax.ShapeDtypeStruct((b, s, 2 * D_MODEL), F32),
    )
    in_specs = [
        pl.BlockSpec((1, tm, d), lambda i, j: (i, j, 0)),
        pl.BlockSpec((1, 1, tm), lambda i, j: (i, 0, j)),
        pl.BlockSpec((1, d), full2),
        pl.BlockSpec(w_in.shape, full2, pipeline_mode=pl.Buffered(1)),
        pl.BlockSpec((SUBLANES, LANES), full2),
        pl.BlockSpec((ROPE_ROWS, 1), full2),
        pl.BlockSpec((LANES, LANES), full2),
    ]
    out_specs = (
        pl.BlockSpec((1, ATTN_WIDTH, tm), lambda i, j: (i, 0, j)),
        pl.BlockSpec((1, ATTN_KV_HEADS, tm, HEAD_DIM), lambda i, j: (i, 0, j, 0)),
        pl.BlockSpec((1, ATTN_KV_HEADS, VT_ROWS, tm), lambda i, j: (i, 0, 0, j)),
        pl.BlockSpec((1, IDX_HEADS * IDX_DIM, tm), lambda i, j: (i, 0, j)),
        pl.BlockSpec((1, tm, IDX_DIM), lambda i, j: (i, j, 0)),
        pl.BlockSpec((1, SUBLANES, tm), lambda i, j: (i, 0, j)),
        pl.BlockSpec((1, tm, POOL_WIDTH), lambda i, j: (i, j, 0)),
        pl.BlockSpec((1, tm, 2 * D_MODEL), lambda i, j: (i, j, 0)),
    )
    return pl.pallas_call(
        _proj_kernel, grid=grid, in_specs=in_specs, out_specs=out_specs, out_shape=out_shape,
        scratch_shapes=[pltpu.VMEM((d, W_PACKED), BF16)],
        compiler_params=_cparams("arbitrary", "arbitrary"), name="proj",
    )(x, pos_rows, g1, w_in, rows, freq, gsum)


def _sort_key(x):
    b = lax.bitcast_convert_type(x, I32)
    return b ^ ((b >> 31) & 0x7FFFFFFF)


def _unsort_key(k):
    return lax.bitcast_convert_type(k ^ ((k >> 31) & 0x7FFFFFFF), F32)


def _attn_kernel(qt_ref, qit_ref, wit_ref, ki_ref, k_ref, vt_ref, tri_ref, o_ref, sc_ref, sa_ref, sb_ref,
                 *, n_sel, ck):
    qb = QUERY_BLOCK
    t0 = pl.program_id(1) * qb
    nck = (t0 + qb + ck - 1) // ck
    qpos = t0 + lax.broadcasted_iota(I32, (1, qb), 1)
    cend = ((qpos >> CHUNK_SHIFT) + 1) * CHUNK
    key_ck = lax.broadcasted_iota(I32, (ck, qb), 0)
    k_sel = float(n_sel)
    cnt_rows = min(ck, VREGS * SUBLANES * LANES // qb)

    def fold(x, op):
        return op(x.reshape(ck // FOLD_ROWS, FOLD_ROWS, qb), axis=0)

    qit = qit_ref[0]
    rhs_i = jnp.concatenate([qit[h * IDX_DIM:(h + 1) * IDX_DIM] for h in range(IDX_HEADS)], axis=1)
    wit = wit_ref[0]

    def score_body(c, carry):
        mxp, mnp, c0p, c1p = carry
        off = pl.multiple_of(c * ck, ck)
        lg = jnp.dot(ki_ref[0, pl.ds(off, ck), :], rhs_i, preferred_element_type=F32)
        sc = jnp.maximum(lg[:, 0:qb], 0.0) * wit[0:1]
        for h in range(1, IDX_HEADS):
            sc = sc + jnp.maximum(lg[:, h * qb:(h + 1) * qb], 0.0) * wit[h:h + 1]
        adm = (off + key_ck) < cend
        sc = jnp.where(adm, sc, -jnp.inf)
        sc_ref[pl.ds(off, ck), :] = sc
        mxp = jnp.maximum(mxp, fold(sc, jnp.max))
        mnp = jnp.minimum(mnp, fold(jnp.where(adm, sc, jnp.inf), jnp.min))
        c0p = c0p + fold(jnp.where(sc >= 0.0, 1.0, 0.0), jnp.sum)
        c1p = c1p + fold(jnp.where(sc > 0.0, 1.0, 0.0), jnp.sum)
        return mxp, mnp, c0p, c1p

    init = (jnp.full((FOLD_ROWS, qb), -jnp.inf, F32), jnp.full((FOLD_ROWS, qb), jnp.inf, F32),
            jnp.zeros((FOLD_ROWS, qb), F32), jnp.zeros((FOLD_ROWS, qb), F32))
    mxp, mnp, c0p, c1p = lax.fori_loop(0, nck, score_body, init)
    mx = jnp.max(mxp, axis=0, keepdims=True)
    mn = jnp.min(mnp, axis=0, keepdims=True)
    c0 = jnp.sum(c0p, axis=0, keepdims=True)
    c1 = jnp.sum(c1p, axis=0, keepdims=True)

    def count(pred):
        def body(c, acc):
            for sb in range(ck // cnt_rows):
                off = pl.multiple_of(c * ck + sb * cnt_rows, cnt_rows)
                hit = pred(sc_ref[pl.ds(off, cnt_rows), :], off + key_ck[0:cnt_rows])
                acc = acc + jnp.sum(jnp.where(hit, 1.0, 0.0).reshape(cnt_rows // FOLD_ROWS, FOLD_ROWS, qb),
                                    axis=0)
            return acc
        acc = lax.fori_loop(0, nck, body, jnp.zeros((FOLD_ROWS, qb), F32))
        return jnp.sum(acc, axis=0, keepdims=True)

    small = cend.astype(F32) <= k_sel
    at_zero = jnp.logical_and(jnp.logical_not(small), jnp.logical_and(c1 < k_sel, c0 >= k_sel))
    positive = jnp.logical_and(jnp.logical_not(small), c1 >= k_sel)
    lo0 = jnp.where(positive, 0.0, mn)
    hi0 = jnp.where(positive, jnp.minimum(2.0 * mx, FLT_MAX), 0.0)
    thr0 = jnp.where(small, -FLT_MAX, 0.0)
    done0 = jnp.where(jnp.logical_or(small, at_zero), 1.0, 0.0)
    tie0 = jnp.where(jnp.logical_and(at_zero, c0 > k_sel), 1.0, 0.0)
    chi0 = jnp.where(positive, 0.0, c0)
    left0 = jnp.sum(1.0 - done0)

    def bisect_cond(st):
        it, left = st[0], st[1]
        return jnp.logical_and(it < BISECT_CAP, left > 0.0)

    def bisect_step(it, lo, hi, thr, done, tie, chi):
        mid_v = 0.5 * lo + 0.5 * hi
        klo = _sort_key(lo)
        khi = _sort_key(hi)
        mid_k = _unsort_key((klo & khi) + ((klo ^ khi) >> 1))
        mid = jnp.where(it < VALUE_BISECT_ITERS, mid_v, mid_k)
        stuck = jnp.logical_or(mid <= lo, mid >= hi)
        cnt = count(lambda s, _: s >= mid)
        active = done == 0.0
        moving = jnp.logical_and(active, jnp.logical_not(stuck))
        hit = jnp.logical_and(moving, cnt == k_sel)
        new_tie = jnp.logical_and(active, stuck)
        thr = jnp.where(hit, mid, jnp.where(new_tie, lo, thr))
        tie = jnp.where(new_tie, 1.0, tie)
        done = jnp.where(jnp.logical_or(hit, new_tie), 1.0, done)
        upd = jnp.logical_and(moving, jnp.logical_not(hit))
        lo = jnp.where(jnp.logical_and(upd, cnt >= k_sel), mid, lo)
        lower = jnp.logical_and(upd, cnt < k_sel)
        hi = jnp.where(lower, mid, hi)
        chi = jnp.where(lower, cnt, chi)
        return lo, hi, thr, done, tie, chi

    def bisect_body(steps, st):
        it, _, lo, hi, thr, done, tie, chi = st
        for u in range(steps):
            lo, hi, thr, done, tie, chi = bisect_step(it + u, lo, hi, thr, done, tie, chi)
        return it + steps, jnp.sum(1.0 - done), lo, hi, thr, done, tie, chi

    st = (jnp.int32(0), left0, lo0, hi0, thr0, done0, tie0, chi0)
    st = lax.while_loop(lambda s: jnp.logical_and(s[0] < BISECT_HEAD, s[1] > 0.0),
                        functools.partial(bisect_body, BISECT_HEAD), st)
    st = lax.while_loop(bisect_cond, functools.partial(bisect_body, BISECT_UNROLL), st)
    thr, tie = st[4], st[6]
    n_above = jnp.where(at_zero, c1, st[7])

    @pl.when(jnp.sum(tie) > 0.0)
    def _():
        need = jnp.where(tie > 0.0, k_sel - n_above, FLT_MAX)

        def drop_body(c, before):
            tr = tri_ref.shape[0]
            for sb in range(ck // tr):
                off = pl.multiple_of(c * ck + sb * tr, tr)
                blk = sc_ref[pl.ds(off, tr), :]
                eq = blk == thr
                eqf = jnp.where(eq, 1.0, 0.0)
                rank = before + jnp.dot(tri_ref[...], eqf.astype(BF16), preferred_element_type=F32)
                sc_ref[pl.ds(off, tr), :] = jnp.where(jnp.logical_and(eq, rank >= need), -jnp.inf, blk)
                before = before + jnp.sum(jnp.sum(eqf.reshape(tr // FOLD_ROWS, FOLD_ROWS, qb), axis=0),
                                          axis=0, keepdims=True)
            return before

        lax.fori_loop(0, nck, drop_body, jnp.zeros((1, qb), F32))

    qt = qt_ref[0]
    rhs = [jnp.concatenate(
        [qt[(r * GROUPS + g) * HEAD_DIM:(r * GROUPS + g + 1) * HEAD_DIM] for g in range(GROUPS)],
        axis=1) for r in range(ATTN_KV_HEADS)]

    n_att = (t0 + qb + ATT_CK - 1) // ATT_CK

    def step_offset(step):
        return pl.multiple_of(jnp.minimum(step, n_att - 1) * ATT_CK, ATT_CK)

    def qk_logits(step, buf):
        off = step_offset(step)
        thr_step = jnp.where(step < n_att, thr, jnp.inf)
        bias = jnp.where(sc_ref[pl.ds(off, ATT_CK), :] >= thr_step, 0.0, MASKED)
        for r in range(ATTN_KV_HEADS):
            lg = jnp.dot(k_ref[0, r, pl.ds(off, ATT_CK), :], rhs[r], preferred_element_type=F32)
            for g in range(GROUPS):
                buf[r, :, g * qb:(g + 1) * qb] = lg[:, g * qb:(g + 1) * qb] + bias

    def softmax_pv(step, buf, carry):
        off = step_offset(step)
        out = []
        for r in range(ATTN_KV_HEADS):
            m, acc = carry[r]
            m_new = jnp.maximum(m, jnp.max(buf[r], axis=0, keepdims=True))
            p = jnp.exp2(buf[r] - m_new).astype(BF16)
            pv = jnp.dot(vt_ref[0, r, :, pl.ds(off, ATT_CK)], p, preferred_element_type=F32)
            out.append((m_new, jnp.exp2(m - m_new) * acc + pv))
        return tuple(out)

    def att_body(i, carry):
        qk_logits(2 * i + 1, sb_ref)
        carry = softmax_pv(2 * i, sa_ref, carry)
        qk_logits(2 * i + 2, sa_ref)
        return softmax_pv(2 * i + 1, sb_ref, carry)

    init_a = tuple((jnp.full((1, GROUPS * qb), MASKED, F32), jnp.zeros((VT_ROWS, GROUPS * qb), F32))
                   for _ in range(ATTN_KV_HEADS))
    qk_logits(0, sa_ref)
    fin = lax.fori_loop(0, (n_att + 1) // 2, att_body, init_a)
    outs = []
    for r in range(ATTN_KV_HEADS):
        acc = fin[r][1]
        o = acc[0:HEAD_DIM] / acc[HEAD_DIM:HEAD_DIM + 1]
        outs.extend(o[:, g * qb:(g + 1) * qb] for g in range(GROUPS))
    o_ref[0] = jnp.concatenate(outs, axis=0).T.astype(BF16)


def _attn_call(qt, k, vt, qit, ki, wit):
    b, _, s = qt.shape
    n_sel = min(IDX_TOPK_MAX, s // 4)
    ck = min(512, s)
    grid = (b, s // QUERY_BLOCK)
    in_specs = [
        pl.BlockSpec((1, ATTN_WIDTH, QUERY_BLOCK), lambda i, j: (i, 0, j)),
        pl.BlockSpec((1, IDX_HEADS * IDX_DIM, QUERY_BLOCK), lambda i, j: (i, 0, j)),
        pl.BlockSpec((1, SUBLANES, QUERY_BLOCK), lambda i, j: (i, 0, j)),
        pl.BlockSpec((1, s, IDX_DIM), lambda i, j: (i, 0, 0)),
        pl.BlockSpec((1, ATTN_KV_HEADS, s, HEAD_DIM), lambda i, j: (i, 0, 0, 0)),
        pl.BlockSpec((1, ATTN_KV_HEADS, VT_ROWS, s), lambda i, j: (i, 0, 0, 0)),
        pl.BlockSpec((min(TIE_ROWS, ck), min(TIE_ROWS, ck)), lambda i, j: (0, 0)),
    ]
    tri = jnp.asarray(np.tril(np.ones((min(TIE_ROWS, ck), min(TIE_ROWS, ck)), np.float32), -1), BF16)
    return pl.pallas_call(
        functools.partial(_attn_kernel, n_sel=n_sel, ck=ck),
        grid=grid, in_specs=in_specs,
        out_specs=pl.BlockSpec((1, QUERY_BLOCK, ATTN_WIDTH), lambda i, j: (i, j, 0)),
        out_shape=jax.ShapeDtypeStruct((b, s, ATTN_WIDTH), BF16),
        scratch_shapes=[pltpu.VMEM((s, QUERY_BLOCK), F32),
                        pltpu.VMEM((ATTN_KV_HEADS, ATT_CK, GROUPS * QUERY_BLOCK), F32),
                        pltpu.VMEM((ATTN_KV_HEADS, ATT_CK, GROUPS * QUERY_BLOCK), F32)],
        compiler_params=_cparams("parallel", "parallel"), name="attn",
    )(qt, qit, wit, ki, k, vt, tri)


def _merge_kernel(x_ref, attn_ref, pool_ref, gate_ref, wba_ref, mix_ref, pscale_ref, wbp_ref, wout_ref,
                  g2_ref, rw_ref, rb_ref, tri_ref,
                  o_ref, h_ref, meta_ref, rgate_ref, cnt_ref, ext_ref, carry_ref):
    tm = x_ref.shape[1]
    j = pl.program_id(1)

    @pl.when(j == 0)
    def _():
        ext_ref[0:POOL_HALO] = jnp.zeros((POOL_HALO, POOL_WIDTH), F32)

    @pl.when(j > 0)
    def _():
        ext_ref[0:POOL_HALO] = ext_ref[tm:tm + POOL_HALO]

    ext_ref[POOL_HALO:POOL_HALO + tm] = pool_ref[0]
    t = j * tm + lax.broadcasted_iota(I32, (tm, 1), 0)
    mixed = []
    for g, w in enumerate(POOL_WINDOWS):
        cols = slice(g * POOL_GROUP_DIM, (g + 1) * POOL_GROUP_DIM)
        cur = ext_ref[POOL_HALO:POOL_HALO + tm, cols]
        wsum = cur
        for i in range(1, w):
            wsum = wsum + ext_ref[POOL_HALO - i:POOL_HALO - i + tm, cols]
        cnt = jnp.minimum(t + 1, w).astype(F32)
        dev = (wsum / cnt - cur).astype(BF16)
        mixed.append(jnp.dot(dev, mix_ref[g], preferred_element_type=F32))
    pooled = (jnp.concatenate(mixed, axis=1) * pscale_ref[...]).astype(BF16)
    branch_a = jnp.dot(attn_ref[0], wba_ref[...], preferred_element_type=F32)
    branch_p = jnp.dot(pooled, wbp_ref[...], preferred_element_type=F32)
    gates = gate_ref[0]
    merged = (jax.nn.sigmoid(gates[:, 0:D_MODEL]) * branch_a
              + jax.nn.sigmoid(gates[:, D_MODEL:2 * D_MODEL]) * branch_p)
    x1 = x_ref[0] + jnp.dot(merged.astype(BF16), wout_ref[...], preferred_element_type=F32)
    o_ref[0] = x1
    _route_tile(x1, jnp.logical_and(pl.program_id(0) == 0, j == 0), g2_ref, rw_ref, rb_ref, tri_ref,
                h_ref, meta_ref, rgate_ref, cnt_ref, carry_ref)


def _merge_call(x, attn, pool, gates, wba, mix, pscale, wbp, wout, g2, rw, rb):
    b, s, d = x.shape
    tm = min(512, s)
    nt = s // tm
    t = b * s
    tile = lambda w: pl.BlockSpec((1, tm, w), lambda i, j: (i, j, 0))
    flat = lambda rows, w: pl.BlockSpec((rows, w), lambda i, j: (i * nt + j, 0))
    full2 = lambda i, j: (0, 0)
    tri = jnp.asarray(np.tril(np.ones((tm, tm), np.float32), -1), BF16)
    in_specs = [
        tile(d), tile(ATTN_WIDTH), tile(POOL_WIDTH), tile(2 * D_MODEL),
        pl.BlockSpec((ATTN_WIDTH, d), full2),
        pl.BlockSpec((len(POOL_WINDOWS), POOL_GROUP_DIM, POOL_GROUP_DIM), lambda i, j: (0, 0, 0)),
        pl.BlockSpec((1, POOL_WIDTH), full2),
        pl.BlockSpec((POOL_WIDTH, d), full2),
        pl.BlockSpec((d, d), full2),
        pl.BlockSpec((1, d), full2),
        pl.BlockSpec((d, LANES), full2),
        pl.BlockSpec((1, LANES), full2),
        pl.BlockSpec((tm, tm), full2),
    ]
    out_shape = (
        jax.ShapeDtypeStruct((b, s, d), F32),
        jax.ShapeDtypeStruct((t * ROW_TILES, LANES), F32),
        jax.ShapeDtypeStruct((2 * TOP_K, t), I32),
        jax.ShapeDtypeStruct((t, LANES), F32),
        jax.ShapeDtypeStruct((1, LANES), I32),
    )
    out_specs = (
        tile(d), flat(tm * ROW_TILES, LANES),
        pl.BlockSpec((2 * TOP_K, tm), lambda i, j: (0, i * nt + j)),
        flat(tm, LANES), pl.BlockSpec((1, LANES), full2),
    )
    return pl.pallas_call(
        _merge_kernel, grid=(b, nt), in_specs=in_specs, out_specs=out_specs, out_shape=out_shape,
        scratch_shapes=[pltpu.VMEM((POOL_HALO + tm, POOL_WIDTH), F32), pltpu.VMEM((1, LANES), F32)],
        compiler_params=_cparams("arbitrary", "arbitrary"), name="merge",
    )(x, attn, pool, gates, wba, mix, pscale, wbp, wout, g2, rw, rb, tri)


ROW_TILES = D_MODEL // LANES


def _load_rows(ref, n, *lead):
    return jnp.concatenate(
        [ref[(*lead, pl.ds(c, n, stride=ROW_TILES), slice(None))] for c in range(ROW_TILES)], axis=1)


def _store_rows(ref, val):
    for c in range(ROW_TILES):
        ref[pl.ds(c, val.shape[0], stride=ROW_TILES), :] = val[:, c * LANES:(c + 1) * LANES]


def _row_tile(ref, i):
    return ref.at[pl.ds(pl.multiple_of(i * ROW_TILES, ROW_TILES), ROW_TILES)]


def _route_tile(x, first_tile, g2_ref, rw_ref, rb_ref, tri_ref, h_ref, meta_ref, gate_ref, cnt_ref, carry_ref):
    tm = x.shape[0]

    @pl.when(first_tile)
    def _():
        carry_ref[...] = jnp.zeros((1, LANES), F32)

    ms = jnp.mean(x * x, axis=-1, keepdims=True)
    h = x * lax.rsqrt(ms + NORM_EPS) * g2_ref[...]
    _store_rows(h_ref, h)
    logits = jnp.dot(h.astype(BF16), rw_ref[...], preferred_element_type=F32) + rb_ref[...]
    lane = lax.broadcasted_iota(I32, (tm, LANES), 1).astype(F32)
    work = jnp.where(lane < N_EXPERTS, logits, -jnp.inf)
    vals, idxs = [], []
    for _ in range(TOP_K):
        m = jnp.max(work, axis=-1, keepdims=True)
        idx = jnp.min(jnp.where(work == m, lane, float(LANES)), axis=-1, keepdims=True)
        vals.append(m)
        idxs.append(idx)
        work = jnp.where(lane == idx, -jnp.inf, work)
    exps = [jnp.exp(v - vals[0]) for v in vals]
    denom = exps[0] + exps[1] + exps[2] + exps[3]
    member = jnp.zeros((tm, LANES), F32)
    for idx in idxs:
        member = member + jnp.where(lane == idx, 1.0, 0.0)
    before = jnp.dot(tri_ref[...], member.astype(BF16), preferred_element_type=F32) + carry_ref[...]
    meta = jnp.zeros((tm, LANES), F32)
    gate = jnp.zeros((tm, LANES), F32)
    for k in range(TOP_K):
        rank = jnp.sum(jnp.where(lane == idxs[k], before, 0.0), axis=-1, keepdims=True)
        meta = jnp.where(lane == float(k), idxs[k], meta)
        meta = jnp.where(lane == float(TOP_K + k), rank, meta)
        gate = jnp.where(lane == float(k), exps[k] / denom, gate)
    meta_ref[...] = meta.T[0:2 * TOP_K].astype(I32)
    gate_ref[...] = gate
    total = carry_ref[...] + jnp.sum(member, axis=0, keepdims=True)
    carry_ref[...] = total
    cnt_ref[...] = total.astype(I32)


ROW_UNROLL = SUBLANES
ROW_DMA_TOKENS = 512


def _dispatch_kernel(slot_ref, h_ref, xs_ref, sem, *, tm):
    def issue(i, carry):
        for u in range(ROW_UNROLL):
            j = i * ROW_UNROLL + u
            for k in range(TOP_K):
                pltpu.make_async_copy(_row_tile(h_ref, j), _row_tile(xs_ref, slot_ref[k * tm + j]),
                                      sem).start(priority=k % 2)
        return carry

    lax.fori_loop(0, tm // ROW_UNROLL, issue, 0)

    for k in range(TOP_K):
        pltpu.make_async_copy(h_ref, xs_ref.at[pl.ds(0, tm * ROW_TILES)], sem).wait()


def _dispatch_call(slots, h2):
    t = h2.shape[0] // ROW_TILES
    tm = ROW_DMA_TOKENS
    return pl.pallas_call(
        functools.partial(_dispatch_kernel, tm=tm),
        grid=(t // tm,),
        in_specs=[pl.BlockSpec((tm * TOP_K,), lambda i: (i,), memory_space=pltpu.SMEM),
                  pl.BlockSpec((tm * ROW_TILES, LANES), lambda i: (i, 0))],
        out_specs=pl.BlockSpec(memory_space=pl.ANY),
        scratch_shapes=[pltpu.SemaphoreType.DMA(())],
        out_shape=jax.ShapeDtypeStruct((t * TOP_K * ROW_TILES, LANES), F32),
        compiler_params=_cparams("arbitrary"), name="dispatch",
    )(slots, h2)


def _ffn_kernel(item_e_ref, item_blk_ref, nact_ref, gstart_ref, gend_ref,
                xs_ref, wgu_ref, bgu_ref, wd_ref, bd_ref, o_ref, wgu_bf, wd_bf):
    w = pl.program_id(0)
    e = item_e_ref[w]
    blk = item_blk_ref[w]
    prev = jnp.maximum(w - 1, 0)
    active = w < nact_ref[0]
    new_expert = jnp.logical_or(w == 0, e != item_e_ref[prev])
    first_visit = jnp.logical_or(w == 0, blk != item_blk_ref[prev])

    @pl.when(jnp.logical_and(active, new_expert))
    def _():
        wgu_bf[...] = wgu_ref[0].astype(BF16)
        wd_bf[...] = wd_ref[0].astype(BF16)

    @pl.when(active)
    def _():
        xb = _load_rows(xs_ref, ROW_BLOCK).astype(BF16)
        gate = jnp.dot(xb, wgu_bf[:, 0:EXPERT_DIM], preferred_element_type=F32) + bgu_ref[0, :, 0:EXPERT_DIM]
        up = (jnp.dot(xb, wgu_bf[:, EXPERT_DIM:2 * EXPERT_DIM], preferred_element_type=F32)
              + bgu_ref[0, :, EXPERT_DIM:2 * EXPERT_DIM])
        gate = jnp.minimum(gate, SWIGLU_LIMIT)
        up = jnp.clip(up, -SWIGLU_LIMIT, SWIGLU_LIMIT)
        act = gate * jax.nn.sigmoid(SWIGLU_ALPHA * gate) * (up + 1.0)
        res = jnp.dot(act.astype(BF16), wd_bf[...], preferred_element_type=F32) + bd_ref[0]
        row = blk * ROW_BLOCK + lax.broadcasted_iota(I32, (ROW_BLOCK, 1), 0)
        mine = jnp.logical_and(row >= gstart_ref[e], row < gend_ref[e])

        @pl.when(first_visit)
        def _():
            _store_rows(o_ref, jnp.where(mine, res, 0.0))

        @pl.when(jnp.logical_not(first_visit))
        def _():
            _store_rows(o_ref, jnp.where(mine, res, _load_rows(o_ref, ROW_BLOCK)))


def _ffn_call(item_e, item_blk, nact, gstart, gend, xs, wgu, bgu, wd, bd):
    d = D_MODEL
    rows = lambda w, ie, ib, *_: (ib[w], 0)
    exp3 = lambda w, ie, *_: (ie[w], 0, 0)
    return pl.pallas_call(
        _ffn_kernel,
        grid_spec=pltpu.PrefetchScalarGridSpec(
            num_scalar_prefetch=5, grid=(item_e.shape[0],),
            in_specs=[pl.BlockSpec((ROW_BLOCK * ROW_TILES, LANES), rows),
                      pl.BlockSpec((1, d, 2 * EXPERT_DIM), exp3),
                      pl.BlockSpec((1, 1, 2 * EXPERT_DIM), exp3),
                      pl.BlockSpec((1, EXPERT_DIM, d), exp3),
                      pl.BlockSpec((1, 1, d), exp3)],
            out_specs=pl.BlockSpec((ROW_BLOCK * ROW_TILES, LANES), rows),
            scratch_shapes=[pltpu.VMEM((d, 2 * EXPERT_DIM), BF16), pltpu.VMEM((EXPERT_DIM, d), BF16)]),
        out_shape=jax.ShapeDtypeStruct(xs.shape, F32),
        compiler_params=_cparams("arbitrary"), name="ffn",
    )(item_e, item_blk, nact, gstart, gend, xs, wgu, bgu, wd, bd)


def _combine_kernel(slot_ref, slot_next_ref, ys_ref, gate_ref, x_ref, o_ref, buf_ref, sem, *, tm, n_steps):
    i = pl.program_id(0)
    half = i % 2

    def gather(slots, dst_half):
        def issue(it, carry):
            for u in range(ROW_UNROLL):
                j = it * ROW_UNROLL + u
                for k in range(TOP_K):
                    pltpu.make_async_copy(_row_tile(ys_ref, slots[k * tm + j]),
                                          _row_tile(buf_ref.at[dst_half, k], j),
                                          sem.at[dst_half]).start(priority=k % 2)
            return carry
        lax.fori_loop(0, tm // ROW_UNROLL, issue, 0)

    @pl.when(i == 0)
    def _():
        gather(slot_ref, 0)

    @pl.when(i + 1 < n_steps)
    def _():
        gather(slot_next_ref, 1 - half)

    for k in range(TOP_K):
        pltpu.make_async_copy(ys_ref.at[pl.ds(0, tm * ROW_TILES)], buf_ref.at[half, k], sem.at[half]).wait()
    gate = gate_ref[...]
    y = x_ref[...]
    for k in range(TOP_K):
        y = y + gate[:, k:k + 1] * _load_rows(buf_ref, tm, half, k)
    o_ref[...] = y


def _combine_call(slots, ys, gates, x1):
    t, d = x1.shape
    tm = ROW_DMA_TOKENS
    n_steps = t // tm
    tile = lambda w: pl.BlockSpec((tm, w), lambda i: (i, 0))
    return pl.pallas_call(
        functools.partial(_combine_kernel, tm=tm, n_steps=n_steps),
        grid=(n_steps,),
        in_specs=[pl.BlockSpec((tm * TOP_K,), lambda i: (i,), memory_space=pltpu.SMEM),
                  pl.BlockSpec((tm * TOP_K,), lambda i: (jnp.minimum(i + 1, n_steps - 1),),
                               memory_space=pltpu.SMEM),
                  pl.BlockSpec(memory_space=pl.ANY), tile(LANES), tile(d)],
        out_specs=tile(d),
        scratch_shapes=[pltpu.VMEM((2, TOP_K, tm * ROW_TILES, LANES), F32), pltpu.SemaphoreType.DMA((2,))],
        out_shape=jax.ShapeDtypeStruct((t, d), F32),
        compiler_params=_cparams("arbitrary"), name="combine",
    )(slots, slots, ys, gates, x1)


def _const_rows(q_g, k_g, i_g, i_b):
    zeros = jnp.zeros((LANES - IDX_DIM,), F32)
    rows = [
        jnp.tile(q_g.astype(F32), LANES // HEAD_DIM),
        jnp.tile(k_g.astype(F32), LANES // HEAD_DIM),
        jnp.concatenate([i_g.astype(F32), zeros]),
        jnp.concatenate([i_b.astype(F32), zeros]),
    ]
    return jnp.concatenate([jnp.stack(rows, axis=0), jnp.zeros((SUBLANES - len(rows), LANES), F32)], axis=0)


def _rope_freqs():
    inv_m = ROPE_THETA ** (-jnp.arange(0, HEAD_DIM, 2, dtype=F32) / HEAD_DIM)
    inv_i = ROPE_THETA ** (-jnp.arange(0, IDX_ROPE_DIM, 2, dtype=F32) / IDX_ROPE_DIM)
    pad = jnp.zeros((ROPE_ROWS - inv_m.shape[0] - inv_i.shape[0],), F32)
    return jnp.concatenate([inv_m, inv_i, pad]).reshape(ROPE_ROWS, 1)


def _ffn_schedule(counts, n_blocks):
    gend = jnp.cumsum(counts).astype(I32)
    gstart = gend - counts
    first_blk = gstart // ROW_BLOCK
    last_blk = (jnp.maximum(gend, 1) - 1) // ROW_BLOCK
    n_items = jnp.where(counts > 0, last_blk - first_blk + 1, 0)
    item_end = jnp.cumsum(n_items).astype(I32)
    item_start = item_end - n_items
    total = item_end[-1]
    w = jnp.minimum(jnp.arange(n_blocks + N_EXPERTS - 1, dtype=I32), total - 1)
    item_e = jnp.sum((item_end[None, :] <= w[:, None]).astype(I32), axis=1)
    mine = item_e[:, None] == jnp.arange(N_EXPERTS, dtype=I32)[None, :]
    item_blk = w + jnp.sum(jnp.where(mine, (first_blk - item_start)[None, :], 0), axis=1)
    return gstart, gend, item_e, item_blk, total.reshape(1)


def _layer(x, positions, norm1_g, w_in, q_norm_g, k_norm_g, idx_k_norm_g, idx_k_norm_b, w_branch_attn,
           pool_mix_w, pool_scale, w_branch_pool, w_out, norm2_g, router_w, router_b, w_gate_up,
           b_gate_up, w_down, b_down):
    b, s, d = x.shape
    t = b * s
    assert (t * TOP_K) % ROW_BLOCK == 0 and s % QUERY_BLOCK == 0
    lane = np.arange(LANES)
    gsum = jnp.asarray(lane[:, None] // HEAD_DIM == lane[None, :] // HEAD_DIM, BF16)

    qt, k, vt, qit, ki, wit, pool, gates = _proj_call(
        x, positions.reshape(b, 1, s), norm1_g.reshape(1, d), w_in,
        _const_rows(q_norm_g, k_norm_g, idx_k_norm_g, idx_k_norm_b), _rope_freqs(), gsum)
    attn = _attn_call(qt, k, vt, qit, ki, wit)
    rw = jnp.pad(router_w, ((0, 0), (0, LANES - N_EXPERTS))).astype(BF16)
    rb = jnp.pad(router_b, (0, LANES - N_EXPERTS)).reshape(1, LANES)
    x1, h2, meta, rgate, counts = _merge_call(
        x, attn, pool, gates, w_branch_attn.astype(BF16), pool_mix_w.astype(BF16),
        pool_scale.reshape(1, POOL_WIDTH), w_branch_pool.astype(BF16), w_out.astype(BF16),
        norm2_g.reshape(1, d), rw, rb)
    x1 = x1.reshape(t, d)

    gstart, gend, item_e, item_blk, nact = _ffn_schedule(counts[0, :N_EXPERTS], t * TOP_K // ROW_BLOCK)
    seg = jnp.sum(jnp.where(meta[None, 0:TOP_K] == jnp.arange(N_EXPERTS, dtype=I32)[:, None, None],
                            gstart[:, None, None], 0), axis=0)
    slots = (seg + meta[TOP_K:2 * TOP_K]).reshape(TOP_K, t // ROW_DMA_TOKENS, ROW_DMA_TOKENS)
    slots = slots.transpose(1, 0, 2).reshape(t * TOP_K)
    xs = _dispatch_call(slots, h2)
    ys = _ffn_call(item_e, item_blk, nact, gstart, gend, xs, w_gate_up,
                   b_gate_up.reshape(N_EXPERTS, 1, 2 * EXPERT_DIM), w_down, b_down.reshape(N_EXPERTS, 1, d))
    out = _combine_call(slots, ys, rgate, x1)
    return out.reshape(b, s, d)


def kernel(x, positions, norm1_g, w_in, q_norm_g, k_norm_g, idx_k_norm_g, idx_k_norm_b, w_branch_attn,
           pool_mix_w, pool_scale, w_branch_pool, w_out, norm2_g, router_w, router_b, w_gate_up,
           b_gate_up, w_down, b_down):
    for l in range(norm1_g.shape[0]):
        x = _layer(x, positions, norm1_g[l], w_in[l], q_norm_g[l], k_norm_g[l], idx_k_norm_g[l],
                   idx_k_norm_b[l], w_branch_attn[l], pool_mix_w[l], pool_scale[l], w_branch_pool[l],
                   w_out[l], norm2_g[l], router_w[l], router_b[l], w_gate_up[l], b_gate_up[l],
                   w_down[l], b_down[l])
    return x
```

```python
import functools

import numpy as np
import jax
import jax.numpy as jnp
from jax import lax
from jax.experimental import pallas as pl
from jax.experimental.pallas import tpu as pltpu

F32 = jnp.float32
BF16 = jnp.bfloat16
I32 = jnp.int32

D_MODEL = 1024
CHUNK = 64
CHUNK_SHIFT = CHUNK.bit_length() - 1
ATTN_HEADS = 8
ATTN_KV_HEADS = 2
HEAD_DIM = 64
GROUPS = ATTN_HEADS // ATTN_KV_HEADS
ATTN_WIDTH = ATTN_HEADS * HEAD_DIM
KV_WIDTH = ATTN_KV_HEADS * HEAD_DIM
ATTN_SCALE = HEAD_DIM ** -0.5
IDX_HEADS = 4
IDX_DIM = 64
IDX_ROPE_DIM = 32
IDX_SCALE = (IDX_HEADS ** -0.5) * (IDX_DIM ** -0.5)
IDX_TOPK_MAX = 256
QUERY_BLOCK = 256
POOL_WINDOWS = (2, 4, 8, 16)
POOL_WIDTH = 512
POOL_GROUP_DIM = 128
POOL_HALO = 16
N_EXPERTS = 32
TOP_K = 4
EXPERT_DIM = 1024
SWIGLU_ALPHA = 1.702
SWIGLU_LIMIT = 7.0
ROPE_THETA = 10000.0
NORM_EPS = 1e-6

LANES = 128
SUBLANES = 8
VREGS = 64
VMEM_LIMIT = 56 * 1024 * 1024
FLT_MAX = float(np.finfo(np.float32).max)
MASKED = -1e30

C_Q = 0
C_K = C_Q + ATTN_WIDTH
C_V = C_K + KV_WIDTH
C_QI = C_V + KV_WIDTH
C_KIW = C_QI + IDX_HEADS * IDX_DIM
C_POOL = C_KIW + LANES
C_GATE = C_POOL + POOL_WIDTH
W_PACKED = C_GATE + 2 * D_MODEL
C_SMALL_END = C_POOL
PACK_COLS = 512

ROW_BLOCK = 512
FOLD_ROWS = 4 * SUBLANES
VALUE_BISECT_ITERS = 16
BISECT_CAP = 64
BISECT_HEAD = 16
BISECT_UNROLL = 2
ATT_CK = 256
TIE_ROWS = 512
VT_ROWS = HEAD_DIM + 2 * SUBLANES
LOG2E = float(np.log2(np.e))
ROPE_ROWS = 64


def _cparams(*sem):
    return pltpu.CompilerParams(dimension_semantics=sem, vmem_limit_bytes=VMEM_LIMIT)


def _proj_kernel(x_ref, pos_ref, g1_ref, wraw_ref, rows_ref, freq_ref, gsum_ref,
                 qt_ref, k_ref, vt_ref, qit_ref, ki_ref, wit_ref, pool_ref, gate_ref, w_ref):
    @pl.when(jnp.logical_and(pl.program_id(0) == 0, pl.program_id(1) == 0))
    def _():
        narrow = IDX_DIM + IDX_HEADS
        w_ref[:, 0:C_KIW] = wraw_ref[:, 0:C_KIW].astype(BF16)
        w_ref[:, C_KIW:C_POOL] = jnp.concatenate(
            [wraw_ref[:, C_KIW:C_KIW + narrow], jnp.zeros((D_MODEL, LANES - narrow), F32)], axis=1).astype(BF16)
        for c0 in range(C_POOL, W_PACKED, PACK_COLS):
            src = c0 - (LANES - narrow)
            w_ref[:, c0:c0 + PACK_COLS] = wraw_ref[:, src:src + PACK_COLS].astype(BF16)

    x = x_ref[0]
    tm = x.shape[0]
    ms = jnp.mean(x * x, axis=-1, keepdims=True)
    h = (x * lax.rsqrt(ms + NORM_EPS) * g1_ref[...]).astype(BF16)
    d1 = jnp.dot(h, w_ref[:, 0:C_SMALL_END], preferred_element_type=F32)
    pool_ref[0] = jnp.dot(h, w_ref[:, C_POOL:C_GATE], preferred_element_type=F32)
    gate_ref[0] = jnp.dot(h, w_ref[:, C_GATE:W_PACKED], preferred_element_type=F32)

    half_m, half_i = HEAD_DIM // 2, IDX_ROPE_DIM // 2
    ang = freq_ref[...] * pos_ref[0].astype(F32)
    cos_t, sin_t = jnp.cos(ang), jnp.sin(ang)
    cm, sm = cos_t[0:half_m], sin_t[0:half_m]
    ci, si = cos_t[half_m:half_m + half_i], sin_t[half_m:half_m + half_i]
    rest_one = jnp.ones((IDX_DIM - IDX_ROPE_DIM, tm), F32)
    rest_zero = jnp.zeros((IDX_DIM - IDX_ROPE_DIM, tm), F32)
    heads = LANES // HEAD_DIM
    cos_m = jnp.concatenate([cm, cm] * heads, axis=0)
    sin_m = jnp.concatenate([-sm, sm] * heads, axis=0)
    cos_i = jnp.concatenate([ci, ci, rest_one] * heads, axis=0)
    sin_i = jnp.concatenate([-si, si, rest_zero] * heads, axis=0)
    rows = rows_ref[...]
    gsum = gsum_ref[...]

    def head_rms(xc, grow):
        sq = xc * xc
        hi = sq.astype(BF16)
        lo = (sq - hi.astype(F32)).astype(BF16)
        ssum = (jnp.dot(hi, gsum, preferred_element_type=F32)
                + jnp.dot(lo, gsum, preferred_element_type=F32))
        return xc * lax.rsqrt(ssum * (1.0 / HEAD_DIM) + NORM_EPS) * grow

    def rope_t(xt, cos_tab, sin_tab, half, width):
        parts = []
        for h0 in range(0, LANES, width):
            parts += [xt[h0 + half:h0 + 2 * half], xt[h0:h0 + half]]
            if 2 * half < width:
                parts.append(xt[h0 + 2 * half:h0 + width])
        return xt * cos_tab + jnp.concatenate(parts, axis=0) * sin_tab

    for c in range(ATTN_WIDTH // LANES):
        qc = d1[:, C_Q + c * LANES:C_Q + (c + 1) * LANES]
        qr = rope_t(head_rms(qc, rows[0:1]).T, cos_m, sin_m, half_m, HEAD_DIM) * (ATTN_SCALE * LOG2E)
        qt_ref[0, c * LANES:(c + 1) * LANES, :] = qr.astype(BF16)

    kr = rope_t(head_rms(d1[:, C_K:C_K + KV_WIDTH], rows[1:2]).T, cos_m, sin_m, half_m, HEAD_DIM).T
    vt = d1[:, C_V:C_V + KV_WIDTH].T
    for r in range(ATTN_KV_HEADS):
        k_ref[0, r] = kr[:, r * HEAD_DIM:(r + 1) * HEAD_DIM].astype(BF16)
        vt_ref[0, r, 0:HEAD_DIM, :] = vt[r * HEAD_DIM:(r + 1) * HEAD_DIM].astype(BF16)
        vt_ref[0, r, HEAD_DIM:VT_ROWS, :] = jnp.ones((VT_ROWS - HEAD_DIM, tm), BF16)

    for c in range(IDX_HEADS * IDX_DIM // LANES):
        qc = d1[:, C_QI + c * LANES:C_QI + (c + 1) * LANES]
        qit_ref[0, c * LANES:(c + 1) * LANES, :] = rope_t(qc.T, cos_i, sin_i, half_i, IDX_DIM).astype(BF16)

    kiw = d1[:, C_KIW:C_KIW + LANES]
    in_ki = lax.broadcasted_iota(I32, (tm, LANES), 1) < IDX_DIM
    mu = jnp.sum(jnp.where(in_ki, kiw, 0.0), axis=-1, keepdims=True) * (1.0 / IDX_DIM)
    dv = jnp.where(in_ki, kiw - mu, 0.0)
    var = jnp.sum(dv * dv, axis=-1, keepdims=True) * (1.0 / IDX_DIM)
    kin = dv * lax.rsqrt(var + NORM_EPS) * rows[2:3] + rows[3:4]
    ki_ref[0] = rope_t(kin.T, cos_i, sin_i, half_i, IDX_DIM).T[:, 0:IDX_DIM].astype(BF16)
    wit_ref[0] = (kiw * IDX_SCALE).T[IDX_DIM:IDX_DIM + SUBLANES]


def _proj_call(x, pos_rows, g1, w_in, rows, freq, gsum):
    b, s, d = x.shape
    tm = min(512, s)
    grid = (b, s // tm)
    full2 = lambda i, j: (0, 0)
    out_shape = (
        jax.ShapeDtypeStruct((b, ATTN_WIDTH, s), BF16),
        jax.ShapeDtypeStruct((b, ATTN_KV_HEADS, s, HEAD_DIM), BF16),
        jax.ShapeDtypeStruct((b, ATTN_KV_HEADS, VT_ROWS, s), BF16),
        jax.ShapeDtypeStruct((b, IDX_HEADS * IDX_DIM, s), BF16),
        jax.ShapeDtypeStruct((b, s, IDX_DIM), BF16),
        jax.ShapeDtypeStruct((b, SUBLANES, s), F32),
        jax.ShapeDtypeStruct((b, s, POOL_WIDTH), F32),
        jax.ShapeDtypeStruct((b, s, 2 * D_MODEL), F32),
    )
    in_specs = [
        pl.BlockSpec((1, tm, d), lambda i, j: (i, j, 0)),
        pl.BlockSpec((1, 1, tm), lambda i, j: (i, 0, j)),
        pl.BlockSpec((1, d), full2),
        pl.BlockSpec(w_in.shape, full2, pipeline_mode=pl.Buffered(1)),
        pl.BlockSpec((SUBLANES, LANES), full2),
        pl.BlockSpec((ROPE_ROWS, 1), full2),
        pl.BlockSpec((LANES, LANES), full2),
    ]
    out_specs = (
        pl.BlockSpec((1, ATTN_WIDTH, tm), lambda i, j: (i, 0, j)),
        pl.BlockSpec((1, ATTN_KV_HEADS, tm, HEAD_DIM), lambda i, j: (i, 0, j, 0)),
        pl.BlockSpec((1, ATTN_KV_HEADS, VT_ROWS, tm), lambda i, j: (i, 0, 0, j)),
        pl.BlockSpec((1, IDX_HEADS * IDX_DIM, tm), lambda i, j: (i, 0, j)),
        pl.BlockSpec((1, tm, IDX_DIM), lambda i, j: (i, j, 0)),
        pl.BlockSpec((1, SUBLANES, tm), lambda i, j: (i, 0, j)),
        pl.BlockSpec((1, tm, POOL_WIDTH), lambda i, j: (i, j, 0)),
        pl.BlockSpec((1, tm, 2 * D_MODEL), lambda i, j: (i, j, 0)),
    )
    return pl.pallas_call(
        _proj_kernel, grid=grid, in_specs=in_specs, out_specs=out_specs, out_shape=out_shape,
        scratch_shapes=[pltpu.VMEM((d, W_PACKED), BF16)],
        compiler_params=_cparams("arbitrary", "arbitrary"), name="proj",
    )(x, pos_rows, g1, w_in, rows, freq, gsum)


def _sort_key(x):
    b = lax.bitcast_convert_type(x, I32)
    return b ^ ((b >> 31) & 0x7FFFFFFF)


def _unsort_key(k):
    return lax.bitcast_convert_type(k ^ ((k >> 31) & 0x7FFFFFFF), F32)


def _attn_kernel(qt_ref, qit_ref, wit_ref, ki_ref, k_ref, vt_ref, tri_ref, o_ref, sc_ref, sa_ref, sb_ref,
                 *, n_sel, ck):
    qb = QUERY_BLOCK
    t0 = pl.program_id(1) * qb
    nck = (t0 + qb + ck - 1) // ck
    qpos = t0 + lax.broadcasted_iota(I32, (1, qb), 1)
    cend = ((qpos >> CHUNK_SHIFT) + 1) * CHUNK
    key_ck = lax.broadcasted_iota(I32, (ck, qb), 0)
    k_sel = float(n_sel)
    cnt_rows = min(ck, VREGS * SUBLANES * LANES // qb)

    def fold(x, op):
        return op(x.reshape(ck // FOLD_ROWS, FOLD_ROWS, qb), axis=0)

    qit = qit_ref[0]
    rhs_i = jnp.concatenate([qit[h * IDX_DIM:(h + 1) * IDX_DIM] for h in range(IDX_HEADS)], axis=1)
    wit = wit_ref[0]

    def score_body(c, carry):
        mxp, mnp, c0p, c1p = carry
        off = pl.multiple_of(c * ck, ck)
        lg = jnp.dot(ki_ref[0, pl.ds(off, ck), :], rhs_i, preferred_element_type=F32)
        sc = jnp.maximum(lg[:, 0:qb], 0.0) * wit[0:1]
        for h in range(1, IDX_HEADS):
            sc = sc + jnp.maximum(lg[:, h * qb:(h + 1) * qb], 0.0) * wit[h:h + 1]
        adm = (off + key_ck) < cend
        sc = jnp.where(adm, sc, -jnp.inf)
        sc_ref[pl.ds(off, ck), :] = sc
        mxp = jnp.maximum(mxp, fold(sc, jnp.max))
        mnp = jnp.minimum(mnp, fold(jnp.where(adm, sc, jnp.inf), jnp.min))
        c0p = c0p + fold(jnp.where(sc >= 0.0, 1.0, 0.0), jnp.sum)
        c1p = c1p + fold(jnp.where(sc > 0.0, 1.0, 0.0), jnp.sum)
        return mxp, mnp, c0p, c1p

    init = (jnp.full((FOLD_ROWS, qb), -jnp.inf, F32), jnp.full((FOLD_ROWS, qb), jnp.inf, F32),
            jnp.zeros((FOLD_ROWS, qb), F32), jnp.zeros((FOLD_ROWS, qb), F32))
    mxp, mnp, c0p, c1p = lax.fori_loop(0, nck, score_body, init)
    mx = jnp.max(mxp, axis=0, keepdims=True)
    mn = jnp.min(mnp, axis=0, keepdims=True)
    c0 = jnp.sum(c0p, axis=0, keepdims=True)
    c1 = jnp.sum(c1p, axis=0, keepdims=True)

    def count(pred):
        def body(c, acc):
            for sb in range(ck // cnt_rows):
                off = pl.multiple_of(c * ck + sb * cnt_rows, cnt_rows)
                hit = pred(sc_ref[pl.ds(off, cnt_rows), :], off + key_ck[0:cnt_rows])
                acc = acc + jnp.sum(jnp.where(hit, 1.0, 0.0).reshape(cnt_rows // FOLD_ROWS, FOLD_ROWS, qb),
                                    axis=0)
            return acc
        acc = lax.fori_loop(0, nck, body, jnp.zeros((FOLD_ROWS, qb), F32))
        return jnp.sum(acc, axis=0, keepdims=True)

    small = cend.astype(F32) <= k_sel
    at_zero = jnp.logical_and(jnp.logical_not(small), jnp.logical_and(c1 < k_sel, c0 >= k_sel))
    positive = jnp.logical_and(jnp.logical_not(small), c1 >= k_sel)
    lo0 = jnp.where(positive, 0.0, mn)
    hi0 = jnp.where(positive, jnp.minimum(2.0 * mx, FLT_MAX), 0.0)
    thr0 = jnp.where(small, -FLT_MAX, 0.0)
    done0 = jnp.where(jnp.logical_or(small, at_zero), 1.0, 0.0)
    tie0 = jnp.where(jnp.logical_and(at_zero, c0 > k_sel), 1.0, 0.0)
    chi0 = jnp.where(positive, 0.0, c0)
    left0 = jnp.sum(1.0 - done0)

    def bisect_cond(st):
        it, left = st[0], st[1]
        return jnp.logical_and(it < BISECT_CAP, left > 0.0)

    def bisect_step(it, lo, hi, thr, done, tie, chi):
        mid_v = 0.5 * lo + 0.5 * hi
        klo = _sort_key(lo)
        khi = _sort_key(hi)
        mid_k = _unsort_key((klo & khi) + ((klo ^ khi) >> 1))
        mid = jnp.where(it < VALUE_BISECT_ITERS, mid_v, mid_k)
        stuck = jnp.logical_or(mid <= lo, mid >= hi)
        cnt = count(lambda s, _: s >= mid)
        active = done == 0.0
        moving = jnp.logical_and(active, jnp.logical_not(stuck))
        hit = jnp.logical_and(moving, cnt == k_sel)
        new_tie = jnp.logical_and(active, stuck)
        thr = jnp.where(hit, mid, jnp.where(new_tie, lo, thr))
        tie = jnp.where(new_tie, 1.0, tie)
        done = jnp.where(jnp.logical_or(hit, new_tie), 1.0, done)
        upd = jnp.logical_and(moving, jnp.logical_not(hit))
        lo = jnp.where(jnp.logical_and(upd, cnt >= k_sel), mid, lo)
        lower = jnp.logical_and(upd, cnt < k_sel)
        hi = jnp.where(lower, mid, hi)
        chi = jnp.where(lower, cnt, chi)
        return lo, hi, thr, done, tie, chi

    def bisect_body(steps, st):
        it, _, lo, hi, thr, done, tie, chi = st
        for u in range(steps):
            lo, hi, thr, done, tie, chi = bisect_step(it + u, lo, hi, thr, done, tie, chi)
        return it + steps, jnp.sum(1.0 - done), lo, hi, thr, done, tie, chi

    st = (jnp.int32(0), left0, lo0, hi0, thr0, done0, tie0, chi0)
    st = lax.while_loop(lambda s: jnp.logical_and(s[0] < BISECT_HEAD, s[1] > 0.0),
                        functools.partial(bisect_body, BISECT_HEAD), st)
    st = lax.while_loop(bisect_cond, functools.partial(bisect_body, BISECT_UNROLL), st)
    thr, tie = st[4], st[6]
    n_above = jnp.where(at_zero, c1, st[7])

    @pl.when(jnp.sum(tie) > 0.0)
    def _():
        need = jnp.where(tie > 0.0, k_sel - n_above, FLT_MAX)

        def drop_body(c, before):
            tr = tri_ref.shape[0]
            for sb in range(ck // tr):
                off = pl.multiple_of(c * ck + sb * tr, tr)
                blk = sc_ref[pl.ds(off, tr), :]
                eq = blk == thr
                eqf = jnp.where(eq, 1.0, 0.0)
                rank = before + jnp.dot(tri_ref[...], eqf.astype(BF16), preferred_element_type=F32)
                sc_ref[pl.ds(off, tr), :] = jnp.where(jnp.logical_and(eq, rank >= need), -jnp.inf, blk)
                before = before + jnp.sum(jnp.sum(eqf.reshape(tr // FOLD_ROWS, FOLD_ROWS, qb), axis=0),
                                          axis=0, keepdims=True)
            return before

        lax.fori_loop(0, nck, drop_body, jnp.zeros((1, qb), F32))

    qt = qt_ref[0]
    rhs = [jnp.concatenate(
        [qt[(r * GROUPS + g) * HEAD_DIM:(r * GROUPS + g + 1) * HEAD_DIM] for g in range(GROUPS)],
        axis=1) for r in range(ATTN_KV_HEADS)]

    n_att = (t0 + qb + ATT_CK - 1) // ATT_CK

    def step_offset(step):
        return pl.multiple_of(jnp.minimum(step, n_att - 1) * ATT_CK, ATT_CK)

    def qk_logits(step, buf):
        off = step_offset(step)
        thr_step = jnp.where(step < n_att, thr, jnp.inf)
        bias = jnp.where(sc_ref[pl.ds(off, ATT_CK), :] >= thr_step, 0.0, MASKED)
        for r in range(ATTN_KV_HEADS):
            lg = jnp.dot(k_ref[0, r, pl.ds(off, ATT_CK), :], rhs[r], preferred_element_type=F32)
            for g in range(GROUPS):
                buf[r, :, g * qb:(g + 1) * qb] = lg[:, g * qb:(g + 1) * qb] + bias

    def softmax_pv(step, buf, carry):
        off = step_offset(step)
        out = []
        for r in range(ATTN_KV_HEADS):
            m, acc = carry[r]
            m_new = jnp.maximum(m, jnp.max(buf[r], axis=0, keepdims=True))
            p = jnp.exp2(buf[r] - m_new).astype(BF16)
            pv = jnp.dot(vt_ref[0, r, :, pl.ds(off, ATT_CK)], p, preferred_element_type=F32)
            out.append((m_new, jnp.exp2(m - m_new) * acc + pv))
        return tuple(out)

    def att_body(i, carry):
        qk_logits(2 * i + 1, sb_ref)
        carry = softmax_pv(2 * i, sa_ref, carry)
        qk_logits(2 * i + 2, sa_ref)
        return softmax_pv(2 * i + 1, sb_ref, carry)

    init_a = tuple((jnp.full((1, GROUPS * qb), MASKED, F32), jnp.zeros((VT_ROWS, GROUPS * qb), F32))
                   for _ in range(ATTN_KV_HEADS))
    qk_logits(0, sa_ref)
    fin = lax.fori_loop(0, (n_att + 1) // 2, att_body, init_a)
    outs = []
    for r in range(ATTN_KV_HEADS):
        acc = fin[r][1]
        o = acc[0:HEAD_DIM] / acc[HEAD_DIM:HEAD_DIM + 1]
        outs.extend(o[:, g * qb:(g + 1) * qb] for g in range(GROUPS))
    o_ref[0] = jnp.concatenate(outs, axis=0).T.astype(BF16)


def _attn_call(qt, k, vt, qit, ki, wit):
    b, _, s = qt.shape
    n_sel = min(IDX_TOPK_MAX, s // 4)
    ck = min(512, s)
    grid = (b, s // QUERY_BLOCK)
    in_specs = [
        pl.BlockSpec((1, ATTN_WIDTH, QUERY_BLOCK), lambda i, j: (i, 0, j)),
        pl.BlockSpec((1, IDX_HEADS * IDX_DIM, QUERY_BLOCK), lambda i, j: (i, 0, j)),
        pl.BlockSpec((1, SUBLANES, QUERY_BLOCK), lambda i, j: (i, 0, j)),
        pl.BlockSpec((1, s, IDX_DIM), lambda i, j: (i, 0, 0)),
        pl.BlockSpec((1, ATTN_KV_HEADS, s, HEAD_DIM), lambda i, j: (i, 0, 0, 0)),
        pl.BlockSpec((1, ATTN_KV_HEADS, VT_ROWS, s), lambda i, j: (i, 0, 0, 0)),
        pl.BlockSpec((min(TIE_ROWS, ck), min(TIE_ROWS, ck)), lambda i, j: (0, 0)),
    ]
    tri = jnp.asarray(np.tril(np.ones((min(TIE_ROWS, ck), min(TIE_ROWS, ck)), np.float32), -1), BF16)
    return pl.pallas_call(
        functools.partial(_attn_kernel, n_sel=n_sel, ck=ck),
        grid=grid, in_specs=in_specs,
        out_specs=pl.BlockSpec((1, QUERY_BLOCK, ATTN_WIDTH), lambda i, j: (i, j, 0)),
        out_shape=jax.ShapeDtypeStruct((b, s, ATTN_WIDTH), BF16),
        scratch_shapes=[pltpu.VMEM((s, QUERY_BLOCK), F32),
                        pltpu.VMEM((ATTN_KV_HEADS, ATT_CK, GROUPS * QUERY_BLOCK), F32),
                        pltpu.VMEM((ATTN_KV_HEADS, ATT_CK, GROUPS * QUERY_BLOCK), F32)],
        compiler_params=_cparams("parallel", "parallel"), name="attn",
    )(qt, qit, wit, ki, k, vt, tri)


def _merge_kernel(x_ref, attn_ref, pool_ref, gate_ref, wba_ref, mix_ref, pscale_ref, wbp_ref, wout_ref,
                  g2_ref, rw_ref, rb_ref, tri_ref,
                  o_ref, h_ref, meta_ref, rgate_ref, cnt_ref, ext_ref, carry_ref):
    tm = x_ref.shape[1]
    j = pl.program_id(1)

    @pl.when(j == 0)
    def _():
        ext_ref[0:POOL_HALO] = jnp.zeros((POOL_HALO, POOL_WIDTH), F32)

    @pl.when(j > 0)
    def _():
        ext_ref[0:POOL_HALO] = ext_ref[tm:tm + POOL_HALO]

    ext_ref[POOL_HALO:POOL_HALO + tm] = pool_ref[0]
    t = j * tm + lax.broadcasted_iota(I32, (tm, 1), 0)
    mixed = []
    for g, w in enumerate(POOL_WINDOWS):
        cols = slice(g * POOL_GROUP_DIM, (g + 1) * POOL_GROUP_DIM)
        cur = ext_ref[POOL_HALO:POOL_HALO + tm, cols]
        wsum = cur
        for i in range(1, w):
            wsum = wsum + ext_ref[POOL_HALO - i:POOL_HALO - i + tm, cols]
        cnt = jnp.minimum(t + 1, w).astype(F32)
        dev = (wsum / cnt - cur).astype(BF16)
        mixed.append(jnp.dot(dev, mix_ref[g], preferred_element_type=F32))
    pooled = (jnp.concatenate(mixed, axis=1) * pscale_ref[...]).astype(BF16)
    branch_a = jnp.dot(attn_ref[0], wba_ref[...], preferred_element_type=F32)
    branch_p = jnp.dot(pooled, wbp_ref[...], preferred_element_type=F32)
    gates = gate_ref[0]
    merged = (jax.nn.sigmoid(gates[:, 0:D_MODEL]) * branch_a
              + jax.nn.sigmoid(gates[:, D_MODEL:2 * D_MODEL]) * branch_p)
    x1 = x_ref[0] + jnp.dot(merged.astype(BF16), wout_ref[...], preferred_element_type=F32)
    o_ref[0] = x1
    _route_tile(x1, jnp.logical_and(pl.program_id(0) == 0, j == 0), g2_ref, rw_ref, rb_ref, tri_ref,
                h_ref, meta_ref, rgate_ref, cnt_ref, carry_ref)


def _merge_call(x, attn, pool, gates, wba, mix, pscale, wbp, wout, g2, rw, rb):
    b, s, d = x.shape
    tm = min(512, s)
    nt = s // tm
    t = b * s
    tile = lambda w: pl.BlockSpec((1, tm, w), lambda i, j: (i, j, 0))
    flat = lambda rows, w: pl.BlockSpec((rows, w), lambda i, j: (i * nt + j, 0))
    full2 = lambda i, j: (0, 0)
    tri = jnp.asarray(np.tril(np.ones((tm, tm), np.float32), -1), BF16)
    in_specs = [
        tile(d), tile(ATTN_WIDTH), tile(POOL_WIDTH), tile(2 * D_MODEL),
        pl.BlockSpec((ATTN_WIDTH, d), full2),
        pl.BlockSpec((len(POOL_WINDOWS), POOL_GROUP_DIM, POOL_GROUP_DIM), lambda i, j: (0, 0, 0)),
        pl.BlockSpec((1, POOL_WIDTH), full2),
        pl.BlockSpec((POOL_WIDTH, d), full2),
        pl.BlockSpec((d, d), full2),
        pl.BlockSpec((1, d), full2),
        pl.BlockSpec((d, LANES), full2),
        pl.BlockSpec((1, LANES), full2),
        pl.BlockSpec((tm, tm), full2),
    ]
    out_shape = (
        jax.ShapeDtypeStruct((b, s, d), F32),
        jax.ShapeDtypeStruct((t * ROW_TILES, LANES), F32),
        jax.ShapeDtypeStruct((2 * TOP_K, t), I32),
        jax.ShapeDtypeStruct((t, LANES), F32),
        jax.ShapeDtypeStruct((1, LANES), I32),
    )
    out_specs = (
        tile(d), flat(tm * ROW_TILES, LANES),
        pl.BlockSpec((2 * TOP_K, tm), lambda i, j: (0, i * nt + j)),
        flat(tm, LANES), pl.BlockSpec((1, LANES), full2),
    )
    return pl.pallas_call(
        _merge_kernel, grid=(b, nt), in_specs=in_specs, out_specs=out_specs, out_shape=out_shape,
        scratch_shapes=[pltpu.VMEM((POOL_HALO + tm, POOL_WIDTH), F32), pltpu.VMEM((1, LANES), F32)],
        compiler_params=_cparams("arbitrary", "arbitrary"), name="merge",
    )(x, attn, pool, gates, wba, mix, pscale, wbp, wout, g2, rw, rb, tri)


ROW_TILES = D_MODEL // LANES


def _load_rows(ref, n, *lead):
    return jnp.concatenate(
        [ref[(*lead, pl.ds(c, n, stride=ROW_TILES), slice(None))] for c in range(ROW_TILES)], axis=1)


def _store_rows(ref, val):
    for c in range(ROW_TILES):
        ref[pl.ds(c, val.shape[0], stride=ROW_TILES), :] = val[:, c * LANES:(c + 1) * LANES]


def _row_tile(ref, i):
    return ref.at[pl.ds(pl.multiple_of(i * ROW_TILES, ROW_TILES), ROW_TILES)]


def _route_tile(x, first_tile, g2_ref, rw_ref, rb_ref, tri_ref, h_ref, meta_ref, gate_ref, cnt_ref, carry_ref):
    tm = x.shape[0]

    @pl.when(first_tile)
    def _():
        carry_ref[...] = jnp.zeros((1, LANES), F32)

    ms = jnp.mean(x * x, axis=-1, keepdims=True)
    h = x * lax.rsqrt(ms + NORM_EPS) * g2_ref[...]
    _store_rows(h_ref, h)
    logits = jnp.dot(h.astype(BF16), rw_ref[...], preferred_element_type=F32) + rb_ref[...]
    lane = lax.broadcasted_iota(I32, (tm, LANES), 1).astype(F32)
    work = jnp.where(lane < N_EXPERTS, logits, -jnp.inf)
    vals, idxs = [], []
    for _ in range(TOP_K):
        m = jnp.max(work, axis=-1, keepdims=True)
        idx = jnp.min(jnp.where(work == m, lane, float(LANES)), axis=-1, keepdims=True)
        vals.append(m)
        idxs.append(idx)
        work = jnp.where(lane == idx, -jnp.inf, work)
    exps = [jnp.exp(v - vals[0]) for v in vals]
    denom = exps[0] + exps[1] + exps[2] + exps[3]
    member = jnp.zeros((tm, LANES), F32)
    for idx in idxs:
        member = member + jnp.where(lane == idx, 1.0, 0.0)
    before = jnp.dot(tri_ref[...], member.astype(BF16), preferred_element_type=F32) + carry_ref[...]
    meta = jnp.zeros((tm, LANES), F32)
    gate = jnp.zeros((tm, LANES), F32)
    for k in range(TOP_K):
        rank = jnp.sum(jnp.where(lane == idxs[k], before, 0.0), axis=-1, keepdims=True)
        meta = jnp.where(lane == float(k), idxs[k], meta)
        meta = jnp.where(lane == float(TOP_K + k), rank, meta)
        gate = jnp.where(lane == float(k), exps[k] / denom, gate)
    meta_ref[...] = meta.T[0:2 * TOP_K].astype(I32)
    gate_ref[...] = gate
    total = carry_ref[...] + jnp.sum(member, axis=0, keepdims=True)
    carry_ref[...] = total
    cnt_ref[...] = total.astype(I32)


ROW_UNROLL = SUBLANES
ROW_DMA_TOKENS = 256
DISPATCH_TOKENS = 512


def _dispatch_kernel(slot_ref, h_ref, xs_ref, sem, *, tm):
    for sub in range(tm // ROW_DMA_TOKENS):
        def issue(i, carry, sub=sub):
            for u in range(ROW_UNROLL):
                j = i * ROW_UNROLL + u
                for k in range(TOP_K):
                    slot = slot_ref[(sub * TOP_K + k) * ROW_DMA_TOKENS + j]
                    pltpu.make_async_copy(_row_tile(h_ref, sub * ROW_DMA_TOKENS + j), _row_tile(xs_ref, slot),
                                          sem).start(priority=k % 2)
            return carry

        lax.fori_loop(0, ROW_DMA_TOKENS // ROW_UNROLL, issue, 0)

    for k in range(TOP_K):
        pltpu.make_async_copy(h_ref, xs_ref.at[pl.ds(0, tm * ROW_TILES)], sem).wait()


def _dispatch_call(slots, h2):
    t = h2.shape[0] // ROW_TILES
    tm = min(DISPATCH_TOKENS, t)
    return pl.pallas_call(
        functools.partial(_dispatch_kernel, tm=tm),
        grid=(t // tm,),
        in_specs=[pl.BlockSpec((tm * TOP_K,), lambda i: (i,), memory_space=pltpu.SMEM),
                  pl.BlockSpec((tm * ROW_TILES, LANES), lambda i: (i, 0))],
        out_specs=pl.BlockSpec(memory_space=pl.ANY),
        scratch_shapes=[pltpu.SemaphoreType.DMA(())],
        out_shape=jax.ShapeDtypeStruct((t * TOP_K * ROW_TILES, LANES), F32),
        compiler_params=_cparams("arbitrary"), name="dispatch",
    )(slots, h2)


def _ffn_kernel(item_e_ref, item_blk_ref, nact_ref, gstart_ref, gend_ref,
                xs_ref, wgu_ref, bgu_ref, wd_ref, bd_ref, o_ref, wgu_bf, wd_bf):
    w = pl.program_id(0)
    e = item_e_ref[w]
    blk = item_blk_ref[w]
    prev = jnp.maximum(w - 1, 0)
    active = w < nact_ref[0]
    new_expert = jnp.logical_or(w == 0, e != item_e_ref[prev])
    first_visit = jnp.logical_or(w == 0, blk != item_blk_ref[prev])

    @pl.when(jnp.logical_and(active, new_expert))
    def _():
        wgu_bf[...] = wgu_ref[0].astype(BF16)
        wd_bf[...] = wd_ref[0].astype(BF16)

    @pl.when(active)
    def _():
        xb = _load_rows(xs_ref, ROW_BLOCK).astype(BF16)
        gate = jnp.dot(xb, wgu_bf[:, 0:EXPERT_DIM], preferred_element_type=F32) + bgu_ref[0, :, 0:EXPERT_DIM]
        up = (jnp.dot(xb, wgu_bf[:, EXPERT_DIM:2 * EXPERT_DIM], preferred_element_type=F32)
              + bgu_ref[0, :, EXPERT_DIM:2 * EXPERT_DIM])
        gate = jnp.minimum(gate, SWIGLU_LIMIT)
        up = jnp.clip(up, -SWIGLU_LIMIT, SWIGLU_LIMIT)
        act = gate * jax.nn.sigmoid(SWIGLU_ALPHA * gate) * (up + 1.0)
        res = jnp.dot(act.astype(BF16), wd_bf[...], preferred_element_type=F32) + bd_ref[0]
        row = blk * ROW_BLOCK + lax.broadcasted_iota(I32, (ROW_BLOCK, 1), 0)
        mine = jnp.logical_and(row >= gstart_ref[e], row < gend_ref[e])

        @pl.when(first_visit)
        def _():
            _store_rows(o_ref, jnp.where(mine, res, 0.0))

        @pl.when(jnp.logical_not(first_visit))
        def _():
            _store_rows(o_ref, jnp.where(mine, res, _load_rows(o_ref, ROW_BLOCK)))


def _ffn_call(item_e, item_blk, nact, gstart, gend, xs, wgu, bgu, wd, bd):
    d = D_MODEL
    rows = lambda w, ie, ib, *_: (ib[w], 0)
    exp3 = lambda w, ie, *_: (ie[w], 0, 0)
    return pl.pallas_call(
        _ffn_kernel,
        grid_spec=pltpu.PrefetchScalarGridSpec(
            num_scalar_prefetch=5, grid=(item_e.shape[0],),
            in_specs=[pl.BlockSpec((ROW_BLOCK * ROW_TILES, LANES), rows),
                      pl.BlockSpec((1, d, 2 * EXPERT_DIM), exp3),
                      pl.BlockSpec((1, 1, 2 * EXPERT_DIM), exp3),
                      pl.BlockSpec((1, EXPERT_DIM, d), exp3),
                      pl.BlockSpec((1, 1, d), exp3)],
            out_specs=pl.BlockSpec((ROW_BLOCK * ROW_TILES, LANES), rows),
            scratch_shapes=[pltpu.VMEM((d, 2 * EXPERT_DIM), BF16), pltpu.VMEM((EXPERT_DIM, d), BF16)]),
        out_shape=jax.ShapeDtypeStruct(xs.shape, F32),
        compiler_params=_cparams("arbitrary"), name="ffn",
    )(item_e, item_blk, nact, gstart, gend, xs, wgu, bgu, wd, bd)


def _combine_kernel(slot_ref, slot_next_ref, ys_ref, gate_ref, x_ref, o_ref, buf_ref, sem, *, tm, n_steps):
    i = pl.program_id(0)
    half = i % 2

    def gather(slots, dst_half):
        def issue(it, carry):
            for u in range(ROW_UNROLL):
                j = it * ROW_UNROLL + u
                for k in range(TOP_K):
                    pltpu.make_async_copy(_row_tile(ys_ref, slots[k * tm + j]),
                                          _row_tile(buf_ref.at[dst_half, k], j),
                                          sem.at[dst_half]).start(priority=k % 2)
            return carry
        lax.fori_loop(0, tm // ROW_UNROLL, issue, 0)

    @pl.when(i == 0)
    def _():
        gather(slot_ref, 0)

    @pl.when(i + 1 < n_steps)
    def _():
        gather(slot_next_ref, 1 - half)

    for k in range(TOP_K):
        pltpu.make_async_copy(ys_ref.at[pl.ds(0, tm * ROW_TILES)], buf_ref.at[half, k], sem.at[half]).wait()
    gate = gate_ref[...]
    y = x_ref[...]
    for k in range(TOP_K):
        y = y + gate[:, k:k + 1] * _load_rows(buf_ref, tm, half, k)
    o_ref[...] = y


def _combine_call(slots, ys, gates, x1):
    t, d = x1.shape
    tm = ROW_DMA_TOKENS
    n_steps = t // tm
    tile = lambda w: pl.BlockSpec((tm, w), lambda i: (i, 0))
    return pl.pallas_call(
        functools.partial(_combine_kernel, tm=tm, n_steps=n_steps),
        grid=(n_steps,),
        in_specs=[pl.BlockSpec((tm * TOP_K,), lambda i: (i,), memory_space=pltpu.SMEM),
                  pl.BlockSpec((tm * TOP_K,), lambda i: (jnp.minimum(i + 1, n_steps - 1),),
                               memory_space=pltpu.SMEM),
                  pl.BlockSpec(memory_space=pl.ANY), tile(LANES), tile(d)],
        out_specs=tile(d),
        scratch_shapes=[pltpu.VMEM((2, TOP_K, tm * ROW_TILES, LANES), F32), pltpu.SemaphoreType.DMA((2,))],
        out_shape=jax.ShapeDtypeStruct((t, d), F32),
        compiler_params=_cparams("arbitrary"), name="combine",
    )(slots, slots, ys, gates, x1)


def _const_rows(q_g, k_g, i_g, i_b):
    zeros = jnp.zeros((LANES - IDX_DIM,), F32)
    rows = [
        jnp.tile(q_g.astype(F32), LANES // HEAD_DIM),
        jnp.tile(k_g.astype(F32), LANES // HEAD_DIM),
        jnp.concatenate([i_g.astype(F32), zeros]),
        jnp.concatenate([i_b.astype(F32), zeros]),
    ]
    return jnp.concatenate([jnp.stack(rows, axis=0), jnp.zeros((SUBLANES - len(rows), LANES), F32)], axis=0)


def _rope_freqs():
    inv_m = ROPE_THETA ** (-jnp.arange(0, HEAD_DIM, 2, dtype=F32) / HEAD_DIM)
    inv_i = ROPE_THETA ** (-jnp.arange(0, IDX_ROPE_DIM, 2, dtype=F32) / IDX_ROPE_DIM)
    pad = jnp.zeros((ROPE_ROWS - inv_m.shape[0] - inv_i.shape[0],), F32)
    return jnp.concatenate([inv_m, inv_i, pad]).reshape(ROPE_ROWS, 1)


def _ffn_schedule(counts, n_blocks):
    gend = jnp.cumsum(counts).astype(I32)
    gstart = gend - counts
    first_blk = gstart // ROW_BLOCK
    last_blk = (jnp.maximum(gend, 1) - 1) // ROW_BLOCK
    n_items = jnp.where(counts > 0, last_blk - first_blk + 1, 0)
    item_end = jnp.cumsum(n_items).astype(I32)
    item_start = item_end - n_items
    total = item_end[-1]
    w = jnp.minimum(jnp.arange(n_blocks + N_EXPERTS - 1, dtype=I32), total - 1)
    item_e = jnp.sum((item_end[None, :] <= w[:, None]).astype(I32), axis=1)
    mine = item_e[:, None] == jnp.arange(N_EXPERTS, dtype=I32)[None, :]
    item_blk = w + jnp.sum(jnp.where(mine, (first_blk - item_start)[None, :], 0), axis=1)
    return gstart, gend, item_e, item_blk, total.reshape(1)


def _layer(x, positions, norm1_g, w_in, q_norm_g, k_norm_g, idx_k_norm_g, idx_k_norm_b, w_branch_attn,
           pool_mix_w, pool_scale, w_branch_pool, w_out, norm2_g, router_w, router_b, w_gate_up,
           b_gate_up, w_down, b_down):
    b, s, d = x.shape
    t = b * s
    assert (t * TOP_K) % ROW_BLOCK == 0 and s % QUERY_BLOCK == 0
    lane = np.arange(LANES)
    gsum = jnp.asarray(lane[:, None] // HEAD_DIM == lane[None, :] // HEAD_DIM, BF16)

    qt, k, vt, qit, ki, wit, pool, gates = _proj_call(
        x, positions.reshape(b, 1, s), norm1_g.reshape(1, d), w_in,
        _const_rows(q_norm_g, k_norm_g, idx_k_norm_g, idx_k_norm_b), _rope_freqs(), gsum)
    attn = _attn_call(qt, k, vt, qit, ki, wit)
    rw = jnp.pad(router_w, ((0, 0), (0, LANES - N_EXPERTS))).astype(BF16)
    rb = jnp.pad(router_b, (0, LANES - N_EXPERTS)).reshape(1, LANES)
    x1, h2, meta, rgate, counts = _merge_call(
        x, attn, pool, gates, w_branch_attn.astype(BF16), pool_mix_w.astype(BF16),
        pool_scale.reshape(1, POOL_WIDTH), w_branch_pool.astype(BF16), w_out.astype(BF16),
        norm2_g.reshape(1, d), rw, rb)
    x1 = x1.reshape(t, d)

    gstart, gend, item_e, item_blk, nact = _ffn_schedule(counts[0, :N_EXPERTS], t * TOP_K // ROW_BLOCK)
    seg = jnp.sum(jnp.where(meta[None, 0:TOP_K] == jnp.arange(N_EXPERTS, dtype=I32)[:, None, None],
                            gstart[:, None, None], 0), axis=0)
    slots = (seg + meta[TOP_K:2 * TOP_K]).reshape(TOP_K, t // ROW_DMA_TOKENS, ROW_DMA_TOKENS)
    slots = slots.transpose(1, 0, 2).reshape(t * TOP_K)
    xs = _dispatch_call(slots, h2)
    ys = _ffn_call(item_e, item_blk, nact, gstart, gend, xs, w_gate_up,
                   b_gate_up.reshape(N_EXPERTS, 1, 2 * EXPERT_DIM), w_down, b_down.reshape(N_EXPERTS, 1, d))
    out = _combine_call(slots, ys, rgate, x1)
    return out.reshape(b, s, d)


def kernel(x, positions, norm1_g, w_in, q_norm_g, k_norm_g, idx_k_norm_g, idx_k_norm_b, w_branch_attn,
           pool_mix_w, pool_scale, w_branch_pool, w_out, norm2_g, router_w, router_b, w_gate_up,
           b_gate_up, w_down, b_down):
    for l in range(norm1_g.shape[0]):
        x = _layer(x, positions, norm1_g[l], w_in[l], q_norm_g[l], k_norm_g[l], idx_k_norm_g[l],
                   idx_k_norm_b[l], w_branch_attn[l], pool_mix_w[l], pool_scale[l], w_branch_pool[l],
                   w_out[l], norm2_g[l], router_w[l], router_b[l], w_gate_up[l], b_gate_up[l],
                   w_down[l], b_down[l])
    return x
```
